```python
import math
import jax, jax.numpy as jnp
from jax import lax
import numpy as np

D_MODEL = 1024
BATCH = 16
SEQ = 256
DEPTH = 4
DEC_BATCH = 2
DEC_SEQ = 1024
PAST_LEN = 256

GRID_W = 64
EPS = 1e-6
D_FF = 2816
N_MOD = 9
N_EVEN = (DEPTH + 1) // 2
N_ODD = DEPTH // 2
POOL_SIZES = (2, 4, 8, 16)
POOL_GROUPS = 4
POOL_CH = 64
A_W = POOL_GROUPS * POOL_CH
DN_HEADS = 6
DN_DK = 128
DN_DV = 128
DN_CONV = 3
DN_CHUNK = 64
QK_W = DN_HEADS * DN_DK
B_W = DN_HEADS * DN_DV
CONV_CH = 2 * QK_W + B_W
P_EVEN = A_W + CONV_CH + B_W + 4 * DN_HEADS
O_EVEN = A_W + B_W
SGU_CHUNK = 128
SGU_HEADS = 6
SGU_CH = 128
C_W = SGU_HEADS * SGU_CH
FN_GROUPS = 4
FN_CH = 64
F_W = FN_GROUPS * FN_CH
P_ODD = 2 * C_W + F_W
O_ODD = C_W + F_W

kernel_name = 'hybrid_prefix_diffusion_step'

F32 = jnp.float32


def _rms_norm(x, w):
    xf = x.astype(F32)
    y = xf * lax.rsqrt(jnp.mean(xf * xf, axis=-1, keepdims=True) + EPS)
    return (y * w.astype(F32)).astype(x.dtype)


def _modulate(h, shift, scale):
    return h * (1 + scale) + shift


def _swiglu(h, w_gate, w_up, w_down):
    return (jax.nn.silu(h @ w_gate) * (h @ w_up)) @ w_down


def _l2norm(x):
    return x * lax.rsqrt(jnp.sum(x * x, axis=-1, keepdims=True) + EPS)


def _grid_pos_embed(n_tok, dtype):
    rows = n_tok // GRID_W
    r = jnp.broadcast_to(jnp.arange(rows, dtype=F32)[:, None], (rows, GRID_W)).reshape(-1)
    col = jnp.broadcast_to(jnp.arange(GRID_W, dtype=F32)[None, :], (rows, GRID_W)).reshape(-1)
    quarter = D_MODEL // 4
    freq = jnp.exp(-math.log(10000.0) * jnp.arange(quarter, dtype=F32) / quarter)

    def emb(p):
        ang = p[:, None] * freq[None, :]
        return jnp.concatenate([jnp.sin(ang), jnp.cos(ang)], axis=-1)

    return jnp.concatenate([emb(r), emb(col)], axis=-1).astype(dtype)


def _pool_mixer(p, w, scale):
    bn, n, _ = p.shape
    pf = p.astype(F32).reshape(bn, n, POOL_GROUPS, POOL_CH)
    csum = jnp.concatenate([jnp.zeros_like(pf[:, :1]), jnp.cumsum(pf, axis=1)], axis=1)
    t = jnp.arange(n)
    outs = []
    for gi, size in enumerate(POOL_SIZES):
        lo = jnp.clip(t - size // 2, 0, n)
        hi = jnp.clip(t + size // 2, 0, n)
        cnt = (hi - lo).astype(F32)[None, :, None]
        cg = csum[:, :, gi]
        outs.append((cg[:, hi] - cg[:, lo]) / cnt - pf[:, :, gi])
    d = jnp.stack(outs, axis=2)
    y = jnp.einsum('blgc,gce->blge', d, w.astype(F32)).reshape(bn, n, A_W)
    return (y * scale.astype(F32)).astype(p.dtype)


def _short_conv(x, w):
    n = x.shape[1]
    pad = DN_CONV // 2
    xp = jnp.pad(x, ((0, 0), (pad, pad), (0, 0)))
    y = xp[:, 0:n] * w[0]
    for tap in range(1, DN_CONV):
        y = y + xp[:, tap:tap + n] * w[tap]
    return y


def _gated_delta_chunked(q, k, v, g, beta, s0):
    bn, n, nh, dk = q.shape
    dv = v.shape[-1]
    nc = n // DN_CHUNK
    cs = DN_CHUNK

    def chunks(t):
        t = t.reshape((bn, nc, cs, nh) + t.shape[3:])
        return jnp.moveaxis(t, 3, 1)

    q, k, v, g, beta = (chunks(t) for t in (q, k, v, g, beta))
    q = q * (dk ** -0.5)
    gc = jnp.cumsum(g, axis=-1)
    idx = jnp.arange(cs)
    lower = idx[:, None] >= idx[None, :]
    strict = idx[:, None] > idx[None, :]
    decay = jnp.exp(jnp.where(lower, gc[..., :, None] - gc[..., None, :], -jnp.inf))
    kb = k * beta[..., None]
    m = jnp.where(strict, jnp.einsum('bhnik,bhnjk->bhnij', kb, k) * decay, 0.0)
    eye = jnp.eye(cs, dtype=F32)
    tinv = lax.linalg.triangular_solve(eye + m, jnp.broadcast_to(eye, m.shape),
                                       left_side=True, lower=True)
    u = tinv @ (v * beta[..., None])
    w = tinv @ (kb * jnp.exp(gc)[..., None])
    a_intra = jnp.einsum('bhnik,bhnjk->bhnij', q, k) * decay
    q_dec = q * jnp.exp(gc)[..., None]
    k_dec = k * jnp.exp(gc[..., -1:] - gc)[..., None]
    g_last = jnp.exp(gc[..., -1])

    def step(s, inp):
        q_i, k_i, u_i, w_i, a_i, gl_i = inp
        v_new = u_i - w_i @ s
        o = q_i @ s + a_i @ v_new
        s = s * gl_i[..., None, None] + jnp.einsum('bhck,bhcv->bhkv', k_i, v_new)
        return s, o

    xs = tuple(jnp.moveaxis(t, 2, 0) for t in (q_dec, k_dec, u, w, a_intra, g_last))
    s_final, o = lax.scan(step, s0, xs)
    o = jnp.moveaxis(jnp.moveaxis(o, 0, 2), 1, 3).reshape(bn, n, nh, dv)
    return o, s_final


def _deltanet_mixer(qkv, z, b_f, b_b, a_f, a_b, conv_w, a_log, dt_bias, norm_w, s0):
    bn, n, _ = qkv.shape
    hq = jax.nn.silu(_short_conv(qkv.astype(F32), conv_w.astype(F32)))
    q, k, v = jnp.split(hq, [QK_W, 2 * QK_W], axis=-1)
    q = _l2norm(q.reshape(bn, n, DN_HEADS, DN_DK))
    k = _l2norm(k.reshape(bn, n, DN_HEADS, DN_DK))
    v = v.reshape(bn, n, DN_HEADS, DN_DV)
    a_log = a_log.astype(F32)
    dt_bias = dt_bias.astype(F32)
    s0 = s0.astype(F32)
    g_f = -jnp.exp(a_log[0]) * jax.nn.softplus(a_f.astype(F32) + dt_bias[0])
    g_b = -jnp.exp(a_log[1]) * jax.nn.softplus(a_b.astype(F32) + dt_bias[1])
    beta_f = jax.nn.sigmoid(b_f.astype(F32))
    beta_b = jax.nn.sigmoid(b_b.astype(F32))
    o_f, s_f = _gated_delta_chunked(q, k, v, g_f, beta_f, s0[:, 0])
    rev = lambda t: jnp.flip(t, axis=1)
    o_b, s_b = _gated_delta_chunked(rev(q), rev(k), rev(v), rev(g_b), rev(beta_b), s0[:, 1])
    o = o_f + rev(o_b)
    o = o * lax.rsqrt(jnp.mean(o * o, axis=-1, keepdims=True) + EPS) * norm_w.astype(F32)
    o = o * jax.nn.silu(z.astype(F32).reshape(bn, n, DN_HEADS, DN_DV))
    return o.reshape(bn, n, B_W).astype(qkv.dtype), jnp.stack([s_f, s_b], axis=1)


def _sgu_mixer(uv, norm_w, w_s, b_s):
    bn, n, _ = uv.shape
    u, v = jnp.split(jax.nn.gelu(uv), 2, axis=-1)
    vf = v.astype(F32)
    mu = jnp.mean(vf, axis=-1, keepdims=True)
    var = jnp.mean(jnp.square(vf - mu), axis=-1, keepdims=True)
    vn = (vf - mu) * lax.rsqrt(var + EPS) * norm_w.astype(F32)
    vn = vn.reshape(bn, n // SGU_CHUNK, SGU_CHUNK, SGU_HEADS, SGU_CH)
    mixed = jnp.einsum('hps,bnshc->bnphc', w_s.astype(F32), vn) + b_s.astype(F32).T[:, :, None]
    return u * mixed.reshape(bn, n, C_W).astype(u.dtype)


def _fourier_mixer(f, w):
    bn, n, _ = f.shape
    ff = f.astype(F32).reshape(bn, n, FN_GROUPS, FN_CH)
    spec = jnp.fft.fft2(ff, axes=(1, 3), norm='ortho').real
    y = jnp.einsum('blgc,gce->blge', spec, w.astype(F32)).reshape(bn, n, F_W)
    return y.astype(f.dtype)


def _trunk(x, cond, init_state, ffn1_norm, ffn1_w_gate, ffn1_w_up, ffn1_w_down, mix_norm,
           ffn2_norm, ffn2_w_gate, ffn2_w_up, ffn2_w_down, ada_w, ada_b, ev_w_in, ev_w_out,
           pool_w, pool_scale, dn_conv_w, dn_a_log, dn_dt_bias, dn_norm_w, od_w_in, od_w_out,
           sgu_norm, sgu_w, sgu_b, fnet_w, final_norm):
    ev_splits = [A_W, A_W + CONV_CH, A_W + CONV_CH + B_W, A_W + CONV_CH + B_W + DN_HEADS,
                 A_W + CONV_CH + B_W + 2 * DN_HEADS, A_W + CONV_CH + B_W + 3 * DN_HEADS]
    final_states = []
    for l in range(DEPTH):
        mod = (jax.nn.silu(cond) @ ada_w[l] + ada_b[l])[:, None, :]
        sh1, sc1, g1, sh2, sc2, g2, sh3, sc3, g3 = jnp.split(mod, N_MOD, axis=-1)
        h = _modulate(_rms_norm(x, ffn1_norm[l]), sh1, sc1)
        x = x + 0.5 * g1 * _swiglu(h, ffn1_w_gate[l], ffn1_w_up[l], ffn1_w_down[l])
        h = _modulate(_rms_norm(x, mix_norm[l]), sh2, sc2)
        if l % 2 == 0:
            e = l // 2
            p_a, qkv, z, b_f, b_b, a_f, a_b = jnp.split(h @ ev_w_in[e], ev_splits, axis=-1)
            y_a = _pool_mixer(p_a, pool_w[e], pool_scale[e])
            y_b, s_fin = _deltanet_mixer(qkv, z, b_f, b_b, a_f, a_b, dn_conv_w[e], dn_a_log[e],
                                         dn_dt_bias[e], dn_norm_w[e], init_state[:, e])
            final_states.append(s_fin)
            mix = jnp.concatenate([y_a, y_b], axis=-1) @ ev_w_out[e]
        else:
            j = l // 2
            uv, f = jnp.split(h @ od_w_in[j], [2 * C_W], axis=-1)
            y_c = _sgu_mixer(uv, sgu_norm[j], sgu_w[j], sgu_b[j])
            y_d = _fourier_mixer(f, fnet_w[j])
            mix = jnp.concatenate([y_c, y_d], axis=-1) @ od_w_out[j]
        x = x + g2 * mix
        h = _modulate(_rms_norm(x, ffn2_norm[l]), sh3, sc3)
        x = x + 0.5 * g3 * _swiglu(h, ffn2_w_gate[l], ffn2_w_up[l], ffn2_w_down[l])
    return _rms_norm(x, final_norm), jnp.stack(final_states, axis=1)


def setup_inputs(seed: int = 0) -> dict:
    key = jax.random.key(seed)
    ks = iter(jax.random.split(key, 40))

    def nrm(shape, scale):
        return jax.random.normal(next(ks), shape, F32) * scale

    def gain(shape):
        return 1.0 + 0.02 * jax.random.normal(next(ks), shape, F32)

    d = D_MODEL
    a_log = jnp.log(jax.random.uniform(next(ks), (N_EVEN, 2, DN_HEADS), F32, 1.0, 16.0))
    dt = jnp.exp(jax.random.uniform(next(ks), (N_EVEN, 2, DN_HEADS), F32)
                 * (math.log(0.1) - math.log(0.001)) + math.log(0.001))
    dt_bias = dt + jnp.log(-jnp.expm1(-dt))
    return {
        'x_prompt': nrm((BATCH, SEQ, d), 1.0),
        'x_sample': nrm((DEC_BATCH, DEC_SEQ, d), 1.0),
        'state_delta': nrm((DEC_BATCH, N_EVEN, 2, DN_HEADS, DN_DK, DN_DV), DN_DK ** -0.5),
        'c': nrm((DEC_BATCH, d), 1.0),
        'c_ctx': nrm((d,), 1.0),
        'ffn1_norm': gain((DEPTH, d)),
        'ffn1_w_gate': nrm((DEPTH, d, D_FF), d ** -0.5),
        'ffn1_w_up': nrm((DEPTH, d, D_FF), d ** -0.5),
        'ffn1_w_down': nrm((DEPTH, D_FF, d), D_FF ** -0.5),
        'mix_norm': gain((DEPTH, d)),
        'ffn2_norm': gain((DEPTH, d)),
        'ffn2_w_gate': nrm((DEPTH, d, D_FF), d ** -0.5),
        'ffn2_w_up': nrm((DEPTH, d, D_FF), d ** -0.5),
        'ffn2_w_down': nrm((DEPTH, D_FF, d), D_FF ** -0.5),
        'ada_w': nrm((DEPTH, d, N_MOD * d), 0.5 * d ** -0.5),
        'ada_b': nrm((DEPTH, N_MOD * d), 0.02),
        'ev_w_in': nrm((N_EVEN, d, P_EVEN), d ** -0.5),
        'ev_w_out': nrm((N_EVEN, O_EVEN, d), O_EVEN ** -0.5),
        'pool_w': nrm((N_EVEN, POOL_GROUPS, POOL_CH, POOL_CH), POOL_CH ** -0.5),
        'pool_scale': gain((N_EVEN, A_W)),
        'dn_conv_w': nrm((N_EVEN, DN_CONV, CONV_CH), DN_CONV ** -0.5),
        'dn_a_log': a_log,
        'dn_dt_bias': dt_bias,
        'dn_norm_w': gain((N_EVEN, DN_DV)),
        'od_w_in': nrm((N_ODD, d, P_ODD), d ** -0.5),
        'od_w_out': nrm((N_ODD, O_ODD, d), O_ODD ** -0.5),
        'sgu_norm': gain((N_ODD, C_W)),
        'sgu_w': nrm((N_ODD, SGU_HEADS, SGU_CHUNK, SGU_CHUNK), SGU_CHUNK ** -0.5),
        'sgu_b': nrm((N_ODD, SGU_HEADS, SGU_CHUNK), 0.02),
        'fnet_w': nrm((N_ODD, FN_GROUPS, FN_CH, FN_CH), FN_CH ** -0.5),
        'final_norm': gain((d,)),
    }


def reference(x_prompt, x_sample, state_delta, c, c_ctx, ffn1_norm, ffn1_w_gate, ffn1_w_up,
              ffn1_w_down, mix_norm, ffn2_norm, ffn2_w_gate, ffn2_w_up, ffn2_w_down, ada_w, ada_b,
              ev_w_in, ev_w_out, pool_w, pool_scale, dn_conv_w, dn_a_log, dn_dt_bias, dn_norm_w,
              od_w_in, od_w_out, sgu_norm, sgu_w, sgu_b, fnet_w, final_norm):
    weights = (ffn1_norm, ffn1_w_gate, ffn1_w_up, ffn1_w_down, mix_norm, ffn2_norm, ffn2_w_gate,
               ffn2_w_up, ffn2_w_down, ada_w, ada_b, ev_w_in, ev_w_out, pool_w, pool_scale,
               dn_conv_w, dn_a_log, dn_dt_bias, dn_norm_w, od_w_in, od_w_out, sgu_norm, sgu_w,
               sgu_b, fnet_w, final_norm)
    zero_state = jnp.zeros((x_prompt.shape[0], N_EVEN, 2, DN_HEADS, DN_DK, DN_DV), F32)
    y_prompt, new_state_delta = _trunk(x_prompt, c_ctx[None, :], zero_state, *weights)
    x_lat = x_sample + _grid_pos_embed(x_sample.shape[1], x_sample.dtype)[None]
    y_sample, _ = _trunk(x_lat, c, state_delta, *weights)
    return (y_prompt, y_sample, new_state_delta)
```

```python
import functools
import math

import jax
import jax.numpy as jnp
from jax import lax
from jax.experimental import pallas as pl
from jax.experimental.pallas import tpu as pltpu

F32 = jnp.float32
BF16 = jnp.bfloat16

D_MODEL = 1024
BATCH = 16
SEQ = 256
DEPTH = 4
DEC_BATCH = 2
DEC_SEQ = 1024
GRID_W = 64
EPS = 1e-6
D_FF = 2816
N_MOD = 9
N_EVEN = (DEPTH + 1) // 2
POOL_SIZES = (2, 4, 8, 16)
POOL_CH = 64
A_W = 256
DN_HEADS = 6
DN_DK = 128
QK_W = DN_HEADS * DN_DK
B_W = QK_W
CONV_CH = 3 * QK_W
P_EVEN = A_W + CONV_CH + B_W + 4 * DN_HEADS
SGU_CHUNK = 128
SGU_HEADS = 6
C_W = 768
FN_CH = 64
F_W = 256
P_ODD = 2 * C_W + F_W

CTX_TOK = BATCH * SEQ
LAT_TOK = DEC_BATCH * DEC_SEQ
N_TOK = CTX_TOK + LAT_TOK

LANES = 128
SUBLANES = 8
VMEM_LIMIT = 56 * 1024 * 1024

TM = 1024
N_ROW_TILES = N_TOK // TM
CTX_TILES = CTX_TOK // TM
TF = 256
P_EVEN_PAD = 27 * LANES
DN_CHUNK = 128
COL_Q, COL_K, COL_V, COL_Z, COL_G = 2, 8, 14, 20, 26


def _cond_of_tile(i):
    return jnp.where(i < CTX_TILES, 0, 1 + (i - CTX_TILES) * TM // DEC_SEQ)


def _silu(x):
    return x * jax.nn.sigmoid(x)


def _dot(a, b):
    return jnp.dot(a.astype(BF16), b.astype(BF16), preferred_element_type=F32)


def _dot_nt(a, b):
    return lax.dot_general(a.astype(BF16), b.astype(BF16), (((1,), (1,)), ((), ())),
                           preferred_element_type=F32)


def _dot_tn(a, b):
    return lax.dot_general(a.astype(BF16), b.astype(BF16), (((0,), (0,)), ((), ())),
                           preferred_element_type=F32)


def _split(a):
    hi = a.astype(BF16)
    lo = (a - hi.astype(F32)).astype(BF16)
    return hi, lo


def _dot3(a, b):
    ah, al = a
    bh, bl = b
    return (jnp.dot(ah, bh, preferred_element_type=F32)
            + (jnp.dot(al, bh, preferred_element_type=F32)
               + jnp.dot(ah, bl, preferred_element_type=F32)))


def _rms(x, w):
    return x * lax.rsqrt(jnp.mean(x * x, axis=-1, keepdims=True) + EPS) * w


def _params(*sem):
    return pltpu.CompilerParams(dimension_semantics=sem, vmem_limit_bytes=VMEM_LIMIT)


def _ada_kernel(c_ref, w_ref, b_ref, o_ref):
    o_ref[0] = _dot(_silu(c_ref[...]), w_ref[0]) + b_ref[0]


def _ada(cond8, ada_w, ada_b):
    tn = 1152
    n = N_MOD * D_MODEL
    return pl.pallas_call(
        _ada_kernel,
        grid=(DEPTH, n // tn),
        in_specs=[pl.BlockSpec((SUBLANES, D_MODEL), lambda l, j: (0, 0)),
                  pl.BlockSpec((1, D_MODEL, tn), lambda l, j: (l, 0, j)),
                  pl.BlockSpec((1, 1, tn), lambda l, j: (l, 0, j))],
        out_specs=pl.BlockSpec((1, SUBLANES, tn), lambda l, j: (l, 0, j)),
        out_shape=jax.ShapeDtypeStruct((DEPTH, SUBLANES, n), F32),
        compiler_params=_params("arbitrary", "arbitrary"),
        name="ada",
    )(cond8, ada_w, ada_b.reshape(DEPTH, 1, n))


def _mod_spec(layer, which):
    return pl.BlockSpec((None, None, None, 1, D_MODEL),
                        lambda i, *_: (layer, which, _cond_of_tile(i), 0, 0))


def _assemble_kernel(xp_ref, xs_ref, pos_ref, o_ref):
    i = pl.program_id(0)

    @pl.when(i < CTX_TILES)
    def _():
        o_ref[...] = xp_ref[...]

    @pl.when(i >= CTX_TILES)
    def _():
        o_ref[...] = xs_ref[...] + pos_ref[...]


def _assemble(xp, xs, pos):
    return pl.pallas_call(
        _assemble_kernel,
        grid=(N_ROW_TILES,),
        in_specs=[pl.BlockSpec((TM, D_MODEL), lambda i: (jnp.minimum(i, CTX_TILES - 1), 0)),
                  pl.BlockSpec((TM, D_MODEL), lambda i: (jnp.maximum(i - CTX_TILES, 0), 0)),
                  pl.BlockSpec((TM, D_MODEL), lambda i: (0, 0))],
        out_specs=pl.BlockSpec((TM, D_MODEL), lambda i: (i, 0)),
        out_shape=jax.ShapeDtypeStruct((N_TOK, D_MODEL), F32),
        compiler_params=_params("arbitrary"),
        name="assemble",
    )(xp, xs, pos)


def _ffn_kernel(x_ref, nw_ref, sh_ref, sc_ref, gt_ref, wg_ref, wu_ref, wd_ref, fn_ref, o_ref,
                h_scr, acc_scr, *, final):
    j = pl.program_id(1)

    @pl.when(j == 0)
    def _():
        h = _rms(x_ref[...], nw_ref[...]) * (1.0 + sc_ref[...]) + sh_ref[...]
        h_scr[...] = h.astype(BF16)
        acc_scr[...] = jnp.zeros_like(acc_scr)

    h = h_scr[...]
    g = jnp.dot(h, wg_ref[0].astype(BF16), preferred_element_type=F32)
    u = jnp.dot(h, wu_ref[0].astype(BF16), preferred_element_type=F32)
    acc_scr[...] += _dot(_silu(g) * u, wd_ref[0])

    @pl.when(j == pl.num_programs(1) - 1)
    def _():
        y = x_ref[...] + 0.5 * gt_ref[...] * acc_scr[...]
        if final:
            y = _rms(y, fn_ref[...])
        o_ref[...] = y


def _ffn(x, mods, layer, sub, norm_w, w_gate, w_up, w_down, final_norm, final):
    row = lambda i, j: (i, 0)
    vec = pl.BlockSpec((1, D_MODEL), lambda i, j: (0, 0))
    return pl.pallas_call(
        functools.partial(_ffn_kernel, final=final),
        grid=(N_ROW_TILES, D_FF // TF),
        in_specs=[pl.BlockSpec((TM, D_MODEL), row), vec,
                  _mod_spec(layer, 3 * sub), _mod_spec(layer, 3 * sub + 1), _mod_spec(layer, 3 * sub + 2),
                  pl.BlockSpec((1, D_MODEL, TF), lambda i, j: (layer, 0, j)),
                  pl.BlockSpec((1, D_MODEL, TF), lambda i, j: (layer, 0, j)),
                  pl.BlockSpec((1, TF, D_MODEL), lambda i, j: (layer, j, 0)),
                  vec],
        out_specs=pl.BlockSpec((TM, D_MODEL), row),
        out_shape=jax.ShapeDtypeStruct((N_TOK, D_MODEL), F32),
        scratch_shapes=[pltpu.VMEM((TM, D_MODEL), BF16), pltpu.VMEM((TM, D_MODEL), F32)],
        compiler_params=_params("arbitrary", "arbitrary"),
        name=f"ffn{sub}_{layer}",
    )(x, norm_w[layer][None], mods, mods, mods, w_gate, w_up, w_down, final_norm[None])


def _mixin_kernel(x_ref, nw_ref, sh_ref, sc_ref, w_ref, o_ref, h_scr):
    @pl.when(pl.program_id(1) == 0)
    def _():
        h = _rms(x_ref[...], nw_ref[...]) * (1.0 + sc_ref[...]) + sh_ref[...]
        h_scr[...] = h.astype(BF16)

    o_ref[...] = jnp.dot(h_scr[...], w_ref[...].astype(BF16), preferred_element_type=F32)


def _mixin(x, mods, layer, norm_w, w_in, tn):
    width = w_in.shape[1]
    return pl.pallas_call(
        _mixin_kernel,
        grid=(N_ROW_TILES, width // tn),
        in_specs=[pl.BlockSpec((TM, D_MODEL), lambda i, j: (i, 0)),
                  pl.BlockSpec((1, D_MODEL), lambda i, j: (0, 0)),
                  _mod_spec(layer, 3), _mod_spec(layer, 4),
                  pl.BlockSpec((D_MODEL, tn), lambda i, j: (0, j))],
        out_specs=pl.BlockSpec((TM, tn), lambda i, j: (i, j)),
        out_shape=jax.ShapeDtypeStruct((N_TOK, width), F32),
        scratch_shapes=[pltpu.VMEM((TM, D_MODEL), BF16)],
        compiler_params=_params("arbitrary", "arbitrary"),
        name=f"mixin_{layer}",
    )(x, norm_w[layer][None], mods, mods, w_in)


def _mixout_kernel(x_ref, ya_ref, yb_ref, gt_ref, w_ref, o_ref, *, split):
    mix = _dot(ya_ref[...], w_ref[0:split, :]) + _dot(yb_ref[...], w_ref[split:, :])
    o_ref[...] = x_ref[...] + gt_ref[...] * mix


def _mixout(x, ya, yb, mods, layer, w_out):
    split = ya.shape[1]
    row = lambda i: (i, 0)
    return pl.pallas_call(
        functools.partial(_mixout_kernel, split=split),
        grid=(N_ROW_TILES,),
        in_specs=[pl.BlockSpec((TM, D_MODEL), row),
                  pl.BlockSpec((TM, split), row),
                  pl.BlockSpec((TM, D_MODEL - split), row),
                  _mod_spec(layer, 5),
                  pl.BlockSpec((D_MODEL, D_MODEL), lambda i: (0, 0))],
        out_specs=pl.BlockSpec((TM, D_MODEL), row),
        out_shape=jax.ShapeDtypeStruct((N_TOK, D_MODEL), F32),
        compiler_params=_params("arbitrary"),
        name=f"mixout_{layer}",
    )(x, ya, yb, mods, w_out)


POOL_PAD = 8


def _pool_kernel(p_ref, w_ref, scale_ref, o_ref, *, seq):
    p = p_ref[...]
    zeros = jnp.zeros((POOL_PAD, A_W), F32)
    xp = jnp.concatenate([zeros, p, zeros], axis=0)
    n_pad = seq + 2 * POOL_PAD

    def back(a, s):
        return pltpu.roll(a, s, axis=0)

    def fwd(a, s):
        return pltpu.roll(a, n_pad - s, axis=0)

    w2 = xp + back(xp, 1)
    w4 = back(w2, 1) + fwd(w2, 1)
    w8 = back(w4, 2) + fwd(w4, 2)
    w16 = back(w8, 4) + fwd(w8, 4)
    sl = slice(POOL_PAD, POOL_PAD + seq)
    group = lax.broadcasted_iota(jnp.int32, (seq, A_W), 1) >> (POOL_CH.bit_length() - 1)
    t = lax.broadcasted_iota(jnp.int32, (seq, A_W), 0)
    wsum = jnp.where(group == 0, w2[sl], jnp.where(group == 1, w4[sl], jnp.where(group == 2, w8[sl], w16[sl])))
    half = jnp.left_shift(1, group)
    cnt = (jnp.minimum(t + half, seq) - jnp.maximum(t - half, 0)).astype(F32)
    d = wsum / cnt - p
    o_ref[...] = _dot(d, w_ref[...]) * scale_ref[...]


def _pool(p_all, w_bd, scale, seq, n_seq, row_block0):
    return pl.pallas_call(
        functools.partial(_pool_kernel, seq=seq),
        grid=(n_seq,),
        in_specs=[pl.BlockSpec((seq, A_W), lambda b: (row_block0 + b, 0)),
                  pl.BlockSpec((A_W, A_W), lambda b: (0, 0)),
                  pl.BlockSpec((1, A_W), lambda b: (0, 0))],
        out_specs=pl.BlockSpec((seq, A_W), lambda b: (b, 0)),
        out_shape=jax.ShapeDtypeStruct((n_seq * seq, A_W), F32),
        compiler_params=_params("arbitrary"),
        name=f"pool_{seq}",
    )(p_all, w_bd, scale)


def _unit_tri_inverse(m, upper):
    n = DN_CHUNK
    r = lax.broadcasted_iota(jnp.int32, (n, n), 0)
    c = lax.broadcasted_iota(jnp.int32, (n, n), 1)
    eye = (r == c).astype(F32)
    t = eye
    s = 1
    while s < n:
        lg = s.bit_length() - 1
        same = (r >> (lg + 1)) == (c >> (lg + 1))
        r_half = (r >> lg) & 1
        c_half = (c >> lg) & 1
        link = same & ((r_half == 0) & (c_half == 1) if upper else (r_half == 1) & (c_half == 0))
        cm = jnp.where(link, m, 0.0)
        if s == 1:
            t = eye - cm
        else:
            ts = _split(t)
            x = _dot3(_split(cm), ts)
            t = t - _dot3(ts, _split(x))
        s *= 2
    return t


def _dn_kernel(q_ref, k_ref, v_ref, z_ref, g_ref, cq_ref, ck_ref, cv_ref, alog_ref, dtb_ref, nw_ref,
               *rest, seq, zero_init, emit_state):
    rest = list(rest)
    s0_ref = None if zero_init else rest.pop(0)
    o_ref = rest.pop(0)
    st_ref = rest.pop(0) if emit_state else None
    h = pl.program_id(1)
    n = DN_CHUNK
    n_blk = seq // n
    row = lax.broadcasted_iota(jnp.int32, (seq, LANES), 0)
    lane = lax.broadcasted_iota(jnp.int32, (seq, LANES), 1)

    def conv_silu(x_ref, cw_ref):
        x = x_ref[...]
        cw = cw_ref[...]
        prev = jnp.where(row >= 1, pltpu.roll(x, 1, axis=0), 0.0)
        nxt = jnp.where(row <= seq - 2, pltpu.roll(x, seq - 1, axis=0), 0.0)
        return _silu(prev * cw[0:1] + x * cw[1:2] + nxt * cw[2:3])

    def l2n(x):
        return x * lax.rsqrt(jnp.sum(x * x, axis=-1, keepdims=True) + EPS)

    qs = l2n(conv_silu(q_ref, cq_ref)) * (DN_DK ** -0.5)
    kn = l2n(conv_silu(k_ref, ck_ref))
    v = conv_silu(v_ref, cv_ref)

    gates = g_ref[...]
    beta_all = jax.nn.sigmoid(gates)
    xg = gates + dtb_ref[...]
    softplus = jnp.maximum(xg, 0.0) + jnp.log1p(jnp.exp(-jnp.abs(xg)))
    g_all = -jnp.exp(alog_ref[...]) * softplus

    def col(a, idx):
        return jnp.sum(jnp.where(lane == idx, a, 0.0), axis=1, keepdims=True)

    beta = (col(beta_all, h), col(beta_all, DN_HEADS + h))
    g_dir = (col(g_all, 2 * DN_HEADS + h), col(g_all, 3 * DN_HEADS + h))

    r2 = lax.broadcasted_iota(jnp.int32, (n, n), 0)
    c2 = lax.broadcasted_iota(jnp.int32, (n, n), 1)
    incl = (r2 >= c2, r2 <= c2)
    strict = (r2 > c2, r2 < c2)
    tri = tuple(_split(m.astype(F32)) for m in incl)

    if zero_init:
        state = [jnp.zeros((DN_DK, DN_DK), F32), jnp.zeros((DN_DK, DN_DK), F32)]
    else:
        state = [s0_ref[0], s0_ref[1]]
    outs = [[None] * n_blk, [None] * n_blk]

    for step in range(n_blk):
        for d in (0, 1):
            blk = step if d == 0 else n_blk - 1 - step
            rs = slice(blk * n, (blk + 1) * n)
            kb, qb, vb = kn[rs], qs[rs], v[rs]
            bcol = beta[d][rs]
            g_wide = jnp.broadcast_to(g_dir[d][rs], (n, n))
            gc_wide = _dot3(tri[d], _split(g_wide))
            gcol = gc_wide[:, 0:1]
            decay = jnp.where(incl[d], jnp.exp(gc_wide - gc_wide.T), 0.0)
            kk = _dot_nt(kb, kb)
            m = jnp.where(strict[d], bcol * kk * decay, 0.0)
            tinv = _unit_tri_inverse(m, upper=(d == 1))
            e_g = jnp.exp(gcol)
            u = _dot(tinv, vb * bcol)
            w = _dot(tinv, kb * (bcol * e_g))
            a = jnp.where(incl[d], _dot_nt(qb, kb) * decay, 0.0)
            g_last = gcol[n - 1:n] if d == 0 else gcol[0:1]
            q_dec = qb * e_g
            k_dec = kb * jnp.exp(g_last - gcol)
            s = state[d]
            ws_qs = _dot(jnp.concatenate([w, q_dec], axis=0), s)
            v_new = u - ws_qs[0:n]
            outs[d][blk] = ws_qs[n:] + _dot(a, v_new)
            state[d] = s * jnp.exp(g_last) + _dot_tn(k_dec, v_new)

    o = jnp.concatenate(outs[0], axis=0) + jnp.concatenate(outs[1], axis=0)
    o = o * lax.rsqrt(jnp.mean(o * o, axis=-1, keepdims=True) + EPS) * nw_ref[...]
    o_ref[...] = o * _silu(z_ref[...])
    if emit_state:
        st_ref[0] = state[0]
        st_ref[1] = state[1]


def _deltanet(p_all, conv_w, alog_row, dtb_row, norm_w, s0, seq, n_seq, row_block0, emit_state):
    zero_init = s0 is None

    def pcol(cb):
        return pl.BlockSpec((seq, LANES), lambda b, h: (row_block0 + b, cb + h))

    def ccol(cb):
        return pl.BlockSpec((3, LANES), lambda b, h: (0, cb + h))

    vec = pl.BlockSpec((1, LANES), lambda b, h: (0, 0))
    state_spec = pl.BlockSpec((None, 2, None, DN_DK, DN_DK), lambda b, h: (b, 0, h, 0, 0))
    in_specs = [pcol(COL_Q), pcol(COL_K), pcol(COL_V), pcol(COL_Z),
                pl.BlockSpec((seq, LANES), lambda b, h: (row_block0 + b, COL_G)),
                ccol(0), ccol(DN_HEADS), ccol(2 * DN_HEADS), vec, vec, vec]
    args = [p_all, p_all, p_all, p_all, p_all, conv_w, conv_w, conv_w, alog_row, dtb_row, norm_w]
    if not zero_init:
        in_specs.append(state_spec)
        args.append(s0)
    out_specs = [pl.BlockSpec((seq, LANES), lambda b, h: (b, h))]
    out_shape = [jax.ShapeDtypeStruct((n_seq * seq, B_W), F32)]
    if emit_state:
        out_specs.append(state_spec)
        out_shape.append(jax.ShapeDtypeStruct((n_seq, 2, DN_HEADS, DN_DK, DN_DK), F32))
    res = pl.pallas_call(
        functools.partial(_dn_kernel, seq=seq, zero_init=zero_init, emit_state=emit_state),
        grid=(n_seq, DN_HEADS),
        in_specs=in_specs,
        out_specs=out_specs,
        out_shape=out_shape,
        compiler_params=_params("arbitrary", "arbitrary"),
        name=f"deltanet_{seq}",
    )(*args)
    return res if emit_state else (res[0], None)


TS = 512


def _sgu_kernel(u_ref, v_ref, nw_ref, ws_ref, bt_ref, o_ref):
    u = jax.nn.gelu(u_ref[...])
    v = jax.nn.gelu(v_ref[...])
    mu = jnp.mean(v, axis=-1, keepdims=True)
    vc = v - mu
    var = jnp.mean(vc * vc, axis=-1, keepdims=True)
    vn = vc * lax.rsqrt(var + EPS) * nw_ref[...]
    bt = bt_ref[...]
    for c in range(TS // SGU_CHUNK):
        rs = slice(c * SGU_CHUNK, (c + 1) * SGU_CHUNK)
        for hd in range(SGU_HEADS):
            cs = slice(hd * LANES, (hd + 1) * LANES)
            mixed = _dot(ws_ref[hd], vn[rs, cs]) + bt[:, hd:hd + 1]
            o_ref[rs, cs] = u[rs, cs] * mixed


def _sgu(p_all, norm_w, w_s, b_t):
    return pl.pallas_call(
        _sgu_kernel,
        grid=(N_TOK // TS,),
        in_specs=[pl.BlockSpec((TS, C_W), lambda i: (i, 0)),
                  pl.BlockSpec((TS, C_W), lambda i: (i, 1)),
                  pl.BlockSpec((1, C_W), lambda i: (0, 0)),
                  pl.BlockSpec((SGU_HEADS, SGU_CHUNK, SGU_CHUNK), lambda i: (0, 0, 0)),
                  pl.BlockSpec((SGU_CHUNK, SGU_HEADS), lambda i: (0, 0))],
        out_specs=pl.BlockSpec((TS, C_W), lambda i: (i, 0)),
        out_shape=jax.ShapeDtypeStruct((N_TOK, C_W), F32),
        compiler_params=_params("arbitrary"),
        name="sgu",
    )(p_all, p_all, norm_w, w_s, b_t)


def _fnet_kernel(f_ref, cn_ref, sn_ref, cc_ref, sc_ref, w_ref, o_ref):
    f = _split(f_ref[...])
    fc = _dot3(f, _split(cc_ref[...]))
    fs = _dot3(f, _split(sc_ref[...]))
    spec = _dot3(_split(cn_ref[...]), _split(fc)) - _dot3(_split(sn_ref[...]), _split(fs))
    o_ref[...] = _dot(spec, w_ref[...])


def _fnet(p_all, cn, sn, cc_bd, sc_bd, w_bd, seq, n_seq, row_block0):
    const = lambda shape: pl.BlockSpec(shape, lambda b: (0, 0))
    return pl.pallas_call(
        _fnet_kernel,
        grid=(n_seq,),
        in_specs=[pl.BlockSpec((seq, F_W), lambda b: (row_block0 + b, 2 * C_W // F_W)),
                  const((seq, seq)), const((seq, seq)), const((F_W, F_W)), const((F_W, F_W)),
                  const((F_W, F_W))],
        out_specs=pl.BlockSpec((seq, F_W), lambda b: (b, 0)),
        out_shape=jax.ShapeDtypeStruct((n_seq * seq, F_W), F32),
        compiler_params=_params("arbitrary"),
        name=f"fnet_{seq}",
    )(p_all, cn, sn, cc_bd, sc_bd, w_bd)


def _grid_pos_embed(n_tok):
    rows = n_tok // GRID_W
    r = jnp.broadcast_to(jnp.arange(rows, dtype=F32)[:, None], (rows, GRID_W)).reshape(-1)
    col = jnp.broadcast_to(jnp.arange(GRID_W, dtype=F32)[None, :], (rows, GRID_W)).reshape(-1)
    quarter = D_MODEL // 4
    freq = jnp.exp(-math.log(10000.0) * jnp.arange(quarter, dtype=F32) / quarter)

    def emb(p):
        ang = p[:, None] * freq[None, :]
        return jnp.concatenate([jnp.sin(ang), jnp.cos(ang)], axis=-1)

    return jnp.concatenate([emb(r), emb(col)], axis=-1)


def _dft_tables(n):
    k = jnp.arange(n, dtype=jnp.int32)
    ang = ((k[:, None] * k[None, :]) % n).astype(F32) * (2.0 * math.pi / n)
    scale = n ** -0.5
    return jnp.cos(ang) * scale, jnp.sin(ang) * scale


def _block_diag(blocks):
    g, a, b = blocks.shape
    eye = jnp.eye(g, dtype=blocks.dtype)
    return (eye[:, None, :, None] * blocks[:, :, None, :]).reshape(g * a, g * b)


def _lane_row(values, offset):
    return jnp.zeros((1, LANES), F32).at[0, offset:offset + values.size].set(values.reshape(-1))


def kernel(x_prompt, x_sample, state_delta, c, c_ctx, ffn1_norm, ffn1_w_gate, ffn1_w_up, ffn1_w_down,
           mix_norm, ffn2_norm, ffn2_w_gate, ffn2_w_up, ffn2_w_down, ada_w, ada_b, ev_w_in, ev_w_out,
           pool_w, pool_scale, dn_conv_w, dn_a_log, dn_dt_bias, dn_norm_w, od_w_in, od_w_out, sgu_norm,
           sgu_w, sgu_b, fnet_w, final_norm):
    cond8 = jnp.zeros((SUBLANES, D_MODEL), F32).at[0].set(c_ctx).at[1:1 + DEC_BATCH].set(c)
    mods = _ada(cond8, ada_w, ada_b)
    mods = mods[:, :1 + DEC_BATCH].reshape(DEPTH, 1 + DEC_BATCH, N_MOD, 1, D_MODEL).transpose(0, 2, 1, 3, 4)

    x = _assemble(x_prompt.reshape(CTX_TOK, D_MODEL), x_sample.reshape(LAT_TOK, D_MODEL),
                  _grid_pos_embed(DEC_SEQ))

    dft = {n: _dft_tables(n) for n in (SEQ, DEC_SEQ)}
    cc, sc = _dft_tables(FN_CH)
    cc_bd = _block_diag(jnp.broadcast_to(cc, (F_W // FN_CH, FN_CH, FN_CH)))
    sc_bd = _block_diag(jnp.broadcast_to(sc, (F_W // FN_CH, FN_CH, FN_CH)))
    groups = ((SEQ, BATCH, 0), (DEC_SEQ, DEC_BATCH, CTX_TOK // DEC_SEQ))

    states = []
    for layer in range(DEPTH):
        x = _ffn(x, mods, layer, 0, ffn1_norm, ffn1_w_gate, ffn1_w_up, ffn1_w_down, final_norm, False)
        if layer % 2 == 0:
            e = layer // 2
            w_in = jnp.pad(ev_w_in[e], ((0, 0), (0, P_EVEN_PAD - P_EVEN)))
            p_all = _mixin(x, mods, layer, mix_norm, w_in, P_EVEN_PAD // 3)
            w_bd = _block_diag(pool_w[e])
            alog_row = _lane_row(dn_a_log[e], 2 * DN_HEADS)
            dtb_row = _lane_row(dn_dt_bias[e], 2 * DN_HEADS)
            ya, yb = [], []
            for seq, n_seq, rb0 in groups:
                ctx = rb0 == 0
                ya.append(_pool(p_all, w_bd, pool_scale[e][None], seq, n_seq, rb0))
                o, st = _deltanet(p_all, dn_conv_w[e], alog_row, dtb_row, dn_norm_w[e][None],
                                  None if ctx else state_delta[:, e], seq, n_seq, rb0, ctx)
                yb.append(o)
                if ctx:
                    states.append(st)
            x = _mixout(x, jnp.concatenate(ya, axis=0), jnp.concatenate(yb, axis=0), mods, layer, ev_w_out[e])
        else:
            j = layer // 2
            p_all = _mixin(x, mods, layer, mix_norm, od_w_in[j], P_ODD // 2)
            yc = _sgu(p_all, sgu_norm[j][None], sgu_w[j], sgu_b[j].T)
            w_bd = _block_diag(fnet_w[j])
            yd = [_fnet(p_all, dft[seq][0], dft[seq][1], cc_bd, sc_bd, w_bd, seq, n_seq, rb0)
                  for seq, n_seq, rb0 in groups]
            x = _mixout(x, yc, jnp.concatenate(yd, axis=0), mods, layer, od_w_out[j])
        x = _ffn(x, mods, layer, 2, ffn2_norm, ffn2_w_gate, ffn2_w_up, ffn2_w_down, final_norm,
                 layer == DEPTH - 1)

    y_prompt = x[:CTX_TOK].reshape(BATCH, SEQ, D_MODEL)
    y_sample = x[CTX_TOK:].reshape(DEC_BATCH, DEC_SEQ, D_MODEL)
    return y_prompt, y_sample, jnp.stack(states, axis=1)
```

```python
import functools
import math

import jax
import jax.numpy as jnp
from jax import lax
from jax.experimental import pallas as pl
from jax.experimental.pallas import tpu as pltpu

F32 = jnp.float32
BF16 = jnp.bfloat16

D_MODEL = 1024
BATCH = 16
SEQ = 256
DEPTH = 4
DEC_BATCH = 2
DEC_SEQ = 1024
GRID_W = 64
EPS = 1e-6
D_FF = 2816
N_MOD = 9
N_EVEN = (DEPTH + 1) // 2
POOL_SIZES = (2, 4, 8, 16)
POOL_CH = 64
A_W = 256
DN_HEADS = 6
DN_DK = 128
QK_W = DN_HEADS * DN_DK
B_W = QK_W
CONV_CH = 3 * QK_W
P_EVEN = A_W + CONV_CH + B_W + 4 * DN_HEADS
SGU_CHUNK = 128
SGU_HEADS = 6
C_W = 768
FN_CH = 64
F_W = 256
P_ODD = 2 * C_W + F_W

CTX_TOK = BATCH * SEQ
LAT_TOK = DEC_BATCH * DEC_SEQ
N_TOK = CTX_TOK + LAT_TOK

LANES = 128
SUBLANES = 8
VMEM_LIMIT = 56 * 1024 * 1024

TM = 1024
N_ROW_TILES = N_TOK // TM
CTX_TILES = CTX_TOK // TM
TF = 256
P_EVEN_PAD = 27 * LANES
DN_CHUNK = 128
COL_Q, COL_K, COL_V, COL_Z = 0, 1, 2, 3
COL_POOL = 4 * QK_W // A_W
COL_G = (4 * QK_W + A_W) // LANES


def _cond_of_tile(i):
    return jnp.where(i < CTX_TILES, 0, 1 + (i - CTX_TILES) * TM // DEC_SEQ)


def _silu(x):
    return x * jax.nn.sigmoid(x)


def _dot(a, b):
    return jnp.dot(a.astype(BF16), b.astype(BF16), preferred_element_type=F32)


def _dot_nt(a, b):
    return lax.dot_general(a.astype(BF16), b.astype(BF16), (((1,), (1,)), ((), ())),
                           preferred_element_type=F32)


def _dot_tn(a, b):
    return lax.dot_general(a.astype(BF16), b.astype(BF16), (((0,), (0,)), ((), ())),
                           preferred_element_type=F32)


def _split(a):
    hi = a.astype(BF16)
    lo = (a - hi.astype(F32)).astype(BF16)
    return hi, lo


def _dot3(a, b):
    ah, al = a
    bh, bl = b
    return (jnp.dot(ah, bh, preferred_element_type=F32)
            + (jnp.dot(al, bh, preferred_element_type=F32)
               + jnp.dot(ah, bl, preferred_element_type=F32)))


def _rms(x, w):
    return x * lax.rsqrt(jnp.mean(x * x, axis=-1, keepdims=True) + EPS) * w


def _params(*sem):
    return pltpu.CompilerParams(dimension_semantics=sem, vmem_limit_bytes=VMEM_LIMIT)


def _ada_kernel(c_ref, w_ref, b_ref, o_ref):
    o_ref[0] = _dot(_silu(c_ref[...]), w_ref[0]) + b_ref[0]


def _ada(cond8, ada_w, ada_b):
    tn = 1152
    n = N_MOD * D_MODEL
    return pl.pallas_call(
        _ada_kernel,
        grid=(DEPTH, n // tn),
        in_specs=[pl.BlockSpec((SUBLANES, D_MODEL), lambda l, j: (0, 0)),
                  pl.BlockSpec((1, D_MODEL, tn), lambda l, j: (l, 0, j)),
                  pl.BlockSpec((1, 1, tn), lambda l, j: (l, 0, j))],
        out_specs=pl.BlockSpec((1, SUBLANES, tn), lambda l, j: (l, 0, j)),
        out_shape=jax.ShapeDtypeStruct((DEPTH, SUBLANES, n), F32),
        compiler_params=_params("arbitrary", "arbitrary"),
        name="ada",
    )(cond8, ada_w, ada_b.reshape(DEPTH, 1, n))


def _mod_spec(layer, which):
    return pl.BlockSpec((None, None, None, 1, D_MODEL),
                        lambda i, *_: (layer, which, _cond_of_tile(i), 0, 0))


def _assemble_kernel(xp_ref, xs_ref, pos_ref, o_ref):
    i = pl.program_id(0)

    @pl.when(i < CTX_TILES)
    def _():
        o_ref[...] = xp_ref[...]

    @pl.when(i >= CTX_TILES)
    def _():
        o_ref[...] = xs_ref[...] + pos_ref[...]


def _assemble(xp, xs, pos):
    return pl.pallas_call(
        _assemble_kernel,
        grid=(N_ROW_TILES,),
        in_specs=[pl.BlockSpec((TM, D_MODEL), lambda i: (jnp.minimum(i, CTX_TILES - 1), 0)),
                  pl.BlockSpec((TM, D_MODEL), lambda i: (jnp.maximum(i - CTX_TILES, 0), 0)),
                  pl.BlockSpec((TM, D_MODEL), lambda i: (0, 0))],
        out_specs=pl.BlockSpec((TM, D_MODEL), lambda i: (i, 0)),
        out_shape=jax.ShapeDtypeStruct((N_TOK, D_MODEL), F32),
        compiler_params=_params("arbitrary"),
        name="assemble",
    )(xp, xs, pos)


def _ffn_kernel(x_ref, nw_ref, sh_ref, sc_ref, gt_ref, wg_ref, wu_ref, wd_ref, fn_ref, o_ref,
                h_scr, acc_scr, *, final):
    j = pl.program_id(1)

    @pl.when(j == 0)
    def _():
        h = _rms(x_ref[...], nw_ref[...]) * (1.0 + sc_ref[...]) + sh_ref[...]
        h_scr[...] = h.astype(BF16)
        acc_scr[...] = jnp.zeros_like(acc_scr)

    h = h_scr[...]
    g = jnp.dot(h, wg_ref[0].astype(BF16), preferred_element_type=F32)
    u = jnp.dot(h, wu_ref[0].astype(BF16), preferred_element_type=F32)
    acc_scr[...] += _dot(_silu(g) * u, wd_ref[0])

    @pl.when(j == pl.num_programs(1) - 1)
    def _():
        y = x_ref[...] + 0.5 * gt_ref[...] * acc_scr[...]
        if final:
            y = _rms(y, fn_ref[...])
        o_ref[...] = y


def _ffn(x, mods, layer, sub, norm_w, w_gate, w_up, w_down, final_norm, final):
    row = lambda i, j: (i, 0)
    vec = pl.BlockSpec((1, D_MODEL), lambda i, j: (0, 0))
    return pl.pallas_call(
        functools.partial(_ffn_kernel, final=final),
        grid=(N_ROW_TILES, D_FF // TF),
        in_specs=[pl.BlockSpec((TM, D_MODEL), row), vec,
                  _mod_spec(layer, 3 * sub), _mod_spec(layer, 3 * sub + 1), _mod_spec(layer, 3 * sub + 2),
                  pl.BlockSpec((1, D_MODEL, TF), lambda i, j: (layer, 0, j)),
                  pl.BlockSpec((1, D_MODEL, TF), lambda i, j: (layer, 0, j)),
                  pl.BlockSpec((1, TF, D_MODEL), lambda i, j: (layer, j, 0)),
                  vec],
        out_specs=pl.BlockSpec((TM, D_MODEL), row),
        out_shape=jax.ShapeDtypeStruct((N_TOK, D_MODEL), F32),
        scratch_shapes=[pltpu.VMEM((TM, D_MODEL), BF16), pltpu.VMEM((TM, D_MODEL), F32)],
        compiler_params=_params("arbitrary", "arbitrary"),
        name=f"ffn{sub}_{layer}",
    )(x, norm_w[layer][None], mods, mods, mods, w_gate, w_up, w_down, final_norm[None])


def _mixin_kernel(x_ref, nw_ref, sh_ref, sc_ref, w_ref, o_ref, h_scr):
    @pl.when(pl.program_id(1) == 0)
    def _():
        h = _rms(x_ref[...], nw_ref[...]) * (1.0 + sc_ref[...]) + sh_ref[...]
        h_scr[...] = h.astype(BF16)

    o_ref[...] = jnp.dot(h_scr[...], w_ref[...].astype(BF16), preferred_element_type=F32)


def _mixin(x, mods, layer, norm_w, w_in, tn):
    width = w_in.shape[1]
    return pl.pallas_call(
        _mixin_kernel,
        grid=(N_ROW_TILES, width // tn),
        in_specs=[pl.BlockSpec((TM, D_MODEL), lambda i, j: (i, 0)),
                  pl.BlockSpec((1, D_MODEL), lambda i, j: (0, 0)),
                  _mod_spec(layer, 3), _mod_spec(layer, 4),
                  pl.BlockSpec((D_MODEL, tn), lambda i, j: (0, j))],
        out_specs=pl.BlockSpec((TM, tn), lambda i, j: (i, j)),
        out_shape=jax.ShapeDtypeStruct((N_TOK, width), F32),
        scratch_shapes=[pltpu.VMEM((TM, D_MODEL), BF16)],
        compiler_params=_params("arbitrary", "arbitrary"),
        name=f"mixin_{layer}",
    )(x, norm_w[layer][None], mods, mods, w_in)


def _mixout_kernel(x_ref, ya_ref, yb_ref, gt_ref, w_ref, o_ref, *, split):
    mix = _dot(ya_ref[...], w_ref[0:split, :]) + _dot(yb_ref[...], w_ref[split:, :])
    o_ref[...] = x_ref[...] + gt_ref[...] * mix


def _mixout(x, ya, yb, mods, layer, w_out):
    split = ya.shape[1]
    row = lambda i: (i, 0)
    return pl.pallas_call(
        functools.partial(_mixout_kernel, split=split),
        grid=(N_ROW_TILES,),
        in_specs=[pl.BlockSpec((TM, D_MODEL), row),
                  pl.BlockSpec((TM, split), row),
                  pl.BlockSpec((TM, D_MODEL - split), row),
                  _mod_spec(layer, 5),
                  pl.BlockSpec((D_MODEL, D_MODEL), lambda i: (0, 0))],
        out_specs=pl.BlockSpec((TM, D_MODEL), row),
        out_shape=jax.ShapeDtypeStruct((N_TOK, D_MODEL), F32),
        compiler_params=_params("arbitrary"),
        name=f"mixout_{layer}",
    )(x, ya, yb, mods, w_out)


POOL_PAD = 8


def _pool_kernel(p_ref, w_ref, scale_ref, o_ref, *, seq):
    p = p_ref[...]
    zeros = jnp.zeros((POOL_PAD, A_W), F32)
    xp = jnp.concatenate([zeros, p, zeros], axis=0)
    n_pad = seq + 2 * POOL_PAD

    def back(a, s):
        return pltpu.roll(a, s, axis=0)

    def fwd(a, s):
        return pltpu.roll(a, n_pad - s, axis=0)

    w2 = xp + back(xp, 1)
    w4 = back(w2, 1) + fwd(w2, 1)
    w8 = back(w4, 2) + fwd(w4, 2)
    w16 = back(w8, 4) + fwd(w8, 4)
    sl = slice(POOL_PAD, POOL_PAD + seq)
    group = lax.broadcasted_iota(jnp.int32, (seq, A_W), 1) >> (POOL_CH.bit_length() - 1)
    t = lax.broadcasted_iota(jnp.int32, (seq, A_W), 0)
    wsum = jnp.where(group == 0, w2[sl], jnp.where(group == 1, w4[sl], jnp.where(group == 2, w8[sl], w16[sl])))
    half = jnp.left_shift(1, group)
    cnt = (jnp.minimum(t + half, seq) - jnp.maximum(t - half, 0)).astype(F32)
    d = wsum / cnt - p
    o_ref[...] = _dot(d, w_ref[...]) * scale_ref[...]


def _pool(p_all, w_bd, scale, seq, n_seq, row_block0):
    return pl.pallas_call(
        functools.partial(_pool_kernel, seq=seq),
        grid=(n_seq,),
        in_specs=[pl.BlockSpec((seq, A_W), lambda b: (row_block0 + b, COL_POOL)),
                  pl.BlockSpec((A_W, A_W), lambda b: (0, 0)),
                  pl.BlockSpec((1, A_W), lambda b: (0, 0))],
        out_specs=pl.BlockSpec((seq, A_W), lambda b: (b, 0)),
        out_shape=jax.ShapeDtypeStruct((n_seq * seq, A_W), F32),
        compiler_params=_params("arbitrary"),
        name=f"pool_{seq}",
    )(p_all, w_bd, scale)


def _link_mask(r, c, s, upper):
    lg = s.bit_length() - 1
    same = (r >> (lg + 1)) == (c >> (lg + 1))
    r_half = (r >> lg) & 1
    c_half = (c >> lg) & 1
    return same & ((r_half == 0) & (c_half == 1) if upper else (r_half == 1) & (c_half == 0))


def _dn_kernel(q_ref, k_ref, v_ref, z_ref, g_ref, cq_ref, ck_ref, cv_ref, alog_ref, dtb_ref, nw_ref,
               *rest, seq, heads, zero_init, emit_state):
    rest = list(rest)
    s0_ref = None if zero_init else rest.pop(0)
    o_ref = rest.pop(0)
    st_ref = rest.pop(0) if emit_state else None
    h0 = pl.program_id(1) * heads
    n = DN_CHUNK
    n_blk = seq // n
    row = lax.broadcasted_iota(jnp.int32, (seq, LANES), 0)
    lane = lax.broadcasted_iota(jnp.int32, (seq, LANES), 1)
    r2 = lax.broadcasted_iota(jnp.int32, (n, n), 0)
    c2 = lax.broadcasted_iota(jnp.int32, (n, n), 1)
    eye = (r2 == c2).astype(F32)
    incl = (r2 >= c2, r2 <= c2)
    strict = (r2 > c2, r2 < c2)
    units = [(hh, blk, d) for hh in range(heads) for blk in range(n_blk) for d in (0, 1)]

    def conv_silu(x, cw):
        prev = jnp.where(row >= 1, pltpu.roll(x, 1, axis=0), 0.0)
        nxt = jnp.where(row <= seq - 2, pltpu.roll(x, seq - 1, axis=0), 0.0)
        return _silu(prev * cw[0:1] + x * cw[1:2] + nxt * cw[2:3])

    def l2n(x):
        return x * lax.rsqrt(jnp.sum(x * x, axis=-1, keepdims=True) + EPS)

    def col(a, idx):
        return jnp.sum(jnp.where(lane == idx, a, 0.0), axis=1, keepdims=True)

    gates = g_ref[...]
    beta_all = jax.nn.sigmoid(gates)
    xg = gates + dtb_ref[...]
    softplus = jnp.maximum(xg, 0.0) + jnp.log1p(jnp.exp(-jnp.abs(xg)))
    g_all = -jnp.exp(alog_ref[...]) * softplus
    r_in = row & (n - 1)
    pre, suf = g_all, g_all
    s = 1
    while s < n:
        pre = pre + jnp.where(r_in >= s, pltpu.roll(pre, s, axis=0), 0.0)
        suf = suf + jnp.where(r_in < n - s, pltpu.roll(suf, seq - s, axis=0), 0.0)
        s *= 2

    qs, kn, vv, beta, gc = [], [], [], [], []
    for hh in range(heads):
        cs = slice(hh * LANES, (hh + 1) * LANES)
        qs.append(l2n(conv_silu(q_ref[:, cs], cq_ref[:, cs])) * (DN_DK ** -0.5))
        kn.append(l2n(conv_silu(k_ref[:, cs], ck_ref[:, cs])))
        vv.append(conv_silu(v_ref[:, cs], cv_ref[:, cs]))
        beta.append((col(beta_all, h0 + hh), col(beta_all, DN_HEADS + h0 + hh)))
        gc.append((col(pre, 2 * DN_HEADS + h0 + hh), col(suf, 3 * DN_HEADS + h0 + hh)))

    kk, qk = {}, {}
    for hh in range(heads):
        for blk in range(n_blk):
            rs = slice(blk * n, (blk + 1) * n)
            kk[hh, blk] = _dot_nt(kn[hh][rs], kn[hh][rs])
            qk[hh, blk] = _dot_nt(qs[hh][rs], kn[hh][rs])

    decay, m, t = {}, {}, {}
    for u in units:
        hh, blk, d = u
        rs = slice(blk * n, (blk + 1) * n)
        g_wide = jnp.broadcast_to(gc[hh][d][rs], (n, n))
        decay[u] = jnp.where(incl[d], jnp.exp(g_wide - g_wide.T), 0.0)
        m[u] = jnp.where(strict[d], beta[hh][d][rs] * kk[hh, blk] * decay[u], 0.0)
        t[u] = eye - jnp.where(_link_mask(r2, c2, 1, d == 1), m[u], 0.0)

    s = 2
    while s < n:
        x = {}
        for u in units:
            x[u] = _dot(jnp.where(_link_mask(r2, c2, s, u[2] == 1), m[u], 0.0), t[u])
        for u in units:
            t[u] = t[u] - _dot(t[u], x[u])
        s *= 2

    resid = {}
    for u in units:
        resid[u] = eye - t[u] - _dot3(_split(m[u]), _split(t[u]))
    for u in units:
        t[u] = t[u] + _dot(t[u], resid[u])

    uw, a_in, q_dec, k_dec, g_last = {}, {}, {}, {}, {}
    for u in units:
        hh, blk, d = u
        rs = slice(blk * n, (blk + 1) * n)
        gcol = gc[hh][d][rs]
        bcol = beta[hh][d][rs]
        e_g = jnp.exp(gcol)
        kb = kn[hh][rs]
        uw[u] = _dot(t[u], jnp.concatenate([vv[hh][rs] * bcol, kb * (bcol * e_g)], axis=1))
        a_in[u] = jnp.where(incl[d], qk[hh, blk] * decay[u], 0.0)
        g_last[u] = gcol[n - 1:n] if d == 0 else gcol[0:1]
        q_dec[u] = qs[hh][rs] * e_g
        k_dec[u] = kb * jnp.exp(g_last[u] - gcol)

    state = {}
    for hh in range(heads):
        for d in (0, 1):
            state[hh, d] = jnp.zeros((DN_DK, DN_DK), F32) if zero_init else s0_ref[d, hh]
    outs = {}
    for step in range(n_blk):
        chains = [(hh, step if d == 0 else n_blk - 1 - step, d) for hh in range(heads) for d in (0, 1)]
        ws_qs = {}
        for u in chains:
            ws_qs[u] = _dot(jnp.concatenate([uw[u][:, LANES:], q_dec[u]], axis=0), state[u[0], u[2]])
        for u in chains:
            v_new = uw[u][:, :LANES] - ws_qs[u][0:n]
            outs[u] = ws_qs[u][n:] + _dot(a_in[u], v_new)
            state[u[0], u[2]] = state[u[0], u[2]] * jnp.exp(g_last[u]) + _dot_tn(k_dec[u], v_new)

    for hh in range(heads):
        cs = slice(hh * LANES, (hh + 1) * LANES)
        o = (jnp.concatenate([outs[hh, blk, 0] for blk in range(n_blk)], axis=0)
             + jnp.concatenate([outs[hh, blk, 1] for blk in range(n_blk)], axis=0))
        o = o * lax.rsqrt(jnp.mean(o * o, axis=-1, keepdims=True) + EPS) * nw_ref[...]
        o_ref[:, cs] = o * _silu(z_ref[:, cs])
        if emit_state:
            st_ref[0, hh] = state[hh, 0]
            st_ref[1, hh] = state[hh, 1]


def _deltanet(p_all, conv_w, alog_row, dtb_row, norm_w, s0, seq, heads, n_seq, row_block0, emit_state):
    zero_init = s0 is None
    width = heads * LANES

    def pcol(cb):
        return pl.BlockSpec((seq, width), lambda b, h: (row_block0 + b, cb * (DN_HEADS // heads) + h))

    def ccol(cb):
        return pl.BlockSpec((3, width), lambda b, h: (0, cb * (DN_HEADS // heads) + h))

    vec = pl.BlockSpec((1, LANES), lambda b, h: (0, 0))
    state_spec = pl.BlockSpec((None, 2, heads, DN_DK, DN_DK), lambda b, h: (b, 0, h, 0, 0))
    in_specs = [pcol(COL_Q), pcol(COL_K), pcol(COL_V), pcol(COL_Z),
                pl.BlockSpec((seq, LANES), lambda b, h: (row_block0 + b, COL_G)),
                ccol(0), ccol(1), ccol(2), vec, vec, vec]
    args = [p_all, p_all, p_all, p_all, p_all, conv_w, conv_w, conv_w, alog_row, dtb_row, norm_w]
    if not zero_init:
        in_specs.append(state_spec)
        args.append(s0)
    out_specs = [pl.BlockSpec((seq, width), lambda b, h: (b, h))]
    out_shape = [jax.ShapeDtypeStruct((n_seq * seq, B_W), F32)]
    if emit_state:
        out_specs.append(state_spec)
        out_shape.append(jax.ShapeDtypeStruct((n_seq, 2, DN_HEADS, DN_DK, DN_DK), F32))
    res = pl.pallas_call(
        functools.partial(_dn_kernel, seq=seq, heads=heads, zero_init=zero_init, emit_state=emit_state),
        grid=(n_seq, DN_HEADS // heads),
        in_specs=in_specs,
        out_specs=out_specs,
        out_shape=out_shape,
        compiler_params=_params("arbitrary", "arbitrary"),
        name=f"deltanet_{seq}",
    )(*args)
    return res if emit_state else (res[0], None)


TS = 512


def _sgu_kernel(u_ref, v_ref, nw_ref, ws_ref, bt_ref, o_ref):
    u = jax.nn.gelu(u_ref[...])
    v = jax.nn.gelu(v_ref[...])
    mu = jnp.mean(v, axis=-1, keepdims=True)
    vc = v - mu
    var = jnp.mean(vc * vc, axis=-1, keepdims=True)
    vn = vc * lax.rsqrt(var + EPS) * nw_ref[...]
    bt = bt_ref[...]
    for c in range(TS // SGU_CHUNK):
        rs = slice(c * SGU_CHUNK, (c + 1) * SGU_CHUNK)
        for hd in range(SGU_HEADS):
            cs = slice(hd * LANES, (hd + 1) * LANES)
            mixed = _dot(ws_ref[hd], vn[rs, cs]) + bt[:, hd:hd + 1]
            o_ref[rs, cs] = u[rs, cs] * mixed


def _sgu(p_all, norm_w, w_s, b_t):
    return pl.pallas_call(
        _sgu_kernel,
        grid=(N_TOK // TS,),
        in_specs=[pl.BlockSpec((TS, C_W), lambda i: (i, 0)),
                  pl.BlockSpec((TS, C_W), lambda i: (i, 1)),
                  pl.BlockSpec((1, C_W), lambda i: (0, 0)),
                  pl.BlockSpec((SGU_HEADS, SGU_CHUNK, SGU_CHUNK), lambda i: (0, 0, 0)),
                  pl.BlockSpec((SGU_CHUNK, SGU_HEADS), lambda i: (0, 0))],
        out_specs=pl.BlockSpec((TS, C_W), lambda i: (i, 0)),
        out_shape=jax.ShapeDtypeStruct((N_TOK, C_W), F32),
        compiler_params=_params("arbitrary"),
        name="sgu",
    )(p_all, p_all, norm_w, w_s, b_t)


def _fnet_kernel(f_ref, cn_ref, sn_ref, cc_ref, sc_ref, w_ref, o_ref):
    f = _split(f_ref[...])
    fc = _dot3(f, _split(cc_ref[...]))
    fs = _dot3(f, _split(sc_ref[...]))
    spec = _dot3(_split(cn_ref[...]), _split(fc)) - _dot3(_split(sn_ref[...]), _split(fs))
    o_ref[...] = _dot(spec, w_ref[...])


def _fnet(p_all, cn, sn, cc_bd, sc_bd, w_bd, seq, n_seq, row_block0):
    const = lambda shape: pl.BlockSpec(shape, lambda b: (0, 0))
    return pl.pallas_call(
        _fnet_kernel,
        grid=(n_seq,),
        in_specs=[pl.BlockSpec((seq, F_W), lambda b: (row_block0 + b, 2 * C_W // F_W)),
                  const((seq, seq)), const((seq, seq)), const((F_W, F_W)), const((F_W, F_W)),
                  const((F_W, F_W))],
        out_specs=pl.BlockSpec((seq, F_W), lambda b: (b, 0)),
        out_shape=jax.ShapeDtypeStruct((n_seq * seq, F_W), F32),
        compiler_params=_params("arbitrary"),
        name=f"fnet_{seq}",
    )(p_all, cn, sn, cc_bd, sc_bd, w_bd)


def _grid_pos_embed(n_tok):
    rows = n_tok // GRID_W
    r = jnp.broadcast_to(jnp.arange(rows, dtype=F32)[:, None], (rows, GRID_W)).reshape(-1)
    col = jnp.broadcast_to(jnp.arange(GRID_W, dtype=F32)[None, :], (rows, GRID_W)).reshape(-1)
    quarter = D_MODEL // 4
    freq = jnp.exp(-math.log(10000.0) * jnp.arange(quarter, dtype=F32) / quarter)

    def emb(p):
        ang = p[:, None] * freq[None, :]
        return jnp.concatenate([jnp.sin(ang), jnp.cos(ang)], axis=-1)

    return jnp.concatenate([emb(r), emb(col)], axis=-1)


def _dft_tables(n):
    k = jnp.arange(n, dtype=jnp.int32)
    ang = ((k[:, None] * k[None, :]) % n).astype(F32) * (2.0 * math.pi / n)
    scale = n ** -0.5
    return jnp.cos(ang) * scale, jnp.sin(ang) * scale


def _block_diag(blocks):
    g, a, b = blocks.shape
    eye = jnp.eye(g, dtype=blocks.dtype)
    return (eye[:, None, :, None] * blocks[:, :, None, :]).reshape(g * a, g * b)


def _even_layout(w):
    qkvz = w[:, A_W:A_W + 4 * QK_W]
    pad = jnp.zeros((w.shape[0], P_EVEN_PAD - P_EVEN), w.dtype)
    return jnp.concatenate([qkvz, w[:, :A_W], w[:, A_W + 4 * QK_W:], pad], axis=1)


def _lane_row(values, offset):
    return jnp.zeros((1, LANES), F32).at[0, offset:offset + values.size].set(values.reshape(-1))


def kernel(x_prompt, x_sample, state_delta, c, c_ctx, ffn1_norm, ffn1_w_gate, ffn1_w_up, ffn1_w_down,
           mix_norm, ffn2_norm, ffn2_w_gate, ffn2_w_up, ffn2_w_down, ada_w, ada_b, ev_w_in, ev_w_out,
           pool_w, pool_scale, dn_conv_w, dn_a_log, dn_dt_bias, dn_norm_w, od_w_in, od_w_out, sgu_norm,
           sgu_w, sgu_b, fnet_w, final_norm):
    cond8 = jnp.zeros((SUBLANES, D_MODEL), F32).at[0].set(c_ctx).at[1:1 + DEC_BATCH].set(c)
    mods = _ada(cond8, ada_w, ada_b)
    mods = mods[:, :1 + DEC_BATCH].reshape(DEPTH, 1 + DEC_BATCH, N_MOD, 1, D_MODEL).transpose(0, 2, 1, 3, 4)

    x = _assemble(x_prompt.reshape(CTX_TOK, D_MODEL), x_sample.reshape(LAT_TOK, D_MODEL),
                  _grid_pos_embed(DEC_SEQ))

    dft = {n: _dft_tables(n) for n in (SEQ, DEC_SEQ)}
    cc, sc = _dft_tables(FN_CH)
    cc_bd = _block_diag(jnp.broadcast_to(cc, (F_W // FN_CH, FN_CH, FN_CH)))
    sc_bd = _block_diag(jnp.broadcast_to(sc, (F_W // FN_CH, FN_CH, FN_CH)))
    groups = ((SEQ, BATCH, 0, DN_HEADS), (DEC_SEQ, DEC_BATCH, CTX_TOK // DEC_SEQ, 1))

    states = []
    for layer in range(DEPTH):
        x = _ffn(x, mods, layer, 0, ffn1_norm, ffn1_w_gate, ffn1_w_up, ffn1_w_down, final_norm, False)
        if layer % 2 == 0:
            e = layer // 2
            p_all = _mixin(x, mods, layer, mix_norm, _even_layout(ev_w_in[e]), P_EVEN_PAD // 3)
            w_bd = _block_diag(pool_w[e])
            alog_row = _lane_row(dn_a_log[e], 2 * DN_HEADS)
            dtb_row = _lane_row(dn_dt_bias[e], 2 * DN_HEADS)
            ya, yb = [], []
            for seq, n_seq, rb0, dn_heads in groups:
                ctx = rb0 == 0
                ya.append(_pool(p_all, w_bd, pool_scale[e][None], seq, n_seq, rb0))
                o, st = _deltanet(p_all, dn_conv_w[e], alog_row, dtb_row, dn_norm_w[e][None],
                                  None if ctx else state_delta[:, e], seq, dn_heads, n_seq, rb0, ctx)
                yb.append(o)
                if ctx:
                    states.append(st)
            x = _mixout(x, jnp.concatenate(ya, axis=0), jnp.concatenate(yb, axis=0), mods, layer, ev_w_out[e])
        else:
            j = layer // 2
            p_all = _mixin(x, mods, layer, mix_norm, od_w_in[j], P_ODD // 2)
            yc = _sgu(p_all, sgu_norm[j][None], sgu_w[j], sgu_b[j].T)
            w_bd = _block_diag(fnet_w[j])
            yd = [_fnet(p_all, dft[seq][0], dft[seq][1], cc_bd, sc_bd, w_bd, seq, n_seq, rb0)
                  for seq, n_seq, rb0, _ in groups]
            x = _mixout(x, yc, jnp.concatenate(yd, axis=0), mods, layer, od_w_out[j])
        x = _ffn(x, mods, layer, 2, ffn2_norm, ffn2_w_gate, ffn2_w_up, ffn2_w_down, final_norm,
                 layer == DEPTH - 1)

    y_prompt = x[:CTX_TOK].reshape(BATCH, SEQ, D_MODEL)
    y_sample = x[CTX_TOK:].reshape(DEC_BATCH, DEC_SEQ, D_MODEL)
    return y_prompt, y_sample, jnp.stack(states, axis=1)
```

```python
import functools
import math

import numpy as np
import jax
import jax.numpy as jnp
from jax import lax
from jax.experimental import pallas as pl
from jax.experimental.pallas import tpu as pltpu

F32 = jnp.float32
BF16 = jnp.bfloat16

D_MODEL = 1024
BATCH = 16
SEQ = 256
DEPTH = 4
DEC_BATCH = 2
DEC_SEQ = 1024
GRID_W = 64
EPS = 1e-6
D_FF = 2816
N_MOD = 9
N_EVEN = (DEPTH + 1) // 2
POOL_SIZES = (2, 4, 8, 16)
POOL_CH = 64
A_W = 256
DN_HEADS = 6
DN_DK = 128
QK_W = DN_HEADS * DN_DK
B_W = QK_W
GATE_W = 4 * DN_HEADS
P_EVEN = A_W + 4 * QK_W + GATE_W
SGU_CHUNK = 128
SGU_HEADS = 6
C_W = 768
FN_CH = 64
F_W = 256
P_ODD = 2 * C_W + F_W

CTX_TOK = BATCH * SEQ
LAT_TOK = DEC_BATCH * DEC_SEQ
N_TOK = CTX_TOK + LAT_TOK

LANES = 128
SUBLANES = 8
VMEM_LIMIT = 56 * 1024 * 1024

TM = 1024
N_ROW_TILES = N_TOK // TM
CTX_TILES = CTX_TOK // TM
TF = 256
DN_CHUNK = 128
COL_Q, COL_K, COL_V, COL_Z = 0, 1, 2, 3


def _cond_of_tile(i, tm):
    return jnp.where(i < CTX_TOK // tm, 0, 1 + (i - CTX_TOK // tm) * tm // DEC_SEQ)


def _silu(x):
    return x * jax.nn.sigmoid(x)


def _dot(a, b):
    return jnp.dot(a.astype(BF16), b.astype(BF16), preferred_element_type=F32)


def _dot_nt(a, b):
    return lax.dot_general(a.astype(BF16), b.astype(BF16), (((1,), (1,)), ((), ())),
                           preferred_element_type=F32)


def _dot_tn(a, b):
    return lax.dot_general(a.astype(BF16), b.astype(BF16), (((0,), (0,)), ((), ())),
                           preferred_element_type=F32)


def _split(a):
    hi = a.astype(BF16)
    lo = (a - hi.astype(F32)).astype(BF16)
    return hi, lo


def _dot3(a, b):
    ah, al = a
    bh, bl = b
    return (jnp.dot(ah, bh, preferred_element_type=F32)
            + (jnp.dot(al, bh, preferred_element_type=F32)
               + jnp.dot(ah, bl, preferred_element_type=F32)))


def _rms(x, w):
    return x * lax.rsqrt(jnp.mean(x * x, axis=-1, keepdims=True) + EPS) * w


def _params(*sem):
    return pltpu.CompilerParams(dimension_semantics=sem, vmem_limit_bytes=VMEM_LIMIT)


def _ctx_rows(tm, width):
    return pl.BlockSpec((tm, width), lambda i, *_: (jnp.minimum(i, CTX_TOK // tm - 1), 0))


def _lat_rows(tm, width):
    return pl.BlockSpec((tm, width), lambda i, *_: (jnp.maximum(i - CTX_TOK // tm, 0), 0))


def _ada_kernel(c_ref, w_ref, b_ref, o_ref):
    o_ref[0] = _dot(_silu(c_ref[...]), w_ref[0]) + b_ref[0]


def _ada(cond8, ada_w, ada_b):
    tn = 1152
    n = N_MOD * D_MODEL
    return pl.pallas_call(
        _ada_kernel,
        grid=(DEPTH, n // tn),
        in_specs=[pl.BlockSpec((SUBLANES, D_MODEL), lambda l, j: (0, 0)),
                  pl.BlockSpec((1, D_MODEL, tn), lambda l, j: (l, 0, j)),
                  pl.BlockSpec((1, 1, tn), lambda l, j: (l, 0, j))],
        out_specs=pl.BlockSpec((1, SUBLANES, tn), lambda l, j: (l, 0, j)),
        out_shape=jax.ShapeDtypeStruct((DEPTH, SUBLANES, n), F32),
        compiler_params=_params("arbitrary", "arbitrary"),
        name="ada",
    )(cond8, ada_w, ada_b.reshape(DEPTH, 1, n))


def _mod_spec(layer, which, tm=TM):
    return pl.BlockSpec((None, None, None, 1, D_MODEL),
                        lambda i, *_: (layer, which, _cond_of_tile(i, tm), 0, 0))


def _layer_vec(layer):
    return pl.BlockSpec((None, 1, D_MODEL), lambda i, *_: (layer, 0, 0))


def _ffn_kernel(*refs, first, final):
    refs = list(refs)
    if first:
        xp_ref, xs_ref, pos_ref = refs[:3]
        refs = refs[3:]
    else:
        x_ref = refs.pop(0)
    nw_ref, sh_ref, sc_ref, gt_ref, wg_ref, wu_ref, wd_ref = refs[:7]
    refs = refs[7:]
    if final:
        fn_ref, oc_ref, ol_ref = refs[:3]
        refs = refs[3:]
    else:
        o_ref = refs.pop(0)
    h_scr, acc_scr = refs[:2]
    if first:
        x_ref = refs[2]
    i = pl.program_id(0)
    j = pl.program_id(1)

    @pl.when(j == 0)
    def _():
        if first:
            @pl.when(i < CTX_TILES)
            def _():
                x_ref[...] = xp_ref[...]

            @pl.when(i >= CTX_TILES)
            def _():
                x_ref[...] = xs_ref[...] + pos_ref[...]

        h = _rms(x_ref[...], nw_ref[...]) * (1.0 + sc_ref[...]) + sh_ref[...]
        h_scr[...] = h.astype(BF16)
        acc_scr[...] = jnp.zeros_like(acc_scr)

    h = h_scr[...]
    g = jnp.dot(h, wg_ref[...].astype(BF16), preferred_element_type=F32)
    u = jnp.dot(h, wu_ref[...].astype(BF16), preferred_element_type=F32)
    acc_scr[...] += _dot(_silu(g) * u, wd_ref[...])

    @pl.when(j == pl.num_programs(1) - 1)
    def _():
        y = x_ref[...] + 0.5 * gt_ref[...] * acc_scr[...]
        if final:
            y = _rms(y, fn_ref[...])

            @pl.when(i < CTX_TILES)
            def _():
                oc_ref[...] = y

            @pl.when(i >= CTX_TILES)
            def _():
                ol_ref[...] = y
        else:
            o_ref[...] = y


def _ffn(x, mods, layer, sub, norm_w, w_gate, w_up, w_down, final_norm=None):
    first = isinstance(x, tuple)
    final = final_norm is not None
    row = pl.BlockSpec((TM, D_MODEL), lambda i, j: (i, 0))
    if first:
        in_specs = [_ctx_rows(TM, D_MODEL), _lat_rows(TM, D_MODEL),
                    pl.BlockSpec((TM, D_MODEL), lambda i, j: (0, 0))]
        args = list(x)
    else:
        in_specs, args = [row], [x]
    in_specs += [_layer_vec(layer),
                 _mod_spec(layer, 3 * sub), _mod_spec(layer, 3 * sub + 1), _mod_spec(layer, 3 * sub + 2),
                 pl.BlockSpec((None, D_MODEL, TF), lambda i, j: (layer, 0, j)),
                 pl.BlockSpec((None, D_MODEL, TF), lambda i, j: (layer, 0, j)),
                 pl.BlockSpec((None, TF, D_MODEL), lambda i, j: (layer, j, 0))]
    args += [norm_w.reshape(DEPTH, 1, D_MODEL), mods, mods, mods, w_gate, w_up, w_down]
    scratch = [pltpu.VMEM((TM, D_MODEL), BF16), pltpu.VMEM((TM, D_MODEL), F32)]
    if first:
        scratch.append(pltpu.VMEM((TM, D_MODEL), F32))
    if final:
        in_specs.append(pl.BlockSpec((1, D_MODEL), lambda i, j: (0, 0)))
        args.append(final_norm[None])
        out_specs = [_ctx_rows(TM, D_MODEL), _lat_rows(TM, D_MODEL)]
        out_shape = [jax.ShapeDtypeStruct((CTX_TOK, D_MODEL), F32), jax.ShapeDtypeStruct((LAT_TOK, D_MODEL), F32)]
    else:
        out_specs = row
        out_shape = jax.ShapeDtypeStruct((N_TOK, D_MODEL), F32)
    return pl.pallas_call(
        functools.partial(_ffn_kernel, first=first, final=final),
        grid=(N_ROW_TILES, D_FF // TF),
        in_specs=in_specs,
        out_specs=out_specs,
        out_shape=out_shape,
        scratch_shapes=scratch,
        compiler_params=_params("arbitrary", "arbitrary"),
        name=f"ffn{sub}_{layer}",
    )(*args)


def _mixin_kernel(x_ref, nw_ref, sh_ref, sc_ref, w_ref, *o_refs, segments):
    h = (_rms(x_ref[...], nw_ref[...]) * (1.0 + sc_ref[...]) + sh_ref[...]).astype(BF16)
    for (lo, hi), o_ref in zip(segments, o_refs):
        o_ref[...] = jnp.dot(h, w_ref[:, lo:hi].astype(BF16), preferred_element_type=F32)


def _mixin(x, mods, layer, norm_w, w_in, w_index, segments, tm):
    width = w_in.shape[2]
    return pl.pallas_call(
        functools.partial(_mixin_kernel, segments=segments),
        grid=(N_TOK // tm,),
        in_specs=[pl.BlockSpec((tm, D_MODEL), lambda i: (i, 0)),
                  _layer_vec(layer), _mod_spec(layer, 3, tm), _mod_spec(layer, 4, tm),
                  pl.BlockSpec((None, D_MODEL, width), lambda i: (w_index, 0, 0),
                               pipeline_mode=pl.Buffered(1))],
        out_specs=[pl.BlockSpec((tm, hi - lo), lambda i: (i, 0)) for lo, hi in segments],
        out_shape=[jax.ShapeDtypeStruct((N_TOK, hi - lo), F32) for lo, hi in segments],
        compiler_params=_params("arbitrary"),
        name=f"mixin_{layer}",
    )(x, norm_w.reshape(DEPTH, 1, D_MODEL), mods, mods, w_in)


def _mixout_kernel(*refs, counts, split):
    refs = list(refs)
    x_ref = refs.pop(0)
    groups = [[refs.pop(0) for _ in range(c)] for c in counts]
    gt_ref, w_ref, o_ref = refs
    ctx = pl.program_id(0) < CTX_TILES

    def pick(rs):
        return rs[0][...] if len(rs) == 1 else jnp.where(ctx, rs[0][...], rs[1][...])

    mix = _dot(pick(groups[0]), w_ref[0:split, :]) + _dot(pick(groups[1]), w_ref[split:, :])
    o_ref[...] = x_ref[...] + gt_ref[...] * mix


def _mixout(x, ya, yb, mods, layer, w_out, w_index):
    row = lambda width: pl.BlockSpec((TM, width), lambda i: (i, 0))
    in_specs, args, counts = [row(D_MODEL)], [x], []
    for y in (ya, yb):
        parts = y if isinstance(y, tuple) else (y,)
        width = parts[0].shape[1]
        in_specs += [row(width)] if len(parts) == 1 else [_ctx_rows(TM, width), _lat_rows(TM, width)]
        args += list(parts)
        counts.append(len(parts))
    split = args[1].shape[1]
    in_specs += [_mod_spec(layer, 5),
                 pl.BlockSpec((None, D_MODEL, D_MODEL), lambda i: (w_index, 0, 0), pipeline_mode=pl.Buffered(1))]
    args += [mods, w_out]
    return pl.pallas_call(
        functools.partial(_mixout_kernel, counts=tuple(counts), split=split),
        grid=(N_ROW_TILES,),
        in_specs=in_specs,
        out_specs=row(D_MODEL),
        out_shape=jax.ShapeDtypeStruct((N_TOK, D_MODEL), F32),
        compiler_params=_params("arbitrary"),
        name=f"mixout_{layer}",
    )(*args)


POOL_PAD = 8


def _pool_kernel(p_ref, w_ref, scale_ref, o_ref, *, seq):
    p = p_ref[...]
    zeros = jnp.zeros((POOL_PAD, A_W), F32)
    xp = jnp.concatenate([zeros, p, zeros], axis=0)
    n_pad = seq + 2 * POOL_PAD

    def back(a, s):
        return pltpu.roll(a, s, axis=0)

    def fwd(a, s):
        return pltpu.roll(a, n_pad - s, axis=0)

    w2 = xp + back(xp, 1)
    w4 = back(w2, 1) + fwd(w2, 1)
    w8 = back(w4, 2) + fwd(w4, 2)
    w16 = back(w8, 4) + fwd(w8, 4)
    sl = slice(POOL_PAD, POOL_PAD + seq)
    group = lax.broadcasted_iota(jnp.int32, (seq, A_W), 1) >> (POOL_CH.bit_length() - 1)
    t = lax.broadcasted_iota(jnp.int32, (seq, A_W), 0)
    wsum = jnp.where(group == 0, w2[sl], jnp.where(group == 1, w4[sl], jnp.where(group == 2, w8[sl], w16[sl])))
    half = jnp.left_shift(1, group)
    cnt = (jnp.minimum(t + half, seq) - jnp.maximum(t - half, 0)).astype(F32)
    d = wsum / cnt - p
    o_ref[...] = _dot(d, w_ref[...]) * scale_ref[...]


def _pool(p_pool, w_bd, scale, seq, n_seq, row_block0):
    return pl.pallas_call(
        functools.partial(_pool_kernel, seq=seq),
        grid=(n_seq,),
        in_specs=[pl.BlockSpec((seq, A_W), lambda b: (row_block0 + b, 0)),
                  pl.BlockSpec((A_W, A_W), lambda b: (0, 0)),
                  pl.BlockSpec((1, A_W), lambda b: (0, 0))],
        out_specs=pl.BlockSpec((seq, A_W), lambda b: (b, 0)),
        out_shape=jax.ShapeDtypeStruct((n_seq * seq, A_W), F32),
        compiler_params=_params("arbitrary"),
        name=f"pool_{seq}",
    )(p_pool, w_bd, scale)


def _link_mask(r, c, s, upper):
    lg = s.bit_length() - 1
    same = (r >> (lg + 1)) == (c >> (lg + 1))
    r_half = (r >> lg) & 1
    c_half = (c >> lg) & 1
    return same & ((r_half == 0) & (c_half == 1) if upper else (r_half == 1) & (c_half == 0))


def _dn_kernel(q_ref, k_ref, v_ref, z_ref, g_ref, cq_ref, ck_ref, cv_ref, alog_ref, dtb_ref, nw_ref,
               *rest, seq, heads, zero_init, state_slot):
    rest = list(rest)
    s0_ref = None if zero_init else rest.pop(0)
    if state_slot:
        rest.pop(0)
    o_ref = rest.pop(0)
    st_ref = rest.pop(0) if state_slot is not None else None
    h0 = pl.program_id(1) * heads
    n = DN_CHUNK
    n_blk = seq // n
    row = lax.broadcasted_iota(jnp.int32, (seq, LANES), 0)
    g_row = lax.broadcasted_iota(jnp.int32, (seq, GATE_W), 0)
    g_lane = lax.broadcasted_iota(jnp.int32, (seq, GATE_W), 1)
    r2 = lax.broadcasted_iota(jnp.int32, (n, n), 0)
    c2 = lax.broadcasted_iota(jnp.int32, (n, n), 1)
    eye = (r2 == c2).astype(F32)
    incl = (r2 >= c2, r2 <= c2)
    strict = (r2 > c2, r2 < c2)
    units = [(hh, blk, d) for hh in range(heads) for blk in range(n_blk) for d in (0, 1)]

    def conv_silu(x, cw):
        prev = jnp.where(row >= 1, pltpu.roll(x, 1, axis=0), 0.0)
        nxt = jnp.where(row <= seq - 2, pltpu.roll(x, seq - 1, axis=0), 0.0)
        return _silu(prev * cw[0:1] + x * cw[1:2] + nxt * cw[2:3])

    def l2n(x):
        return x * lax.rsqrt(jnp.sum(x * x, axis=-1, keepdims=True) + EPS)

    def col(a, idx):
        return jnp.sum(jnp.where(g_lane == idx, a, 0.0), axis=1, keepdims=True)

    gates = g_ref[...]
    beta_all = jax.nn.sigmoid(gates)
    xg = gates + dtb_ref[...]
    softplus = jnp.maximum(xg, 0.0) + jnp.log1p(jnp.exp(-jnp.abs(xg)))
    g_all = -jnp.exp(alog_ref[...]) * softplus
    r_in = g_row & (n - 1)
    pre, suf = g_all, g_all
    s = 1
    while s < n:
        pre = pre + jnp.where(r_in >= s, pltpu.roll(pre, s, axis=0), 0.0)
        suf = suf + jnp.where(r_in < n - s, pltpu.roll(suf, seq - s, axis=0), 0.0)
        s *= 2

    qs, kn, vv, beta, gc = [], [], [], [], []
    for hh in range(heads):
        cs = slice(hh * LANES, (hh + 1) * LANES)
        qs.append(l2n(conv_silu(q_ref[:, cs], cq_ref[:, cs])) * (DN_DK ** -0.5))
        kn.append(l2n(conv_silu(k_ref[:, cs], ck_ref[:, cs])))
        vv.append(conv_silu(v_ref[:, cs], cv_ref[:, cs]))
        beta.append((col(beta_all, h0 + hh), col(beta_all, DN_HEADS + h0 + hh)))
        gc.append((col(pre, 2 * DN_HEADS + h0 + hh), col(suf, 3 * DN_HEADS + h0 + hh)))

    kk, qk = {}, {}
    for hh in range(heads):
        for blk in range(n_blk):
            rs = slice(blk * n, (blk + 1) * n)
            kk[hh, blk] = _dot_nt(kn[hh][rs], kn[hh][rs])
            qk[hh, blk] = _dot_nt(qs[hh][rs], kn[hh][rs])

    decay, m, t = {}, {}, {}
    for u in units:
        hh, blk, d = u
        rs = slice(blk * n, (blk + 1) * n)
        g_wide = jnp.broadcast_to(gc[hh][d][rs], (n, n))
        decay[u] = jnp.where(incl[d], jnp.exp(g_wide - g_wide.T), 0.0)
        m[u] = jnp.where(strict[d], beta[hh][d][rs] * kk[hh, blk] * decay[u], 0.0)
        t[u] = eye - jnp.where(_link_mask(r2, c2, 1, d == 1), m[u], 0.0)

    s = 2
    while s < n:
        x = {}
        for u in units:
            x[u] = _dot(jnp.where(_link_mask(r2, c2, s, u[2] == 1), m[u], 0.0), t[u])
        for u in units:
            t[u] = t[u] - _dot(t[u], x[u])
        s *= 2

    resid = {}
    for u in units:
        resid[u] = eye - t[u] - _dot3(_split(m[u]), _split(t[u]))
    for u in units:
        t[u] = t[u] + _dot(t[u], resid[u])

    uw, a_in, q_dec, k_dec, g_last = {}, {}, {}, {}, {}
    for u in units:
        hh, blk, d = u
        rs = slice(blk * n, (blk + 1) * n)
        gcol = gc[hh][d][rs]
        bcol = beta[hh][d][rs]
        e_g = jnp.exp(gcol)
        kb = kn[hh][rs]
        uw[u] = _dot(t[u], jnp.concatenate([vv[hh][rs] * bcol, kb * (bcol * e_g)], axis=1))
        a_in[u] = jnp.where(incl[d], qk[hh, blk] * decay[u], 0.0)
        g_last[u] = gcol[n - 1:n] if d == 0 else gcol[0:1]
        q_dec[u] = qs[hh][rs] * e_g
        k_dec[u] = kb * jnp.exp(g_last[u] - gcol)

    state = {}
    for hh in range(heads):
        for d in (0, 1):
            state[hh, d] = jnp.zeros((DN_DK, DN_DK), F32) if zero_init else s0_ref[d, hh]
    outs = {}
    for step in range(n_blk):
        chains = [(hh, step if d == 0 else n_blk - 1 - step, d) for hh in range(heads) for d in (0, 1)]
        ws_qs = {}
        for u in chains:
            ws_qs[u] = _dot(jnp.concatenate([uw[u][:, LANES:], q_dec[u]], axis=0), state[u[0], u[2]])
        for u in chains:
            v_new = uw[u][:, :LANES] - ws_qs[u][0:n]
            outs[u] = ws_qs[u][n:] + _dot(a_in[u], v_new)
            state[u[0], u[2]] = state[u[0], u[2]] * jnp.exp(g_last[u]) + _dot_tn(k_dec[u], v_new)

    for hh in range(heads):
        cs = slice(hh * LANES, (hh + 1) * LANES)
        o = (jnp.concatenate([outs[hh, blk, 0] for blk in range(n_blk)], axis=0)
             + jnp.concatenate([outs[hh, blk, 1] for blk in range(n_blk)], axis=0))
        o = o * lax.rsqrt(jnp.mean(o * o, axis=-1, keepdims=True) + EPS) * nw_ref[...]
        o_ref[:, cs] = o * _silu(z_ref[:, cs])
        if state_slot == 0:
            st_ref[0, 0, hh] = state[hh, 0]
            st_ref[0, 1, hh] = state[hh, 1]
            st_ref[1:, :, hh] = jnp.zeros((N_EVEN - 1, 2, DN_DK, DN_DK), F32)
        elif state_slot:
            st_ref[0, hh] = state[hh, 0]
            st_ref[1, hh] = state[hh, 1]


def _deltanet(qkvz, gates, conv_w, e, alog_row, dtb_row, norm_w, s0, seq, heads, n_seq, row_block0,
              state_slot=None, states=None):
    zero_init = s0 is None
    width = heads * LANES
    per = DN_HEADS // heads

    def pcol(cb):
        return pl.BlockSpec((seq, width), lambda b, h: (row_block0 + b, cb * per + h))

    def ccol(cb):
        return pl.BlockSpec((None, 3, width), lambda b, h: (e, 0, cb * per + h))

    gvec = pl.BlockSpec((1, GATE_W), lambda b, h: (0, 0))
    in_specs = [pcol(COL_Q), pcol(COL_K), pcol(COL_V), pcol(COL_Z),
                pl.BlockSpec((seq, GATE_W), lambda b, h: (row_block0 + b, 0)),
                ccol(0), ccol(1), ccol(2), gvec, gvec,
                pl.BlockSpec((None, 1, LANES), lambda b, h: (e, 0, 0))]
    args = [qkvz, qkvz, qkvz, qkvz, gates, conv_w, conv_w, conv_w, alog_row, dtb_row,
            norm_w.reshape(N_EVEN, 1, LANES)]
    if not zero_init:
        in_specs.append(pl.BlockSpec((None, None, 2, heads, DN_DK, DN_DK), lambda b, h: (b, e, 0, h, 0, 0)))
        args.append(s0)
    aliases = {}
    if state_slot:
        aliases = {len(args): 1}
        in_specs.append(pl.BlockSpec(memory_space=pl.ANY))
        args.append(states)
    out_specs = [pl.BlockSpec((seq, width), lambda b, h: (b, h))]
    out_shape = [jax.ShapeDtypeStruct((n_seq * seq, B_W), F32)]
    if state_slot is not None:
        if state_slot == 0:
            out_specs.append(pl.BlockSpec((None, N_EVEN, 2, heads, DN_DK, DN_DK), lambda b, h: (b, 0, 0, h, 0, 0)))
        else:
            out_specs.append(pl.BlockSpec((None, None, 2, heads, DN_DK, DN_DK),
                                          lambda b, h: (b, state_slot, 0, h, 0, 0)))
        out_shape.append(jax.ShapeDtypeStruct((n_seq, N_EVEN, 2, DN_HEADS, DN_DK, DN_DK), F32))
    res = pl.pallas_call(
        functools.partial(_dn_kernel, seq=seq, heads=heads, zero_init=zero_init, state_slot=state_slot),
        grid=(n_seq, per),
        in_specs=in_specs,
        out_specs=out_specs,
        out_shape=out_shape,
        input_output_aliases=aliases,
        compiler_params=_params("arbitrary", "arbitrary"),
        name=f"deltanet_{seq}",
    )(*args)
    return res if state_slot is not None else (res[0], None)


TS = 512


def _sgu_kernel(u_ref, v_ref, nw_ref, ws_ref, bt_ref, o_ref):
    u = jax.nn.gelu(u_ref[...])
    v = jax.nn.gelu(v_ref[...])
    mu = jnp.mean(v, axis=-1, keepdims=True)
    vc = v - mu
    var = jnp.mean(vc * vc, axis=-1, keepdims=True)
    vn = vc * lax.rsqrt(var + EPS) * nw_ref[...]
    bt = bt_ref[...]
    for c in range(TS // SGU_CHUNK):
        rs = slice(c * SGU_CHUNK, (c + 1) * SGU_CHUNK)
        for hd in range(SGU_HEADS):
            cs = slice(hd * LANES, (hd + 1) * LANES)
            mixed = _dot(ws_ref[hd], vn[rs, cs]) + bt[:, hd:hd + 1]
            o_ref[rs, cs] = u[rs, cs] * mixed


def _sgu(uv, norm_w, w_s, b_t):
    return pl.pallas_call(
        _sgu_kernel,
        grid=(N_TOK // TS,),
        in_specs=[pl.BlockSpec((TS, C_W), lambda i: (i, 0)),
                  pl.BlockSpec((TS, C_W), lambda i: (i, 1)),
                  pl.BlockSpec((1, C_W), lambda i: (0, 0)),
                  pl.BlockSpec((SGU_HEADS, SGU_CHUNK, SGU_CHUNK), lambda i: (0, 0, 0)),
                  pl.BlockSpec((SGU_CHUNK, SGU_HEADS), lambda i: (0, 0))],
        out_specs=pl.BlockSpec((TS, C_W), lambda i: (i, 0)),
        out_shape=jax.ShapeDtypeStruct((N_TOK, C_W), F32),
        compiler_params=_params("arbitrary"),
        name="sgu",
    )(uv, uv, norm_w, w_s, b_t)


def _fnet_kernel(f_ref, cn_ref, sn_ref, cc_ref, sc_ref, w_ref, o_ref):
    f = _split(f_ref[...])
    fc = _dot3(f, _split(cc_ref[...]))
    fs = _dot3(f, _split(sc_ref[...]))
    spec = _dot3(_split(cn_ref[...]), _split(fc)) - _dot3(_split(sn_ref[...]), _split(fs))
    o_ref[...] = _dot(spec, w_ref[...])


def _fnet(f_all, cn, sn, cc_bd, sc_bd, w_bd, seq, n_seq, row_block0):
    const = lambda shape: pl.BlockSpec(shape, lambda b: (0, 0))
    return pl.pallas_call(
        _fnet_kernel,
        grid=(n_seq,),
        in_specs=[pl.BlockSpec((seq, F_W), lambda b: (row_block0 + b, 0)),
                  const((seq, seq)), const((seq, seq)), const((F_W, F_W)), const((F_W, F_W)),
                  const((F_W, F_W))],
        out_specs=pl.BlockSpec((seq, F_W), lambda b: (b, 0)),
        out_shape=jax.ShapeDtypeStruct((n_seq * seq, F_W), F32),
        compiler_params=_params("arbitrary"),
        name=f"fnet_{seq}",
    )(f_all, cn, sn, cc_bd, sc_bd, w_bd)


def _grid_pos_embed(n_tok):
    rows = n_tok // GRID_W
    r = np.repeat(np.arange(rows, dtype=np.float64), GRID_W)
    col = np.tile(np.arange(GRID_W, dtype=np.float64), rows)
    quarter = D_MODEL // 4
    freq = np.exp(-math.log(10000.0) * np.arange(quarter, dtype=np.float64) / quarter)

    def emb(p):
        ang = p[:, None] * freq[None, :]
        return np.concatenate([np.sin(ang), np.cos(ang)], axis=-1)

    return np.concatenate([emb(r), emb(col)], axis=-1).astype(np.float32)


def _dft_tables(n):
    k = np.arange(n, dtype=np.int64)
    ang = ((k[:, None] * k[None, :]) % n).astype(np.float64) * (2.0 * math.pi / n)
    scale = n ** -0.5
    return (np.cos(ang) * scale).astype(np.float32), (np.sin(ang) * scale).astype(np.float32)


def _block_diag(blocks):
    g, a, b = blocks.shape
    eye = jnp.eye(g, dtype=blocks.dtype)
    return (eye[:, None, :, None] * blocks[:, :, None, :]).reshape(g * a, g * b)


def _gate_row(values):
    return jnp.concatenate([jnp.zeros((2 * DN_HEADS,), F32), values.reshape(-1)])[None]


def kernel(x_prompt, x_sample, state_delta, c, c_ctx, ffn1_norm, ffn1_w_gate, ffn1_w_up, ffn1_w_down,
           mix_norm, ffn2_norm, ffn2_w_gate, ffn2_w_up, ffn2_w_down, ada_w, ada_b, ev_w_in, ev_w_out,
           pool_w, pool_scale, dn_conv_w, dn_a_log, dn_dt_bias, dn_norm_w, od_w_in, od_w_out, sgu_norm,
           sgu_w, sgu_b, fnet_w, final_norm):
    cond8 = jnp.zeros((SUBLANES, D_MODEL), F32).at[0].set(c_ctx).at[1:1 + DEC_BATCH].set(c)
    mods = _ada(cond8, ada_w, ada_b)
    mods = mods[:, :1 + DEC_BATCH].reshape(DEPTH, 1 + DEC_BATCH, N_MOD, 1, D_MODEL).transpose(0, 2, 1, 3, 4)

    dft = {n: _dft_tables(n) for n in (SEQ, DEC_SEQ)}
    cc, sc = _dft_tables(FN_CH)
    cc_bd = np.kron(np.eye(F_W // FN_CH, dtype=np.float32), cc)
    sc_bd = np.kron(np.eye(F_W // FN_CH, dtype=np.float32), sc)
    groups = ((SEQ, BATCH, 0, DN_HEADS), (DEC_SEQ, DEC_BATCH, CTX_TOK // DEC_SEQ, 1))
    even_segments = ((A_W, A_W + 4 * QK_W), (0, A_W), (A_W + 4 * QK_W, P_EVEN))
    odd_segments = ((0, 2 * C_W), (2 * C_W, P_ODD))

    x = (x_prompt.reshape(CTX_TOK, D_MODEL), x_sample.reshape(LAT_TOK, D_MODEL), _grid_pos_embed(DEC_SEQ))
    states = None
    for layer in range(DEPTH):
        x = _ffn(x, mods, layer, 0, ffn1_norm, ffn1_w_gate, ffn1_w_up, ffn1_w_down)
        if layer % 2 == 0:
            e = layer // 2
            qkvz, p_pool, gates = _mixin(x, mods, layer, mix_norm, ev_w_in, e, even_segments, 512)
            w_bd = _block_diag(pool_w[e])
            alog_row = _gate_row(dn_a_log[e])
            dtb_row = _gate_row(dn_dt_bias[e])
            ya, yb = [], []
            for seq, n_seq, rb0, dn_heads in groups:
                ctx = rb0 == 0
                ya.append(_pool(p_pool, w_bd, pool_scale[e][None], seq, n_seq, rb0))
                o, st = _deltanet(qkvz, gates, dn_conv_w, e, alog_row, dtb_row, dn_norm_w,
                                  None if ctx else state_delta, seq, dn_heads, n_seq, rb0,
                                  state_slot=e if ctx else None, states=states)
                yb.append(o)
                if ctx:
                    states = st
            x = _mixout(x, tuple(ya), tuple(yb), mods, layer, ev_w_out, e)
        else:
            j = layer // 2
            uv, f_all = _mixin(x, mods, layer, mix_norm, od_w_in, j, odd_segments, TM)
            yc = _sgu(uv, sgu_norm[j][None], sgu_w[j], sgu_b[j].T)
            w_bd = _block_diag(fnet_w[j])
            yd = tuple(_fnet(f_all, dft[seq][0], dft[seq][1], cc_bd, sc_bd, w_bd, seq, n_seq, rb0)
                       for seq, n_seq, rb0, _ in groups)
            x = _mixout(x, yc, yd, mods, layer, od_w_out, j)
        x = _ffn(x, mods, layer, 2, ffn2_norm, ffn2_w_gate, ffn2_w_up, ffn2_w_down,
                 final_norm if layer == DEPTH - 1 else None)

    y_prompt, y_sample = x
    return (y_prompt.reshape(BATCH, SEQ, D_MODEL), y_sample.reshape(DEC_BATCH, DEC_SEQ, D_MODEL), states)
```

```python
import functools
import math

import numpy as np
import jax
import jax.numpy as jnp
from jax import lax
from jax.experimental import pallas as pl
from jax.experimental.pallas import tpu as pltpu

F32 = jnp.float32
BF16 = jnp.bfloat16

D_MODEL = 1024
BATCH = 16
SEQ = 256
DEPTH = 4
DEC_BATCH = 2
DEC_SEQ = 1024
GRID_W = 64
EPS = 1e-6
D_FF = 2816
N_MOD = 9
N_EVEN = (DEPTH + 1) // 2
POOL_SIZES = (2, 4, 8, 16)
POOL_CH = 64
A_W = 256
DN_HEADS = 6
DN_DK = 128
QK_W = DN_HEADS * DN_DK
B_W = QK_W
GATE_W = 4 * DN_HEADS
P_EVEN = A_W + 4 * QK_W + GATE_W
SGU_CHUNK = 128
SGU_HEADS = 6
C_W = 768
FN_CH = 64
F_W = 256
P_ODD = 2 * C_W + F_W

CTX_TOK = BATCH * SEQ
LAT_TOK = DEC_BATCH * DEC_SEQ
N_TOK = CTX_TOK + LAT_TOK

LANES = 128
SUBLANES = 8
VMEM_LIMIT = 56 * 1024 * 1024

TM = 1024
N_ROW_TILES = N_TOK // TM
CTX_TILES = CTX_TOK // TM
TF = 256
DN_CHUNK = 128
COL_Q, COL_K, COL_V, COL_Z = 0, 1, 2, 3


def _cond_of_tile(i, tm):
    return jnp.where(i < CTX_TOK // tm, 0, 1 + (i - CTX_TOK // tm) * tm // DEC_SEQ)


def _silu(x):
    return x * jax.nn.sigmoid(x)


def _dot(a, b):
    return jnp.dot(a.astype(BF16), b.astype(BF16), preferred_element_type=F32)


def _dot_nt(a, b):
    return lax.dot_general(a.astype(BF16), b.astype(BF16), (((1,), (1,)), ((), ())),
                           preferred_element_type=F32)


def _dot_tn(a, b):
    return lax.dot_general(a.astype(BF16), b.astype(BF16), (((0,), (0,)), ((), ())),
                           preferred_element_type=F32)


def _split(a):
    hi = a.astype(BF16)
    lo = (a - hi.astype(F32)).astype(BF16)
    return hi, lo


def _dot3(a, b):
    ah, al = a
    bh, bl = b
    return (jnp.dot(ah, bh, preferred_element_type=F32)
            + (jnp.dot(al, bh, preferred_element_type=F32)
               + jnp.dot(ah, bl, preferred_element_type=F32)))


def _rms(x, w):
    return x * lax.rsqrt(jnp.mean(x * x, axis=-1, keepdims=True) + EPS) * w


def _norm_mod(x, nw, scale, shift):
    r = lax.rsqrt(jnp.mean(x * x, axis=-1, keepdims=True) + EPS)
    return ((x * r) * (nw * (1.0 + scale)) + shift).astype(BF16)


def _params(*sem):
    return pltpu.CompilerParams(dimension_semantics=sem, vmem_limit_bytes=VMEM_LIMIT)


def _ctx_rows(tm, width):
    return pl.BlockSpec((tm, width), lambda i, *_: (jnp.minimum(i, CTX_TOK // tm - 1), 0))


def _lat_rows(tm, width):
    return pl.BlockSpec((tm, width), lambda i, *_: (jnp.maximum(i - CTX_TOK // tm, 0), 0))


def _ada_kernel(c_ref, w_ref, b_ref, o_ref):
    o_ref[0] = _dot(_silu(c_ref[...]), w_ref[0]) + b_ref[0]


def _ada(cond8, ada_w, ada_b):
    tn = 1152
    n = N_MOD * D_MODEL
    return pl.pallas_call(
        _ada_kernel,
        grid=(DEPTH, n // tn),
        in_specs=[pl.BlockSpec((SUBLANES, D_MODEL), lambda l, j: (0, 0)),
                  pl.BlockSpec((1, D_MODEL, tn), lambda l, j: (l, 0, j)),
                  pl.BlockSpec((1, 1, tn), lambda l, j: (l, 0, j))],
        out_specs=pl.BlockSpec((1, SUBLANES, tn), lambda l, j: (l, 0, j)),
        out_shape=jax.ShapeDtypeStruct((DEPTH, SUBLANES, n), F32),
        compiler_params=_params("arbitrary", "arbitrary"),
        name="ada",
    )(cond8, ada_w, ada_b.reshape(DEPTH, 1, n))


def _mod_spec(layer, which, tm=TM):
    return pl.BlockSpec((None, None, None, 1, D_MODEL),
                        lambda i, *_: (layer, which, _cond_of_tile(i, tm), 0, 0))


def _layer_vec(layer):
    return pl.BlockSpec((None, 1, D_MODEL), lambda i, *_: (layer, 0, 0))


def _ffn_kernel(*refs, first, mix_counts, mix_split, final):
    refs = list(refs)
    take = lambda k: [refs.pop(0) for _ in range(k)]
    if first:
        xp_ref, xs_ref, pos_ref = take(3)
    else:
        (x_in_ref,) = take(1)
    if mix_counts:
        mix_groups = [take(c) for c in mix_counts]
        g2_ref, wo_ref = take(2)
    nw_ref, sh_ref, sc_ref, gt_ref, wg_ref, wu_ref, wd_ref = take(7)
    if final:
        fn_ref, oc_ref, ol_ref = take(3)
    else:
        (o_ref,) = take(1)
    h_scr, acc_scr = take(2)
    x_ref = refs.pop(0) if (first or mix_counts) else x_in_ref
    i = pl.program_id(0)
    j = pl.program_id(1)
    ctx = i < CTX_TILES

    @pl.when(j == 0)
    def _():
        if first:
            @pl.when(ctx)
            def _():
                x_ref[...] = xp_ref[...]

            @pl.when(jnp.logical_not(ctx))
            def _():
                x_ref[...] = xs_ref[...] + pos_ref[...]
        elif mix_counts:
            def pick(rs):
                return rs[0][...] if len(rs) == 1 else jnp.where(ctx, rs[0][...], rs[1][...])

            mix = (_dot(pick(mix_groups[0]), wo_ref[0:mix_split, :])
                   + _dot(pick(mix_groups[1]), wo_ref[mix_split:, :]))
            x_ref[...] = x_in_ref[...] + g2_ref[...] * mix

        h_scr[...] = _norm_mod(x_ref[...], nw_ref[...], sc_ref[...], sh_ref[...])
        acc_scr[...] = jnp.zeros_like(acc_scr)

    h = h_scr[...]
    g = jnp.dot(h, wg_ref[...].astype(BF16), preferred_element_type=F32)
    u = jnp.dot(h, wu_ref[...].astype(BF16), preferred_element_type=F32)
    acc_scr[...] += _dot(_silu(g) * u, wd_ref[...])

    @pl.when(j == pl.num_programs(1) - 1)
    def _():
        y = x_ref[...] + 0.5 * gt_ref[...] * acc_scr[...]
        if final:
            y = _rms(y, fn_ref[...])

            @pl.when(ctx)
            def _():
                oc_ref[...] = y

            @pl.when(jnp.logical_not(ctx))
            def _():
                ol_ref[...] = y
        else:
            o_ref[...] = y


def _ffn(x, mods, layer, sub, norm_w, w_gate, w_up, w_down, mix=None, final_norm=None):
    first = isinstance(x, tuple)
    final = final_norm is not None
    row = lambda width: pl.BlockSpec((TM, width), lambda i, j: (i, 0))
    if first:
        in_specs = [_ctx_rows(TM, D_MODEL), _lat_rows(TM, D_MODEL),
                    pl.BlockSpec((TM, D_MODEL), lambda i, j: (0, 0))]
        args = list(x)
    else:
        in_specs, args = [row(D_MODEL)], [x]
    mix_counts, mix_split = (), 0
    if mix is not None:
        ya, yb, w_out, w_index = mix
        counts = []
        for y in (ya, yb):
            parts = y if isinstance(y, tuple) else (y,)
            width = parts[0].shape[1]
            in_specs += [row(width)] if len(parts) == 1 else [_ctx_rows(TM, width), _lat_rows(TM, width)]
            args += list(parts)
            counts.append(len(parts))
        mix_counts = tuple(counts)
        mix_split = (ya[0] if isinstance(ya, tuple) else ya).shape[1]
        in_specs += [_mod_spec(layer, 5),
                     pl.BlockSpec((None, D_MODEL, D_MODEL), lambda i, j: (w_index, 0, 0),
                                  pipeline_mode=pl.Buffered(1))]
        args += [mods, w_out]
    in_specs += [_layer_vec(layer),
                 _mod_spec(layer, 3 * sub), _mod_spec(layer, 3 * sub + 1), _mod_spec(layer, 3 * sub + 2),
                 pl.BlockSpec((None, D_MODEL, TF), lambda i, j: (layer, 0, j)),
                 pl.BlockSpec((None, D_MODEL, TF), lambda i, j: (layer, 0, j)),
                 pl.BlockSpec((None, TF, D_MODEL), lambda i, j: (layer, j, 0))]
    args += [norm_w.reshape(DEPTH, 1, D_MODEL), mods, mods, mods, w_gate, w_up, w_down]
    scratch = [pltpu.VMEM((TM, D_MODEL), BF16), pltpu.VMEM((TM, D_MODEL), F32)]
    if first or mix_counts:
        scratch.append(pltpu.VMEM((TM, D_MODEL), F32))
    if final:
        in_specs.append(pl.BlockSpec((1, D_MODEL), lambda i, j: (0, 0)))
        args.append(final_norm[None])
        out_specs = [_ctx_rows(TM, D_MODEL), _lat_rows(TM, D_MODEL)]
        out_shape = [jax.ShapeDtypeStruct((CTX_TOK, D_MODEL), F32), jax.ShapeDtypeStruct((LAT_TOK, D_MODEL), F32)]
    else:
        out_specs = row(D_MODEL)
        out_shape = jax.ShapeDtypeStruct((N_TOK, D_MODEL), F32)
    return pl.pallas_call(
        functools.partial(_ffn_kernel, first=first, mix_counts=mix_counts, mix_split=mix_split, final=final),
        grid=(N_ROW_TILES, D_FF // TF),
        in_specs=in_specs,
        out_specs=out_specs,
        out_shape=out_shape,
        scratch_shapes=scratch,
        compiler_params=_params("arbitrary", "arbitrary"),
        name=f"ffn{sub}_{layer}",
    )(*args)


def _mixin_kernel(x_ref, nw_ref, sh_ref, sc_ref, w_ref, *o_refs, segments, transposed):
    h = _norm_mod(x_ref[...], nw_ref[...], sc_ref[...], sh_ref[...])
    for (lo, hi), o_ref in zip(segments, o_refs):
        if transposed:
            o_ref[...] = _dot_nt(h, w_ref[lo:hi, :])
        else:
            o_ref[...] = _dot(h, w_ref[:, lo:hi])
    if transposed:
        lo, hi = segments[-1]
        o_refs[-1][...] = _dot_nt(w_ref[lo:hi, :], h)


def _mixin(x, mods, layer, norm_w, w_in, w_index, segments, tm, transposed=False):
    shape = w_in.shape[1:]
    out_specs = [pl.BlockSpec((tm, hi - lo), lambda i: (i, 0)) for lo, hi in segments]
    out_shape = [jax.ShapeDtypeStruct((N_TOK, hi - lo), F32) for lo, hi in segments]
    if transposed:
        lo, hi = segments[-1]
        out_specs.append(pl.BlockSpec((hi - lo, tm), lambda i: (0, i)))
        out_shape.append(jax.ShapeDtypeStruct((hi - lo, N_TOK), F32))
    return pl.pallas_call(
        functools.partial(_mixin_kernel, segments=segments, transposed=transposed),
        grid=(N_TOK // tm,),
        in_specs=[pl.BlockSpec((tm, D_MODEL), lambda i: (i, 0)),
                  _layer_vec(layer), _mod_spec(layer, 3, tm), _mod_spec(layer, 4, tm),
                  pl.BlockSpec((None,) + shape, lambda i: (w_index, 0, 0), pipeline_mode=pl.Buffered(1))],
        out_specs=out_specs,
        out_shape=out_shape,
        compiler_params=_params("arbitrary"),
        name=f"mixin_{layer}",
    )(x, norm_w.reshape(DEPTH, 1, D_MODEL), mods, mods, w_in)


POOL_PAD = 8


def _pool_kernel(p_ref, w_ref, scale_ref, o_ref, *, seq):
    p = p_ref[...]
    zeros = jnp.zeros((POOL_PAD, A_W), F32)
    xp = jnp.concatenate([zeros, p, zeros], axis=0)
    n_pad = seq + 2 * POOL_PAD

    def back(a, s):
        return pltpu.roll(a, s, axis=0)

    def fwd(a, s):
        return pltpu.roll(a, n_pad - s, axis=0)

    w2 = xp + back(xp, 1)
    w4 = back(w2, 1) + fwd(w2, 1)
    w8 = back(w4, 2) + fwd(w4, 2)
    w16 = back(w8, 4) + fwd(w8, 4)
    sl = slice(POOL_PAD, POOL_PAD + seq)
    group = lax.broadcasted_iota(jnp.int32, (seq, A_W), 1) >> (POOL_CH.bit_length() - 1)
    t = lax.broadcasted_iota(jnp.int32, (seq, A_W), 0)
    wsum = jnp.where(group == 0, w2[sl], jnp.where(group == 1, w4[sl], jnp.where(group == 2, w8[sl], w16[sl])))
    half = jnp.left_shift(1, group)
    cnt = (jnp.minimum(t + half, seq) - jnp.maximum(t - half, 0)).astype(F32)
    d = wsum / cnt - p
    o_ref[...] = (_dot(d, w_ref[...]) * scale_ref[...]).astype(o_ref.dtype)


def _pool(p_pool, w_bd, scale, seq, n_seq, row_block0):
    return pl.pallas_call(
        functools.partial(_pool_kernel, seq=seq),
        grid=(n_seq,),
        in_specs=[pl.BlockSpec((seq, A_W), lambda b: (row_block0 + b, 0)),
                  pl.BlockSpec((A_W, A_W), lambda b: (0, 0)),
                  pl.BlockSpec((1, A_W), lambda b: (0, 0))],
        out_specs=pl.BlockSpec((seq, A_W), lambda b: (b, 0)),
        out_shape=jax.ShapeDtypeStruct((n_seq * seq, A_W), BF16),
        compiler_params=_params("arbitrary"),
        name=f"pool_{seq}",
    )(p_pool, w_bd, scale)


def _link_mask(r, c, s, upper):
    lg = s.bit_length() - 1
    same = (r >> (lg + 1)) == (c >> (lg + 1))
    r_half = (r >> lg) & 1
    c_half = (c >> lg) & 1
    return same & ((r_half == 0) & (c_half == 1) if upper else (r_half == 1) & (c_half == 0))


def _dn_kernel(q_ref, k_ref, v_ref, z_ref, g_ref, gt_ref, cq_ref, ck_ref, cv_ref, alog_ref, dtb_ref,
               alog_c_ref, dtb_c_ref, nw_ref, *rest, seq, heads, zero_init, state_slot):
    rest = list(rest)
    s0_ref = None if zero_init else rest.pop(0)
    if state_slot:
        rest.pop(0)
    o_ref = rest.pop(0)
    st_ref = rest.pop(0) if state_slot is not None else None
    gct_scr = rest.pop(0)
    h0 = pl.program_id(1) * heads
    n = DN_CHUNK
    n_blk = seq // n
    row = lax.broadcasted_iota(jnp.int32, (seq, LANES), 0)
    g_row = lax.broadcasted_iota(jnp.int32, (seq, GATE_W), 0)
    g_lane = lax.broadcasted_iota(jnp.int32, (seq, GATE_W), 1)
    r2 = lax.broadcasted_iota(jnp.int32, (n, n), 0)
    c2 = lax.broadcasted_iota(jnp.int32, (n, n), 1)
    eye = (r2 == c2).astype(F32)
    incl = (r2 >= c2, r2 <= c2)
    strict = (r2 > c2, r2 < c2)
    levels = [1 << b for b in range(n.bit_length() - 1)]
    link = {(s, d): _link_mask(r2, c2, s, d == 1) for s in levels for d in (0, 1)}
    units = [(hh, blk, d) for hh in range(heads) for blk in range(n_blk) for d in (0, 1)]

    def conv_silu(x, cw):
        prev = jnp.where(row >= 1, pltpu.roll(x, 1, axis=0), 0.0)
        nxt = jnp.where(row <= seq - 2, pltpu.roll(x, seq - 1, axis=0), 0.0)
        return _silu(prev * cw[0:1] + x * cw[1:2] + nxt * cw[2:3])

    def l2n(x):
        return x * lax.rsqrt(jnp.sum(x * x, axis=-1, keepdims=True) + EPS)

    def col(a, idx):
        return jnp.sum(jnp.where(g_lane == idx, a, 0.0), axis=1, keepdims=True)

    def log_decay(a, alog, dtb):
        xg = a + dtb
        return -jnp.exp(alog) * (jnp.maximum(xg, 0.0) + jnp.log1p(jnp.exp(-jnp.abs(xg))))

    gates = g_ref[...]
    beta_all = jax.nn.sigmoid(gates)
    g_all = log_decay(gates, alog_ref[...], dtb_ref[...])
    r_in = g_row & (n - 1)
    pre, suf = g_all, g_all
    s = 1
    while s < n:
        pre = pre + jnp.where(r_in >= s, pltpu.roll(pre, s, axis=0), 0.0)
        suf = suf + jnp.where(r_in < n - s, pltpu.roll(suf, seq - s, axis=0), 0.0)
        s *= 2

    g_all_t = log_decay(gt_ref[...], alog_c_ref[...], dtb_c_ref[...])
    ones_pre = incl[1].astype(BF16)
    ones_suf = incl[0].astype(BF16)
    backward_rows = lax.broadcasted_iota(jnp.int32, (GATE_W, n), 0) >= 3 * DN_HEADS
    for blk in range(n_blk):
        cols = slice(blk * n, (blk + 1) * n)
        rem = g_all_t[:, cols]
        acc_pre = jnp.zeros((GATE_W, n), F32)
        acc_suf = jnp.zeros((GATE_W, n), F32)
        for _ in range(3):
            piece = rem.astype(BF16)
            rem = rem - piece.astype(F32)
            acc_pre = acc_pre + jnp.dot(piece, ones_pre, preferred_element_type=F32)
            acc_suf = acc_suf + jnp.dot(piece, ones_suf, preferred_element_type=F32)
        sums = jnp.where(backward_rows, acc_suf, acc_pre)
        for r in range(2 * DN_HEADS):
            gct_scr[r, :, cols] = sums[2 * DN_HEADS + r:2 * DN_HEADS + r + 1, :]

    qs, kn, vv, beta, gc = [], [], [], [], []
    for hh in range(heads):
        cs = slice(hh * LANES, (hh + 1) * LANES)
        qs.append(l2n(conv_silu(q_ref[:, cs], cq_ref[:, cs])) * (DN_DK ** -0.5))
        kn.append(l2n(conv_silu(k_ref[:, cs], ck_ref[:, cs])))
        vv.append(conv_silu(v_ref[:, cs], cv_ref[:, cs]))
        beta.append((col(beta_all, h0 + hh), col(beta_all, DN_HEADS + h0 + hh)))
        gc.append((col(pre, 2 * DN_HEADS + h0 + hh), col(suf, 3 * DN_HEADS + h0 + hh)))

    kk, qk = {}, {}
    for hh in range(heads):
        for blk in range(n_blk):
            rs = slice(blk * n, (blk + 1) * n)
            kk[hh, blk] = _dot_nt(kn[hh][rs], kn[hh][rs])
            qk[hh, blk] = _dot_nt(qs[hh][rs], kn[hh][rs])

    m, m_hi, t, a_in = {}, {}, {}, {}
    for u in units:
        hh, blk, d = u
        rs = slice(blk * n, (blk + 1) * n)
        g_lanes = gct_scr[d * DN_HEADS + h0 + hh, :, rs]
        decay = jnp.where(incl[d], jnp.exp(gc[hh][d][rs] - g_lanes), 0.0)
        m[u] = jnp.where(strict[d], beta[hh][d][rs] * kk[hh, blk] * decay, 0.0)
        m_hi[u] = m[u].astype(BF16)
        a_in[u] = (qk[hh, blk] * decay).astype(BF16)
        t[u] = eye - jnp.where(link[1, d], m[u], 0.0)

    for s in levels[1:]:
        tb, x = {}, {}
        for u in units:
            tb[u] = t[u].astype(BF16)
            x[u] = jnp.where(link[s, u[2]], jnp.dot(m_hi[u], tb[u], preferred_element_type=F32), 0.0)
        for u in units:
            t[u] = t[u] - jnp.dot(tb[u], x[u].astype(BF16), preferred_element_type=F32)

    tb, resid = {}, {}
    for u in units:
        tb[u] = t[u].astype(BF16)
        m_lo = (m[u] - m_hi[u].astype(F32)).astype(BF16)
        mt = jnp.dot(jnp.concatenate([m_hi[u], m_lo], axis=0), tb[u], preferred_element_type=F32)
        resid[u] = eye - t[u] - (mt[0:n] + mt[n:])
    for u in units:
        t[u] = t[u] + jnp.dot(tb[u], resid[u].astype(BF16), preferred_element_type=F32)

    uw, q_dec, k_dec, g_last = {}, {}, {}, {}
    for u in units:
        hh, blk, d = u
        rs = slice(blk * n, (blk + 1) * n)
        gcol = gc[hh][d][rs]
        bcol = beta[hh][d][rs]
        e_g = jnp.exp(gcol)
        kb = kn[hh][rs]
        uw[u] = _dot(t[u], jnp.concatenate([vv[hh][rs] * bcol, kb * (bcol * e_g)], axis=1))
        g_last[u] = gcol[n - 1:n] if d == 0 else gcol[0:1]
        q_dec[u] = qs[hh][rs] * e_g
        k_dec[u] = kb * jnp.exp(g_last[u] - gcol)

    state = {}
    for hh in range(heads):
        for d in (0, 1):
            state[hh, d] = jnp.zeros((DN_DK, DN_DK), F32) if zero_init else s0_ref[d, hh]
    outs = {}
    for step in range(n_blk):
        chains = [(hh, step if d == 0 else n_blk - 1 - step, d) for hh in range(heads) for d in (0, 1)]
        ws_qs = {}
        for u in chains:
            ws_qs[u] = _dot(jnp.concatenate([uw[u][:, LANES:], q_dec[u]], axis=0), state[u[0], u[2]])
        for u in chains:
            v_new = (uw[u][:, :LANES] - ws_qs[u][0:n]).astype(BF16)
            outs[u] = ws_qs[u][n:] + jnp.dot(a_in[u], v_new, preferred_element_type=F32)
            state[u[0], u[2]] = state[u[0], u[2]] * jnp.exp(g_last[u]) + _dot_tn(k_dec[u], v_new)

    for hh in range(heads):
        cs = slice(hh * LANES, (hh + 1) * LANES)
        o = (jnp.concatenate([outs[hh, blk, 0] for blk in range(n_blk)], axis=0)
             + jnp.concatenate([outs[hh, blk, 1] for blk in range(n_blk)], axis=0))
        o = o * lax.rsqrt(jnp.mean(o * o, axis=-1, keepdims=True) + EPS) * nw_ref[...]
        o_ref[:, cs] = (o * _silu(z_ref[:, cs])).astype(o_ref.dtype)
        if state_slot == 0:
            st_ref[0, 0, hh] = state[hh, 0]
            st_ref[0, 1, hh] = state[hh, 1]
            st_ref[1:, :, hh] = jnp.zeros((N_EVEN - 1, 2, DN_DK, DN_DK), F32)
        elif state_slot:
            st_ref[0, hh] = state[hh, 0]
            st_ref[1, hh] = state[hh, 1]


def _deltanet(qkvz, gates, gates_t, conv_w, e, alog, dtb, norm_w, s0, seq, heads, n_seq, row_block0,
              state_slot=None, states=None):
    zero_init = s0 is None
    width = heads * LANES
    per = DN_HEADS // heads
    alog_row = jnp.concatenate([jnp.zeros((2 * DN_HEADS,), F32), alog.reshape(-1)])[None]
    dtb_row = jnp.concatenate([jnp.zeros((2 * DN_HEADS,), F32), dtb.reshape(-1)])[None]

    def pcol(cb):
        return pl.BlockSpec((seq, width), lambda b, h: (row_block0 + b, cb * per + h))

    def ccol(cb):
        return pl.BlockSpec((None, 3, width), lambda b, h: (e, 0, cb * per + h))

    grow = pl.BlockSpec((1, GATE_W), lambda b, h: (0, 0))
    gcol = pl.BlockSpec((GATE_W, 1), lambda b, h: (0, 0))
    in_specs = [pcol(COL_Q), pcol(COL_K), pcol(COL_V), pcol(COL_Z),
                pl.BlockSpec((seq, GATE_W), lambda b, h: (row_block0 + b, 0)),
                pl.BlockSpec((GATE_W, seq), lambda b, h: (0, row_block0 + b)),
                ccol(0), ccol(1), ccol(2), grow, grow, gcol, gcol,
                pl.BlockSpec((None, 1, LANES), lambda b, h: (e, 0, 0))]
    args = [qkvz, qkvz, qkvz, qkvz, gates, gates_t, conv_w, conv_w, conv_w, alog_row, dtb_row,
            alog_row.T, dtb_row.T, norm_w.reshape(N_EVEN, 1, LANES)]
    if not zero_init:
        in_specs.append(pl.BlockSpec((None, None, 2, heads, DN_DK, DN_DK), lambda b, h: (b, e, 0, h, 0, 0)))
        args.append(s0)
    aliases = {}
    if state_slot:
        aliases = {len(args): 1}
        in_specs.append(pl.BlockSpec(memory_space=pl.ANY))
        args.append(states)
    out_specs = [pl.BlockSpec((seq, width), lambda b, h: (b, h))]
    out_shape = [jax.ShapeDtypeStruct((n_seq * seq, B_W), BF16)]
    if state_slot is not None:
        if state_slot == 0:
            out_specs.append(pl.BlockSpec((None, N_EVEN, 2, heads, DN_DK, DN_DK), lambda b, h: (b, 0, 0, h, 0, 0)))
        else:
            out_specs.append(pl.BlockSpec((None, None, 2, heads, DN_DK, DN_DK),
                                          lambda b, h: (b, state_slot, 0, h, 0, 0)))
        out_shape.append(jax.ShapeDtypeStruct((n_seq, N_EVEN, 2, DN_HEADS, DN_DK, DN_DK), F32))
    res = pl.pallas_call(
        functools.partial(_dn_kernel, seq=seq, heads=heads, zero_init=zero_init, state_slot=state_slot),
        grid=(n_seq, per),
        in_specs=in_specs,
        out_specs=out_specs,
        out_shape=out_shape,
        scratch_shapes=[pltpu.VMEM((2 * DN_HEADS, 1, seq), F32)],
        input_output_aliases=aliases,
        compiler_params=_params("arbitrary", "arbitrary"),
        name=f"deltanet_{seq}",
    )(*args)
    return res if state_slot is not None else (res[0], None)


TS = 512


def _sgu_kernel(u_ref, v_ref, nw_ref, ws_ref, bt_ref, o_ref):
    u = jax.nn.gelu(u_ref[...])
    v = jax.nn.gelu(v_ref[...])
    mu = jnp.mean(v, axis=-1, keepdims=True)
    vc = v - mu
    var = jnp.mean(vc * vc, axis=-1, keepdims=True)
    vn = vc * lax.rsqrt(var + EPS) * nw_ref[...]
    bt = bt_ref[...]
    for c in range(TS // SGU_CHUNK):
        rs = slice(c * SGU_CHUNK, (c + 1) * SGU_CHUNK)
        for hd in range(SGU_HEADS):
            cs = slice(hd * LANES, (hd + 1) * LANES)
            mixed = _dot(ws_ref[hd], vn[rs, cs]) + bt[:, hd:hd + 1]
            o_ref[rs, cs] = (u[rs, cs] * mixed).astype(o_ref.dtype)


def _sgu(uv, norm_w, w_s, b_t):
    return pl.pallas_call(
        _sgu_kernel,
        grid=(N_TOK // TS,),
        in_specs=[pl.BlockSpec((TS, C_W), lambda i: (i, 0)),
                  pl.BlockSpec((TS, C_W), lambda i: (i, 1)),
                  pl.BlockSpec((1, C_W), lambda i: (0, 0)),
                  pl.BlockSpec((SGU_HEADS, SGU_CHUNK, SGU_CHUNK), lambda i: (0, 0, 0)),
                  pl.BlockSpec((SGU_CHUNK, SGU_HEADS), lambda i: (0, 0))],
        out_specs=pl.BlockSpec((TS, C_W), lambda i: (i, 0)),
        out_shape=jax.ShapeDtypeStruct((N_TOK, C_W), BF16),
        compiler_params=_params("arbitrary"),
        name="sgu",
    )(uv, uv, norm_w, w_s, b_t)


def _fnet_kernel(f_ref, cn_ref, sn_ref, cc_ref, sc_ref, w_ref, o_ref):
    f = _split(f_ref[...])
    fc = _dot3(f, _split(cc_ref[...]))
    fs = _dot3(f, _split(sc_ref[...]))
    spec = _dot3(_split(cn_ref[...]), _split(fc)) - _dot3(_split(sn_ref[...]), _split(fs))
    o_ref[...] = _dot(spec, w_ref[...]).astype(o_ref.dtype)


def _fnet(f_all, cn, sn, cc_bd, sc_bd, w_bd, seq, n_seq, row_block0):
    const = lambda shape: pl.BlockSpec(shape, lambda b: (0, 0))
    return pl.pallas_call(
        _fnet_kernel,
        grid=(n_seq,),
        in_specs=[pl.BlockSpec((seq, F_W), lambda b: (row_block0 + b, 0)),
                  const((seq, seq)), const((seq, seq)), const((F_W, F_W)), const((F_W, F_W)),
                  const((F_W, F_W))],
        out_specs=pl.BlockSpec((seq, F_W), lambda b: (b, 0)),
        out_shape=jax.ShapeDtypeStruct((n_seq * seq, F_W), BF16),
        compiler_params=_params("arbitrary"),
        name=f"fnet_{seq}",
    )(f_all, cn, sn, cc_bd, sc_bd, w_bd)


def _grid_pos_embed(n_tok):
    rows = n_tok // GRID_W
    r = np.repeat(np.arange(rows, dtype=np.float64), GRID_W)
    col = np.tile(np.arange(GRID_W, dtype=np.float64), rows)
    quarter = D_MODEL // 4
    freq = np.exp(-math.log(10000.0) * np.arange(quarter, dtype=np.float64) / quarter)

    def emb(p):
        ang = p[:, None] * freq[None, :]
        return np.concatenate([np.sin(ang), np.cos(ang)], axis=-1)

    return np.concatenate([emb(r), emb(col)], axis=-1).astype(np.float32)


def _dft_tables(n):
    k = np.arange(n, dtype=np.int64)
    ang = ((k[:, None] * k[None, :]) % n).astype(np.float64) * (2.0 * math.pi / n)
    scale = n ** -0.5
    return (np.cos(ang) * scale).astype(np.float32), (np.sin(ang) * scale).astype(np.float32)


def _block_diag(blocks):
    g, a, b = blocks.shape
    eye = jnp.eye(g, dtype=blocks.dtype)
    return (eye[:, None, :, None] * blocks[:, :, None, :]).reshape(g * a, g * b)


def kernel(x_prompt, x_sample, state_delta, c, c_ctx, ffn1_norm, ffn1_w_gate, ffn1_w_up, ffn1_w_down,
           mix_norm, ffn2_norm, ffn2_w_gate, ffn2_w_up, ffn2_w_down, ada_w, ada_b, ev_w_in, ev_w_out,
           pool_w, pool_scale, dn_conv_w, dn_a_log, dn_dt_bias, dn_norm_w, od_w_in, od_w_out, sgu_norm,
           sgu_w, sgu_b, fnet_w, final_norm):
    cond8 = jnp.zeros((SUBLANES, D_MODEL), F32).at[0].set(c_ctx).at[1:1 + DEC_BATCH].set(c)
    mods = _ada(cond8, ada_w, ada_b)
    mods = mods[:, :1 + DEC_BATCH].reshape(DEPTH, 1 + DEC_BATCH, N_MOD, 1, D_MODEL).transpose(0, 2, 1, 3, 4)

    dft = {n: _dft_tables(n) for n in (SEQ, DEC_SEQ)}
    cc, sc = _dft_tables(FN_CH)
    cc_bd = np.kron(np.eye(F_W // FN_CH, dtype=np.float32), cc)
    sc_bd = np.kron(np.eye(F_W // FN_CH, dtype=np.float32), sc)
    groups = ((SEQ, BATCH, 0, DN_HEADS), (DEC_SEQ, DEC_BATCH, CTX_TOK // DEC_SEQ, 1))
    even_segments = ((A_W, A_W + 4 * QK_W), (0, A_W), (A_W + 4 * QK_W, P_EVEN))
    odd_segments = ((0, 2 * C_W), (2 * C_W, P_ODD))

    x = (x_prompt.reshape(CTX_TOK, D_MODEL), x_sample.reshape(LAT_TOK, D_MODEL), _grid_pos_embed(DEC_SEQ))
    ev_w_in_t = jnp.swapaxes(ev_w_in, 1, 2)
    states = None
    for layer in range(DEPTH):
        x = _ffn(x, mods, layer, 0, ffn1_norm, ffn1_w_gate, ffn1_w_up, ffn1_w_down)
        if layer % 2 == 0:
            e = layer // 2
            qkvz, p_pool, gates, gates_t = _mixin(x, mods, layer, mix_norm, ev_w_in_t, e, even_segments, 512,
                                                  transposed=True)
            w_bd = _block_diag(pool_w[e])
            ya, yb = [], []
            for seq, n_seq, rb0, dn_heads in groups:
                ctx = rb0 == 0
                ya.append(_pool(p_pool, w_bd, pool_scale[e][None], seq, n_seq, rb0))
                o, st = _deltanet(qkvz, gates, gates_t, dn_conv_w, e, dn_a_log[e], dn_dt_bias[e], dn_norm_w,
                                  None if ctx else state_delta, seq, dn_heads, n_seq, rb0,
                                  state_slot=e if ctx else None, states=states)
                yb.append(o)
                if ctx:
                    states = st
            mix = (tuple(ya), tuple(yb), ev_w_out, e)
        else:
            j = layer // 2
            uv, f_all = _mixin(x, mods, layer, mix_norm, od_w_in, j, odd_segments, TM)
            yc = _sgu(uv, sgu_norm[j][None], sgu_w[j], sgu_b[j].T)
            w_bd = _block_diag(fnet_w[j])
            yd = tuple(_fnet(f_all, dft[seq][0], dft[seq][1], cc_bd, sc_bd, w_bd, seq, n_seq, rb0)
                       for seq, n_seq, rb0, _ in groups)
            mix = (yc, yd, od_w_out, j)
        x = _ffn(x, mods, layer, 2, ffn2_norm, ffn2_w_gate, ffn2_w_up, ffn2_w_down, mix=mix,
                 final_norm=final_norm if layer == DEPTH - 1 else None)

    y_prompt, y_sample = x
    return (y_prompt.reshape(BATCH, SEQ, D_MODEL), y_sample.reshape(DEC_BATCH, DEC_SEQ, D_MODEL), states)
```

```python
import functools
import math

import numpy as np
import jax
import jax.numpy as jnp
from jax import lax
from jax.experimental import pallas as pl
from jax.experimental.pallas import tpu as pltpu

F32 = jnp.float32
BF16 = jnp.bfloat16

D_MODEL = 1024
BATCH = 16
SEQ = 256
DEPTH = 4
DEC_BATCH = 2
DEC_SEQ = 1024
GRID_W = 64
EPS = 1e-6
D_FF = 2816
N_MOD = 9
N_EVEN = (DEPTH + 1) // 2
POOL_SIZES = (2, 4, 8, 16)
POOL_CH = 64
A_W = 256
DN_HEADS = 6
DN_DK = 128
QK_W = DN_HEADS * DN_DK
B_W = QK_W
GATE_W = 4 * DN_HEADS
P_EVEN = A_W + 4 * QK_W + GATE_W
SGU_CHUNK = 128
SGU_HEADS = 6
C_W = 768
FN_CH = 64
F_W = 256
P_ODD = 2 * C_W + F_W

CTX_TOK = BATCH * SEQ
LAT_TOK = DEC_BATCH * DEC_SEQ
N_TOK = CTX_TOK + LAT_TOK

LANES = 128
SUBLANES = 8
VMEM_LIMIT = 56 * 1024 * 1024

TM = 1024
N_ROW_TILES = N_TOK // TM
CTX_TILES = CTX_TOK // TM
TF = 256
DN_CHUNK = 128
COL_Q, COL_K, COL_V, COL_Z = 0, 1, 2, 3


def _cond_of_tile(i, tm):
    return jnp.where(i < CTX_TOK // tm, 0, 1 + (i - CTX_TOK // tm) * tm // DEC_SEQ)


def _silu(x):
    return x * jax.nn.sigmoid(x)


def _dot(a, b):
    return jnp.dot(a.astype(BF16), b.astype(BF16), preferred_element_type=F32)


def _dot_nt(a, b):
    return lax.dot_general(a.astype(BF16), b.astype(BF16), (((1,), (1,)), ((), ())),
                           preferred_element_type=F32)


def _dot_tn(a, b):
    return lax.dot_general(a.astype(BF16), b.astype(BF16), (((0,), (0,)), ((), ())),
                           preferred_element_type=F32)


def _split(a):
    hi = a.astype(BF16)
    lo = (a - hi.astype(F32)).astype(BF16)
    return hi, lo


def _dot3(a, b):
    ah, al = a
    bh, bl = b
    return (jnp.dot(ah, bh, preferred_element_type=F32)
            + (jnp.dot(al, bh, preferred_element_type=F32)
               + jnp.dot(ah, bl, preferred_element_type=F32)))


def _rms(x, w):
    return x * lax.rsqrt(jnp.mean(x * x, axis=-1, keepdims=True) + EPS) * w


def _norm_mod(x, nw, scale, shift):
    r = lax.rsqrt(jnp.mean(x * x, axis=-1, keepdims=True) + EPS)
    return ((x * r) * (nw * (1.0 + scale)) + shift).astype(BF16)


def _params(*sem):
    return pltpu.CompilerParams(dimension_semantics=sem, vmem_limit_bytes=VMEM_LIMIT)


def _ctx_rows(tm, width):
    return pl.BlockSpec((tm, width), lambda i, *_: (jnp.minimum(i, CTX_TOK // tm - 1), 0))


def _lat_rows(tm, width):
    return pl.BlockSpec((tm, width), lambda i, *_: (jnp.maximum(i - CTX_TOK // tm, 0), 0))


def _ada_kernel(c_ref, w_ref, b_ref, o_ref):
    o_ref[0] = _dot(_silu(c_ref[...]), w_ref[0]) + b_ref[0]


def _ada(cond8, ada_w, ada_b):
    tn = 1152
    n = N_MOD * D_MODEL
    return pl.pallas_call(
        _ada_kernel,
        grid=(DEPTH, n // tn),
        in_specs=[pl.BlockSpec((SUBLANES, D_MODEL), lambda l, j: (0, 0)),
                  pl.BlockSpec((1, D_MODEL, tn), lambda l, j: (l, 0, j)),
                  pl.BlockSpec((1, 1, tn), lambda l, j: (l, 0, j))],
        out_specs=pl.BlockSpec((1, SUBLANES, tn), lambda l, j: (l, 0, j)),
        out_shape=jax.ShapeDtypeStruct((DEPTH, SUBLANES, n), F32),
        compiler_params=_params("arbitrary", "arbitrary"),
        name="ada",
    )(cond8, ada_w, ada_b.reshape(DEPTH, 1, n))


def _mod_spec(layer, which, tm=TM):
    return pl.BlockSpec((None, None, None, 1, D_MODEL),
                        lambda i, *_: (layer, which, _cond_of_tile(i, tm), 0, 0))


def _layer_vec(layer):
    return pl.BlockSpec((None, 1, D_MODEL), lambda i, *_: (layer, 0, 0))


def _ffn_kernel(*refs, first, mix_counts, mix_split, final, pair):
    refs = list(refs)
    take = lambda k: [refs.pop(0) for _ in range(k)]
    if first:
        xp_ref, xs_ref, pos_ref = take(3)
    else:
        (x_in_ref,) = take(1)
    if mix_counts:
        mix_groups = [take(c) for c in mix_counts]
        g2_ref, wo_ref = take(2)
    nw_ref, sh_ref, sc_ref, gt_ref = take(4)
    w_blocks = [take(3) for _ in range(2 if pair else 1)]
    if final:
        fn_ref, oc_ref, ol_ref = take(3)
    else:
        (o_ref,) = take(1)
    h_scr, acc_scr = take(2)
    x_ref = refs.pop(0) if (first or mix_counts) else x_in_ref
    i = pl.program_id(0)
    j = pl.program_id(1)
    ctx = i < CTX_TILES

    @pl.when(j == 0)
    def _():
        if first:
            @pl.when(ctx)
            def _():
                x_ref[...] = xp_ref[...]

            @pl.when(jnp.logical_not(ctx))
            def _():
                x_ref[...] = xs_ref[...] + pos_ref[...]
        elif mix_counts:
            def pick(rs):
                return rs[0][...] if len(rs) == 1 else jnp.where(ctx, rs[0][...], rs[1][...])

            mix = (_dot(pick(mix_groups[0]), wo_ref[0:mix_split, :])
                   + _dot(pick(mix_groups[1]), wo_ref[mix_split:, :]))
            x_ref[...] = x_in_ref[...] + g2_ref[...] * mix

        h_scr[...] = _norm_mod(x_ref[...], nw_ref[...], sc_ref[...], sh_ref[...])
        acc_scr[...] = jnp.zeros_like(acc_scr)

    def accumulate(blocks):
        def cat(k, axis):
            parts = [b[k][...].astype(BF16) for b in blocks]
            return parts[0] if len(parts) == 1 else jnp.concatenate(parts, axis=axis)

        h = h_scr[...]
        g = jnp.dot(h, cat(0, 1), preferred_element_type=F32)
        u = jnp.dot(h, cat(1, 1), preferred_element_type=F32)
        acc_scr[...] += jnp.dot((_silu(g) * u).astype(BF16), cat(2, 0), preferred_element_type=F32)

    last = pl.num_programs(1) - 1
    if pair:
        @pl.when(j < last)
        def _():
            accumulate(w_blocks)

        @pl.when(j == last)
        def _():
            accumulate(w_blocks[:1])
    else:
        accumulate(w_blocks)

    @pl.when(j == last)
    def _():
        y = x_ref[...] + 0.5 * gt_ref[...] * acc_scr[...]
        if final:
            y = _rms(y, fn_ref[...])

            @pl.when(ctx)
            def _():
                oc_ref[...] = y

            @pl.when(jnp.logical_not(ctx))
            def _():
                ol_ref[...] = y
        else:
            o_ref[...] = y


def _ffn(x, mods, layer, sub, norm_w, w_gate, w_up, w_down, mix=None, final_norm=None):
    first = isinstance(x, tuple)
    final = final_norm is not None
    row = lambda width: pl.BlockSpec((TM, width), lambda i, j: (i, 0))
    if first:
        in_specs = [_ctx_rows(TM, D_MODEL), _lat_rows(TM, D_MODEL),
                    pl.BlockSpec((TM, D_MODEL), lambda i, j: (0, 0))]
        args = list(x)
    else:
        in_specs, args = [row(D_MODEL)], [x]
    mix_counts, mix_split = (), 0
    if mix is not None:
        ya, yb, w_out, w_index = mix
        counts = []
        for y in (ya, yb):
            parts = y if isinstance(y, tuple) else (y,)
            width = parts[0].shape[1]
            in_specs += [row(width)] if len(parts) == 1 else [_ctx_rows(TM, width), _lat_rows(TM, width)]
            args += list(parts)
            counts.append(len(parts))
        mix_counts = tuple(counts)
        mix_split = (ya[0] if isinstance(ya, tuple) else ya).shape[1]
        in_specs += [_mod_spec(layer, 5),
                     pl.BlockSpec((None, D_MODEL, D_MODEL), lambda i, j: (w_index, 0, 0),
                                  pipeline_mode=pl.Buffered(1))]
        args += [mods, w_out]
    in_specs += [_layer_vec(layer),
                 _mod_spec(layer, 3 * sub), _mod_spec(layer, 3 * sub + 1), _mod_spec(layer, 3 * sub + 2)]
    args += [norm_w.reshape(DEPTH, 1, D_MODEL), mods, mods, mods]
    pair = not final
    n_blocks = D_FF // TF
    per_step = 2 if pair else 1
    for k in range(per_step):
        blk = lambda j, k=k: jnp.minimum(per_step * j + k, n_blocks - 1)
        in_specs += [pl.BlockSpec((None, D_MODEL, TF), lambda i, j, blk=blk: (layer, 0, blk(j))),
                     pl.BlockSpec((None, D_MODEL, TF), lambda i, j, blk=blk: (layer, 0, blk(j))),
                     pl.BlockSpec((None, TF, D_MODEL), lambda i, j, blk=blk: (layer, blk(j), 0))]
        args += [w_gate, w_up, w_down]
    scratch = [pltpu.VMEM((TM, D_MODEL), BF16), pltpu.VMEM((TM, D_MODEL), F32)]
    if first or mix_counts:
        scratch.append(pltpu.VMEM((TM, D_MODEL), F32))
    if final:
        in_specs.append(pl.BlockSpec((1, D_MODEL), lambda i, j: (0, 0)))
        args.append(final_norm[None])
        out_specs = [_ctx_rows(TM, D_MODEL), _lat_rows(TM, D_MODEL)]
        out_shape = [jax.ShapeDtypeStruct((CTX_TOK, D_MODEL), F32), jax.ShapeDtypeStruct((LAT_TOK, D_MODEL), F32)]
    else:
        out_specs = row(D_MODEL)
        out_shape = jax.ShapeDtypeStruct((N_TOK, D_MODEL), F32)
    return pl.pallas_call(
        functools.partial(_ffn_kernel, first=first, mix_counts=mix_counts, mix_split=mix_split, final=final,
                          pair=pair),
        grid=(N_ROW_TILES, pl.cdiv(n_blocks, per_step)),
        in_specs=in_specs,
        out_specs=out_specs,
        out_shape=out_shape,
        scratch_shapes=scratch,
        compiler_params=_params("arbitrary", "arbitrary"),
        name=f"ffn{sub}_{layer}",
    )(*args)


def _mixin_kernel(x_ref, nw_ref, sh_ref, sc_ref, w_ref, *rest, segments, transposed):
    o_refs, wb_scr = rest[:-1], rest[-1]

    @pl.when(pl.program_id(0) == 0)
    def _():
        wb_scr[...] = w_ref[...].astype(BF16)

    h = _norm_mod(x_ref[...], nw_ref[...], sc_ref[...], sh_ref[...])
    for (lo, hi), o_ref in zip(segments, o_refs):
        if transposed:
            o_ref[...] = _dot_nt(h, wb_scr[lo:hi, :])
        else:
            o_ref[...] = _dot(h, wb_scr[:, lo:hi])
    if transposed:
        lo, hi = segments[-1]
        o_refs[-1][...] = _dot_nt(wb_scr[lo:hi, :], h)


def _mixin(x, mods, layer, norm_w, w_in, w_index, segments, tm, transposed=False):
    shape = w_in.shape[1:]
    out_specs = [pl.BlockSpec((tm, hi - lo), lambda i: (i, 0)) for lo, hi in segments]
    out_shape = [jax.ShapeDtypeStruct((N_TOK, hi - lo), F32) for lo, hi in segments]
    if transposed:
        lo, hi = segments[-1]
        out_specs.append(pl.BlockSpec((hi - lo, tm), lambda i: (0, i)))
        out_shape.append(jax.ShapeDtypeStruct((hi - lo, N_TOK), F32))
    return pl.pallas_call(
        functools.partial(_mixin_kernel, segments=segments, transposed=transposed),
        grid=(N_TOK // tm,),
        in_specs=[pl.BlockSpec((tm, D_MODEL), lambda i: (i, 0)),
                  _layer_vec(layer), _mod_spec(layer, 3, tm), _mod_spec(layer, 4, tm),
                  pl.BlockSpec((None,) + shape, lambda i: (w_index, 0, 0), pipeline_mode=pl.Buffered(1))],
        out_specs=out_specs,
        out_shape=out_shape,
        scratch_shapes=[pltpu.VMEM(shape, BF16)],
        compiler_params=_params("arbitrary"),
        name=f"mixin_{layer}",
    )(x, norm_w.reshape(DEPTH, 1, D_MODEL), mods, mods, w_in)


def _pool_kernel(p_ref, w_ref, scale_ref, o_ref):
    seq = jnp.where(pl.program_id(0) < CTX_TILES, SEQ, DEC_SEQ)
    p = p_ref[...]
    pos = lax.broadcasted_iota(jnp.int32, (TM, A_W), 0) & (seq - 1)

    def before(a, s):
        return jnp.where(pos >= s, pltpu.roll(a, s, axis=0), 0.0)

    def after(a, s):
        return jnp.where(pos < seq - s, pltpu.roll(a, TM - s, axis=0), 0.0)

    sums = []
    f, b = p, before(p, 1)
    for h in (size // 2 for size in POOL_SIZES):
        if h > 1:
            f = f + after(f, h // 2)
            b = b + before(b, h // 2)
        sums.append(f + b)
    group = lax.broadcasted_iota(jnp.int32, (TM, A_W), 1) >> (POOL_CH.bit_length() - 1)
    wsum = jnp.where(group == 0, sums[0], jnp.where(group == 1, sums[1], jnp.where(group == 2, sums[2], sums[3])))
    half = jnp.left_shift(1, group)
    cnt = (jnp.minimum(pos + half, seq) - jnp.maximum(pos - half, 0)).astype(F32)
    d = wsum / cnt - p
    o_ref[...] = (_dot(d, w_ref[...]) * scale_ref[...]).astype(o_ref.dtype)


def _pool(p_pool, w_bd, scale):
    return pl.pallas_call(
        _pool_kernel,
        grid=(N_ROW_TILES,),
        in_specs=[pl.BlockSpec((TM, A_W), lambda i: (i, 0)),
                  pl.BlockSpec((A_W, A_W), lambda i: (0, 0)),
                  pl.BlockSpec((1, A_W), lambda i: (0, 0))],
        out_specs=pl.BlockSpec((TM, A_W), lambda i: (i, 0)),
        out_shape=jax.ShapeDtypeStruct((N_TOK, A_W), BF16),
        compiler_params=_params("arbitrary"),
        name="pool",
    )(p_pool, w_bd, scale)


def _link_mask(r, c, s, upper):
    lg = s.bit_length() - 1
    same = (r >> (lg + 1)) == (c >> (lg + 1))
    r_half = (r >> lg) & 1
    c_half = (c >> lg) & 1
    return same & ((r_half == 0) & (c_half == 1) if upper else (r_half == 1) & (c_half == 0))


def _dn_kernel(q_ref, k_ref, v_ref, z_ref, g_ref, gt_ref, cq_ref, ck_ref, cv_ref, alog_ref, dtb_ref,
               alog_c_ref, dtb_c_ref, nw_ref, *rest, seq, heads, zero_init, state_slot):
    rest = list(rest)
    s0_ref = None if zero_init else rest.pop(0)
    if state_slot:
        rest.pop(0)
    o_ref = rest.pop(0)
    st_ref = rest.pop(0) if state_slot is not None else None
    gct_scr = rest.pop(0)
    h0 = pl.program_id(1) * heads
    n = DN_CHUNK
    n_blk = seq // n
    row = lax.broadcasted_iota(jnp.int32, (seq, LANES), 0)
    g_row = lax.broadcasted_iota(jnp.int32, (seq, GATE_W), 0)
    g_lane = lax.broadcasted_iota(jnp.int32, (seq, GATE_W), 1)
    r2 = lax.broadcasted_iota(jnp.int32, (n, n), 0)
    c2 = lax.broadcasted_iota(jnp.int32, (n, n), 1)
    eye = (r2 == c2).astype(F32)
    incl = (r2 >= c2, r2 <= c2)
    strict = (r2 > c2, r2 < c2)
    levels = [1 << b for b in range(n.bit_length() - 1)]
    link = {(s, d): _link_mask(r2, c2, s, d == 1) for s in levels for d in (0, 1)}
    units = [(hh, blk, d) for hh in range(heads) for blk in range(n_blk) for d in (0, 1)]

    def conv_silu(x, cw):
        prev = jnp.where(row >= 1, pltpu.roll(x, 1, axis=0), 0.0)
        nxt = jnp.where(row <= seq - 2, pltpu.roll(x, seq - 1, axis=0), 0.0)
        return _silu(prev * cw[0:1] + x * cw[1:2] + nxt * cw[2:3])

    def l2n(x):
        return x * lax.rsqrt(jnp.sum(x * x, axis=-1, keepdims=True) + EPS)

    def col(a, idx):
        return jnp.sum(jnp.where(g_lane == idx, a, 0.0), axis=1, keepdims=True)

    def log_decay(a, alog, dtb):
        xg = a + dtb
        return -jnp.exp(alog) * (jnp.maximum(xg, 0.0) + jnp.log1p(jnp.exp(-jnp.abs(xg))))

    gates = g_ref[...]
    beta_all = jax.nn.sigmoid(gates)
    g_all = log_decay(gates, alog_ref[...], dtb_ref[...])
    r_in = g_row & (n - 1)
    pre, suf = g_all, g_all
    s = 1
    while s < n:
        pre = pre + jnp.where(r_in >= s, pltpu.roll(pre, s, axis=0), 0.0)
        suf = suf + jnp.where(r_in < n - s, pltpu.roll(suf, seq - s, axis=0), 0.0)
        s *= 2

    g_all_t = log_decay(gt_ref[...], alog_c_ref[...], dtb_c_ref[...])
    ones_pre = incl[1].astype(BF16)
    ones_suf = incl[0].astype(BF16)
    backward_rows = lax.broadcasted_iota(jnp.int32, (GATE_W, n), 0) >= 3 * DN_HEADS
    for blk in range(n_blk):
        cols = slice(blk * n, (blk + 1) * n)
        rem = g_all_t[:, cols]
        acc_pre = jnp.zeros((GATE_W, n), F32)
        acc_suf = jnp.zeros((GATE_W, n), F32)
        for _ in range(3):
            piece = rem.astype(BF16)
            rem = rem - piece.astype(F32)
            acc_pre = acc_pre + jnp.dot(piece, ones_pre, preferred_element_type=F32)
            acc_suf = acc_suf + jnp.dot(piece, ones_suf, preferred_element_type=F32)
        sums = jnp.where(backward_rows, acc_suf, acc_pre)
        for r in range(2 * DN_HEADS):
            gct_scr[r, :, cols] = sums[2 * DN_HEADS + r:2 * DN_HEADS + r + 1, :]

    qs, kn, vv, beta, gc = [], [], [], [], []
    for hh in range(heads):
        cs = slice(hh * LANES, (hh + 1) * LANES)
        qs.append(l2n(conv_silu(q_ref[:, cs], cq_ref[:, cs])) * (DN_DK ** -0.5))
        kn.append(l2n(conv_silu(k_ref[:, cs], ck_ref[:, cs])))
        vv.append(conv_silu(v_ref[:, cs], cv_ref[:, cs]))
        beta.append((col(beta_all, h0 + hh), col(beta_all, DN_HEADS + h0 + hh)))
        gc.append((col(pre, 2 * DN_HEADS + h0 + hh), col(suf, 3 * DN_HEADS + h0 + hh)))

    kk, qk = {}, {}
    for hh in range(heads):
        for blk in range(n_blk):
            rs = slice(blk * n, (blk + 1) * n)
            kk[hh, blk] = _dot_nt(kn[hh][rs], kn[hh][rs])
            qk[hh, blk] = _dot_nt(qs[hh][rs], kn[hh][rs])

    m, m_hi, t, a_in = {}, {}, {}, {}
    for u in units:
        hh, blk, d = u
        rs = slice(blk * n, (blk + 1) * n)
        g_lanes = gct_scr[d * DN_HEADS + h0 + hh, :, rs]
        decay = jnp.where(incl[d], jnp.exp(gc[hh][d][rs] - g_lanes), 0.0)
        m[u] = jnp.where(strict[d], beta[hh][d][rs] * kk[hh, blk] * decay, 0.0)
        m_hi[u] = m[u].astype(BF16)
        a_in[u] = (qk[hh, blk] * decay).astype(BF16)
        t[u] = eye - jnp.where(link[1, d], m[u], 0.0)

    for s in levels[1:]:
        tb, x = {}, {}
        for u in units:
            tb[u] = t[u].astype(BF16)
            x[u] = jnp.where(link[s, u[2]], jnp.dot(m_hi[u], tb[u], preferred_element_type=F32), 0.0)
        for u in units:
            t[u] = t[u] - jnp.dot(tb[u], x[u].astype(BF16), preferred_element_type=F32)

    tb, resid = {}, {}
    for u in units:
        tb[u] = t[u].astype(BF16)
        m_lo = (m[u] - m_hi[u].astype(F32)).astype(BF16)
        mt = jnp.dot(jnp.concatenate([m_hi[u], m_lo], axis=0), tb[u], preferred_element_type=F32)
        resid[u] = eye - t[u] - (mt[0:n] + mt[n:])
    for u in units:
        t[u] = t[u] + jnp.dot(tb[u], resid[u].astype(BF16), preferred_element_type=F32)

    uw, q_dec, k_dec, g_last = {}, {}, {}, {}
    for u in units:
        hh, blk, d = u
        rs = slice(blk * n, (blk + 1) * n)
        gcol = gc[hh][d][rs]
        bcol = beta[hh][d][rs]
        e_g = jnp.exp(gcol)
        kb = kn[hh][rs]
        uw[u] = _dot(t[u], jnp.concatenate([vv[hh][rs] * bcol, kb * (bcol * e_g)], axis=1))
        g_last[u] = gcol[n - 1:n] if d == 0 else gcol[0:1]
        q_dec[u] = qs[hh][rs] * e_g
        k_dec[u] = kb * jnp.exp(g_last[u] - gcol)

    state = {}
    for hh in range(heads):
        for d in (0, 1):
            state[hh, d] = jnp.zeros((DN_DK, DN_DK), F32) if zero_init else s0_ref[d, hh]
    outs = {}
    for step in range(n_blk):
        chains = [(hh, step if d == 0 else n_blk - 1 - step, d) for hh in range(heads) for d in (0, 1)]
        ws_qs = {}
        for u in chains:
            ws_qs[u] = _dot(jnp.concatenate([uw[u][:, LANES:], q_dec[u]], axis=0), state[u[0], u[2]])
        for u in chains:
            v_new = (uw[u][:, :LANES] - ws_qs[u][0:n]).astype(BF16)
            outs[u] = ws_qs[u][n:] + jnp.dot(a_in[u], v_new, preferred_element_type=F32)
            state[u[0], u[2]] = state[u[0], u[2]] * jnp.exp(g_last[u]) + _dot_tn(k_dec[u], v_new)

    for hh in range(heads):
        cs = slice(hh * LANES, (hh + 1) * LANES)
        o = (jnp.concatenate([outs[hh, blk, 0] for blk in range(n_blk)], axis=0)
             + jnp.concatenate([outs[hh, blk, 1] for blk in range(n_blk)], axis=0))
        o = o * lax.rsqrt(jnp.mean(o * o, axis=-1, keepdims=True) + EPS) * nw_ref[...]
        o_ref[:, cs] = (o * _silu(z_ref[:, cs])).astype(o_ref.dtype)
        if state_slot == 0:
            st_ref[0, 0, hh] = state[hh, 0]
            st_ref[0, 1, hh] = state[hh, 1]
            st_ref[1:, :, hh] = jnp.zeros((N_EVEN - 1, 2, DN_DK, DN_DK), F32)
        elif state_slot:
            st_ref[0, hh] = state[hh, 0]
            st_ref[1, hh] = state[hh, 1]


def _deltanet(qkvz, gates, gates_t, conv_w, e, alog, dtb, norm_w, s0, seq, heads, n_seq, row_block0,
              state_slot=None, states=None):
    zero_init = s0 is None
    width = heads * LANES
    per = DN_HEADS // heads
    alog_row = jnp.concatenate([jnp.zeros((2 * DN_HEADS,), F32), alog.reshape(-1)])[None]
    dtb_row = jnp.concatenate([jnp.zeros((2 * DN_HEADS,), F32), dtb.reshape(-1)])[None]

    def pcol(cb):
        return pl.BlockSpec((seq, width), lambda b, h: (row_block0 + b, cb * per + h))

    def ccol(cb):
        return pl.BlockSpec((None, 3, width), lambda b, h: (e, 0, cb * per + h))

    grow = pl.BlockSpec((1, GATE_W), lambda b, h: (0, 0))
    gcol = pl.BlockSpec((GATE_W, 1), lambda b, h: (0, 0))
    in_specs = [pcol(COL_Q), pcol(COL_K), pcol(COL_V), pcol(COL_Z),
                pl.BlockSpec((seq, GATE_W), lambda b, h: (row_block0 + b, 0)),
                pl.BlockSpec((GATE_W, seq), lambda b, h: (0, row_block0 + b)),
                ccol(0), ccol(1), ccol(2), grow, grow, gcol, gcol,
                pl.BlockSpec((None, 1, LANES), lambda b, h: (e, 0, 0))]
    args = [qkvz, qkvz, qkvz, qkvz, gates, gates_t, conv_w, conv_w, conv_w, alog_row, dtb_row,
            alog_row.T, dtb_row.T, norm_w.reshape(N_EVEN, 1, LANES)]
    if not zero_init:
        in_specs.append(pl.BlockSpec((None, None, 2, heads, DN_DK, DN_DK), lambda b, h: (b, e, 0, h, 0, 0)))
        args.append(s0)
    aliases = {}
    if state_slot:
        aliases = {len(args): 1}
        in_specs.append(pl.BlockSpec(memory_space=pl.ANY))
        args.append(states)
    out_specs = [pl.BlockSpec((seq, width), lambda b, h: (b, h))]
    out_shape = [jax.ShapeDtypeStruct((n_seq * seq, B_W), BF16)]
    if state_slot is not None:
        if state_slot == 0:
            out_specs.append(pl.BlockSpec((None, N_EVEN, 2, heads, DN_DK, DN_DK), lambda b, h: (b, 0, 0, h, 0, 0)))
        else:
            out_specs.append(pl.BlockSpec((None, None, 2, heads, DN_DK, DN_DK),
                                          lambda b, h: (b, state_slot, 0, h, 0, 0)))
        out_shape.append(jax.ShapeDtypeStruct((n_seq, N_EVEN, 2, DN_HEADS, DN_DK, DN_DK), F32))
    res = pl.pallas_call(
        functools.partial(_dn_kernel, seq=seq, heads=heads, zero_init=zero_init, state_slot=state_slot),
        grid=(n_seq, per),
        in_specs=in_specs,
        out_specs=out_specs,
        out_shape=out_shape,
        scratch_shapes=[pltpu.VMEM((2 * DN_HEADS, 1, seq), F32)],
        input_output_aliases=aliases,
        compiler_params=_params("arbitrary", "arbitrary"),
        name=f"deltanet_{seq}",
    )(*args)
    return res if state_slot is not None else (res[0], None)


TS = 512


def _sgu_kernel(u_ref, v_ref, nw_ref, ws_ref, bt_ref, o_ref):
    u = jax.nn.gelu(u_ref[...])
    v = jax.nn.gelu(v_ref[...])
    mu = jnp.mean(v, axis=-1, keepdims=True)
    vc = v - mu
    var = jnp.mean(vc * vc, axis=-1, keepdims=True)
    vn = vc * lax.rsqrt(var + EPS) * nw_ref[...]
    bt = bt_ref[...]
    for c in range(TS // SGU_CHUNK):
        rs = slice(c * SGU_CHUNK, (c + 1) * SGU_CHUNK)
        for hd in range(SGU_HEADS):
            cs = slice(hd * LANES, (hd + 1) * LANES)
            mixed = _dot(ws_ref[hd], vn[rs, cs]) + bt[:, hd:hd + 1]
            o_ref[rs, cs] = (u[rs, cs] * mixed).astype(o_ref.dtype)


def _sgu(uv, norm_w, w_s, b_t):
    return pl.pallas_call(
        _sgu_kernel,
        grid=(N_TOK // TS,),
        in_specs=[pl.BlockSpec((TS, C_W), lambda i: (i, 0)),
                  pl.BlockSpec((TS, C_W), lambda i: (i, 1)),
                  pl.BlockSpec((1, C_W), lambda i: (0, 0)),
                  pl.BlockSpec((SGU_HEADS, SGU_CHUNK, SGU_CHUNK), lambda i: (0, 0, 0)),
                  pl.BlockSpec((SGU_CHUNK, SGU_HEADS), lambda i: (0, 0))],
        out_specs=pl.BlockSpec((TS, C_W), lambda i: (i, 0)),
        out_shape=jax.ShapeDtypeStruct((N_TOK, C_W), BF16),
        compiler_params=_params("arbitrary"),
        name="sgu",
    )(uv, uv, norm_w, w_s, b_t)


def _fnet_kernel(f_ref, ch_hi_ref, ch_lo_ref, ctx_hi_ref, ctx_lo_ref, lat_hi_ref, lat_lo_ref, w_ref, o_ref):
    fcs = _dot3(_split(f_ref[...]), (ch_hi_ref[...], ch_lo_ref[...]))
    wb = w_ref[...].astype(BF16)
    ctx = pl.program_id(0) < CTX_TILES

    def finish(rows, hi_ref, lo_ref):
        stacked = jnp.concatenate([fcs[rows, :F_W], fcs[rows, F_W:]], axis=0)
        spec = _dot3((hi_ref[...], lo_ref[...]), _split(stacked))
        o_ref[rows, :] = jnp.dot(spec.astype(BF16), wb, preferred_element_type=F32).astype(o_ref.dtype)

    @pl.when(ctx)
    def _():
        for s in range(TM // SEQ):
            finish(slice(s * SEQ, (s + 1) * SEQ), ctx_hi_ref, ctx_lo_ref)

    @pl.when(jnp.logical_not(ctx))
    def _():
        finish(slice(0, DEC_SEQ), lat_hi_ref, lat_lo_ref)


def _fnet(f_all, tables, w_bd):
    const = lambda a: pl.BlockSpec(a.shape, lambda i: (0, 0))
    return pl.pallas_call(
        _fnet_kernel,
        grid=(N_ROW_TILES,),
        in_specs=[pl.BlockSpec((TM, F_W), lambda i: (i, 0))] + [const(t) for t in tables] + [const(w_bd)],
        out_specs=pl.BlockSpec((TM, F_W), lambda i: (i, 0)),
        out_shape=jax.ShapeDtypeStruct((N_TOK, F_W), BF16),
        compiler_params=_params("arbitrary"),
        name="fnet",
    )(f_all, *tables, w_bd)


def _grid_pos_embed(n_tok):
    rows = n_tok // GRID_W
    r = np.repeat(np.arange(rows, dtype=np.float64), GRID_W)
    col = np.tile(np.arange(GRID_W, dtype=np.float64), rows)
    quarter = D_MODEL // 4
    freq = np.exp(-math.log(10000.0) * np.arange(quarter, dtype=np.float64) / quarter)

    def emb(p):
        ang = p[:, None] * freq[None, :]
        return np.concatenate([np.sin(ang), np.cos(ang)], axis=-1)

    return np.concatenate([emb(r), emb(col)], axis=-1).astype(np.float32)


def _dft_tables(n):
    k = np.arange(n, dtype=np.int64)
    ang = ((k[:, None] * k[None, :]) % n).astype(np.float64) * (2.0 * math.pi / n)
    scale = n ** -0.5
    return np.cos(ang) * scale, np.sin(ang) * scale


def _host_split(a):
    hi = a.astype(np.float32).astype(BF16)
    lo = (a - hi.astype(np.float64)).astype(np.float32).astype(BF16)
    return hi, lo


def _fnet_tables():
    cc, sc = _dft_tables(FN_CH)
    eye = np.eye(F_W // FN_CH)
    tables = list(_host_split(np.concatenate([np.kron(eye, cc), np.kron(eye, sc)], axis=1)))
    for n in (SEQ, DEC_SEQ):
        cn, sn = _dft_tables(n)
        tables += list(_host_split(np.concatenate([cn, -sn], axis=1)))
    return tables


def _block_diag(blocks):
    g, a, b = blocks.shape
    eye = jnp.eye(g, dtype=blocks.dtype)
    return (eye[:, None, :, None] * blocks[:, :, None, :]).reshape(g * a, g * b)


def kernel(x_prompt, x_sample, state_delta, c, c_ctx, ffn1_norm, ffn1_w_gate, ffn1_w_up, ffn1_w_down,
           mix_norm, ffn2_norm, ffn2_w_gate, ffn2_w_up, ffn2_w_down, ada_w, ada_b, ev_w_in, ev_w_out,
           pool_w, pool_scale, dn_conv_w, dn_a_log, dn_dt_bias, dn_norm_w, od_w_in, od_w_out, sgu_norm,
           sgu_w, sgu_b, fnet_w, final_norm):
    cond8 = jnp.zeros((SUBLANES, D_MODEL), F32).at[0].set(c_ctx).at[1:1 + DEC_BATCH].set(c)
    mods = _ada(cond8, ada_w, ada_b)
    mods = mods[:, :1 + DEC_BATCH].reshape(DEPTH, 1 + DEC_BATCH, N_MOD, 1, D_MODEL).transpose(0, 2, 1, 3, 4)

    fnet_tables = _fnet_tables()
    groups = ((SEQ, BATCH, 0, DN_HEADS), (DEC_SEQ, DEC_BATCH, CTX_TOK // DEC_SEQ, 1))
    even_segments = ((A_W, A_W + 4 * QK_W), (0, A_W), (A_W + 4 * QK_W, P_EVEN))
    odd_segments = ((0, 2 * C_W), (2 * C_W, P_ODD))

    x = (x_prompt.reshape(CTX_TOK, D_MODEL), x_sample.reshape(LAT_TOK, D_MODEL), _grid_pos_embed(DEC_SEQ))
    ev_w_in_t = jnp.swapaxes(ev_w_in, 1, 2)
    states = None
    for layer in range(DEPTH):
        x = _ffn(x, mods, layer, 0, ffn1_norm, ffn1_w_gate, ffn1_w_up, ffn1_w_down)
        if layer % 2 == 0:
            e = layer // 2
            qkvz, p_pool, gates, gates_t = _mixin(x, mods, layer, mix_norm, ev_w_in_t, e, even_segments, 512,
                                                  transposed=True)
            w_bd = _block_diag(pool_w[e])
            ya = _pool(p_pool, w_bd, pool_scale[e][None])
            yb = []
            for seq, n_seq, rb0, dn_heads in groups:
                ctx = rb0 == 0
                o, st = _deltanet(qkvz, gates, gates_t, dn_conv_w, e, dn_a_log[e], dn_dt_bias[e], dn_norm_w,
                                  None if ctx else state_delta, seq, dn_heads, n_seq, rb0,
                                  state_slot=e if ctx else None, states=states)
                yb.append(o)
                if ctx:
                    states = st
            mix = (ya, tuple(yb), ev_w_out, e)
        else:
            j = layer // 2
            uv, f_all = _mixin(x, mods, layer, mix_norm, od_w_in, j, odd_segments, TM)
            yc = _sgu(uv, sgu_norm[j][None], sgu_w[j], sgu_b[j].T)
            w_bd = _block_diag(fnet_w[j])
            yd = _fnet(f_all, fnet_tables, w_bd)
            mix = (yc, yd, od_w_out, j)
        x = _ffn(x, mods, layer, 2, ffn2_norm, ffn2_w_gate, ffn2_w_up, ffn2_w_down, mix=mix,
                 final_norm=final_norm if layer == DEPTH - 1 else None)

    y_prompt, y_sample = x
    return (y_prompt.reshape(BATCH, SEQ, D_MODEL), y_sample.reshape(DEC_BATCH, DEC_SEQ, D_MODEL), states)
```

```python
import functools
import math

import numpy as np
import jax
import jax.numpy as jnp
from jax import lax
from jax.experimental import pallas as pl
from jax.experimental.pallas import tpu as pltpu

F32 = jnp.float32
BF16 = jnp.bfloat16

D_MODEL = 1024
BATCH = 16
SEQ = 256
DEPTH = 4
DEC_BATCH = 2
DEC_SEQ = 1024
GRID_W = 64
EPS = 1e-6
D_FF = 2816
N_MOD = 9
N_EVEN = (DEPTH + 1) // 2
POOL_SIZES = (2, 4, 8, 16)
POOL_CH = 64
A_W = 256
DN_HEADS = 6
DN_DK = 128
QK_W = DN_HEADS * DN_DK
B_W = QK_W
GATE_W = 4 * DN_HEADS
P_EVEN = A_W + 4 * QK_W + GATE_W
SGU_CHUNK = 128
SGU_HEADS = 6
C_W = 768
FN_CH = 64
F_W = 256
P_ODD = 2 * C_W + F_W

CTX_TOK = BATCH * SEQ
LAT_TOK = DEC_BATCH * DEC_SEQ
N_TOK = CTX_TOK + LAT_TOK

LANES = 128
SUBLANES = 8
VMEM_LIMIT = 56 * 1024 * 1024

TM = 1024
N_ROW_TILES = N_TOK // TM
CTX_TILES = CTX_TOK // TM
TF = 256
DN_CHUNK = 128
COL_Q, COL_K, COL_V, COL_Z = 0, 1, 2, 3


def _cond_of_tile(i, tm):
    return jnp.where(i < CTX_TOK // tm, 0, 1 + (i - CTX_TOK // tm) * tm // DEC_SEQ)


def _silu(x):
    return x * jax.nn.sigmoid(x)


def _dot(a, b):
    return jnp.dot(a.astype(BF16), b.astype(BF16), preferred_element_type=F32)


def _dot_nt(a, b):
    return lax.dot_general(a.astype(BF16), b.astype(BF16), (((1,), (1,)), ((), ())),
                           preferred_element_type=F32)


def _dot_tn(a, b):
    return lax.dot_general(a.astype(BF16), b.astype(BF16), (((0,), (0,)), ((), ())),
                           preferred_element_type=F32)


def _split(a):
    hi = a.astype(BF16)
    lo = (a - hi.astype(F32)).astype(BF16)
    return hi, lo


def _dot3(a, b):
    ah, al = a
    bh, bl = b
    return (jnp.dot(ah, bh, preferred_element_type=F32)
            + (jnp.dot(al, bh, preferred_element_type=F32)
               + jnp.dot(ah, bl, preferred_element_type=F32)))


def _rms(x, w):
    return x * lax.rsqrt(jnp.mean(x * x, axis=-1, keepdims=True) + EPS) * w


def _norm_mod(x, nw, scale, shift):
    r = lax.rsqrt(jnp.mean(x * x, axis=-1, keepdims=True) + EPS)
    return ((x * r) * (nw * (1.0 + scale)) + shift).astype(BF16)


def _params(*sem):
    return pltpu.CompilerParams(dimension_semantics=sem, vmem_limit_bytes=VMEM_LIMIT)


def _ctx_rows(tm, width):
    return pl.BlockSpec((tm, width), lambda i, *_: (jnp.minimum(i, CTX_TOK // tm - 1), 0))


def _lat_rows(tm, width):
    return pl.BlockSpec((tm, width), lambda i, *_: (jnp.maximum(i - CTX_TOK // tm, 0), 0))


def _ada_kernel(c_ref, w_ref, b_ref, o_ref):
    o_ref[0] = _dot(_silu(c_ref[...]), w_ref[0]) + b_ref[0]


def _ada(cond8, ada_w, ada_b):
    tn = 1152
    n = N_MOD * D_MODEL
    return pl.pallas_call(
        _ada_kernel,
        grid=(DEPTH, n // tn),
        in_specs=[pl.BlockSpec((SUBLANES, D_MODEL), lambda l, j: (0, 0)),
                  pl.BlockSpec((1, D_MODEL, tn), lambda l, j: (l, 0, j)),
                  pl.BlockSpec((1, 1, tn), lambda l, j: (l, 0, j))],
        out_specs=pl.BlockSpec((1, SUBLANES, tn), lambda l, j: (l, 0, j)),
        out_shape=jax.ShapeDtypeStruct((DEPTH, SUBLANES, n), F32),
        compiler_params=_params("arbitrary", "arbitrary"),
        name="ada",
    )(cond8, ada_w, ada_b.reshape(DEPTH, 1, n))


def _mod_spec(layer, which, tm=TM, n_seg=1, seg=0):
    return pl.BlockSpec((None, None, None, 1, D_MODEL),
                        lambda i, *_: (layer, which, _cond_of_tile(i * n_seg + seg, tm), 0, 0))


def _layer_vec(layer):
    return pl.BlockSpec((None, 1, D_MODEL), lambda i, *_: (layer, 0, 0))


def _ffn_kernel(*refs, first, mix_counts, mix_split, final, n_seg):
    refs = list(refs)
    take = lambda k: [refs.pop(0) for _ in range(k)]
    if first:
        xp_ref, xs_ref, pos_ref = take(3)
    else:
        (x_in_ref,) = take(1)
    if mix_counts:
        mix_groups = [take(c) for c in mix_counts]
        g2_ref, wo_ref = take(2)
    (nw_ref,) = take(1)
    seg_mods = [take(3) for _ in range(n_seg)]
    wg_ref, wu_ref, wd_ref = take(3)
    if final:
        fn_ref, oc_ref, ol_ref = take(3)
    else:
        (o_ref,) = take(1)
    h_scr, acc_scr = take(2)
    x_ref = refs.pop(0) if (first or mix_counts) else x_in_ref
    i = pl.program_id(0)
    j = pl.program_id(1)
    ctx = i < CTX_TILES

    @pl.when(j == 0)
    def _():
        if first:
            @pl.when(ctx)
            def _():
                x_ref[...] = xp_ref[...]

            @pl.when(jnp.logical_not(ctx))
            def _():
                x_ref[...] = xs_ref[...] + pos_ref[...]
        elif mix_counts:
            def pick(rs):
                return rs[0][...] if len(rs) == 1 else jnp.where(ctx, rs[0][...], rs[1][...])

            mix = (_dot(pick(mix_groups[0]), wo_ref[0:mix_split, :])
                   + _dot(pick(mix_groups[1]), wo_ref[mix_split:, :]))
            x_ref[...] = x_in_ref[...] + g2_ref[...] * mix

        for s, (sh_ref, sc_ref, _) in enumerate(seg_mods):
            rows = slice(s * TM, (s + 1) * TM)
            h_scr[rows, :] = _norm_mod(x_ref[rows, :], nw_ref[...], sc_ref[...], sh_ref[...])
        acc_scr[...] = jnp.zeros_like(acc_scr)

    h = h_scr[...]
    g = jnp.dot(h, wg_ref[...].astype(BF16), preferred_element_type=F32)
    u = jnp.dot(h, wu_ref[...].astype(BF16), preferred_element_type=F32)
    acc_scr[...] += _dot(_silu(g) * u, wd_ref[...])

    @pl.when(j == pl.num_programs(1) - 1)
    def _():
        if final:
            y = _rms(x_ref[...] + 0.5 * seg_mods[0][2][...] * acc_scr[...], fn_ref[...])

            @pl.when(ctx)
            def _():
                oc_ref[...] = y

            @pl.when(jnp.logical_not(ctx))
            def _():
                ol_ref[...] = y
        else:
            for s, (_, _, gt_ref) in enumerate(seg_mods):
                rows = slice(s * TM, (s + 1) * TM)
                o_ref[rows, :] = x_ref[rows, :] + 0.5 * gt_ref[...] * acc_scr[rows, :]


def _ffn(x, mods, layer, sub, norm_w, w_gate, w_up, w_down, mix=None, final_norm=None):
    first = isinstance(x, tuple)
    final = final_norm is not None
    n_seg = 2 if (not first and mix is None and not final) else 1
    tm = n_seg * TM
    row = lambda width: pl.BlockSpec((tm, width), lambda i, j: (i, 0))
    if first:
        in_specs = [_ctx_rows(TM, D_MODEL), _lat_rows(TM, D_MODEL),
                    pl.BlockSpec((TM, D_MODEL), lambda i, j: (0, 0))]
        args = list(x)
    else:
        in_specs, args = [row(D_MODEL)], [x]
    mix_counts, mix_split = (), 0
    if mix is not None:
        ya, yb, w_out, w_index = mix
        counts = []
        for y in (ya, yb):
            parts = y if isinstance(y, tuple) else (y,)
            width = parts[0].shape[1]
            in_specs += [row(width)] if len(parts) == 1 else [_ctx_rows(TM, width), _lat_rows(TM, width)]
            args += list(parts)
            counts.append(len(parts))
        mix_counts = tuple(counts)
        mix_split = (ya[0] if isinstance(ya, tuple) else ya).shape[1]
        in_specs += [_mod_spec(layer, 5),
                     pl.BlockSpec((None, D_MODEL, D_MODEL), lambda i, j: (w_index, 0, 0),
                                  pipeline_mode=pl.Buffered(1))]
        args += [mods, w_out]
    in_specs.append(_layer_vec(layer))
    args.append(norm_w.reshape(DEPTH, 1, D_MODEL))
    for seg in range(n_seg):
        in_specs += [_mod_spec(layer, 3 * sub + k, TM, n_seg, seg) for k in range(3)]
        args += [mods, mods, mods]
    in_specs += [pl.BlockSpec((None, D_MODEL, TF), lambda i, j: (layer, 0, j)),
                 pl.BlockSpec((None, D_MODEL, TF), lambda i, j: (layer, 0, j)),
                 pl.BlockSpec((None, TF, D_MODEL), lambda i, j: (layer, j, 0))]
    args += [w_gate, w_up, w_down]
    scratch = [pltpu.VMEM((tm, D_MODEL), BF16), pltpu.VMEM((tm, D_MODEL), F32)]
    if first or mix_counts:
        scratch.append(pltpu.VMEM((TM, D_MODEL), F32))
    if final:
        in_specs.append(pl.BlockSpec((1, D_MODEL), lambda i, j: (0, 0)))
        args.append(final_norm[None])
        out_specs = [_ctx_rows(TM, D_MODEL), _lat_rows(TM, D_MODEL)]
        out_shape = [jax.ShapeDtypeStruct((CTX_TOK, D_MODEL), F32), jax.ShapeDtypeStruct((LAT_TOK, D_MODEL), F32)]
    else:
        out_specs = row(D_MODEL)
        out_shape = jax.ShapeDtypeStruct((N_TOK, D_MODEL), F32)
    return pl.pallas_call(
        functools.partial(_ffn_kernel, first=first, mix_counts=mix_counts, mix_split=mix_split, final=final,
                          n_seg=n_seg),
        grid=(N_TOK // tm, D_FF // TF),
        in_specs=in_specs,
        out_specs=out_specs,
        out_shape=out_shape,
        scratch_shapes=scratch,
        compiler_params=_params("arbitrary", "arbitrary"),
        name=f"ffn{sub}_{layer}",
    )(*args)


def _mixin_kernel(x_ref, nw_ref, sh_ref, sc_ref, w_ref, *rest, segments, transposed):
    o_refs, wb_scr = rest[:-1], rest[-1]

    @pl.when(pl.program_id(0) == 0)
    def _():
        wb_scr[...] = w_ref[...].astype(BF16)

    h = _norm_mod(x_ref[...], nw_ref[...], sc_ref[...], sh_ref[...])
    for (lo, hi), o_ref in zip(segments, o_refs):
        if transposed:
            o_ref[...] = _dot_nt(h, wb_scr[lo:hi, :])
        else:
            o_ref[...] = _dot(h, wb_scr[:, lo:hi])
    if transposed:
        lo, hi = segments[-1]
        o_refs[-1][...] = _dot_nt(wb_scr[lo:hi, :], h)


def _mixin(x, mods, layer, norm_w, w_in, w_index, segments, tm, transposed=False):
    shape = w_in.shape[1:]
    out_specs = [pl.BlockSpec((tm, hi - lo), lambda i: (i, 0)) for lo, hi in segments]
    out_shape = [jax.ShapeDtypeStruct((N_TOK, hi - lo), F32) for lo, hi in segments]
    if transposed:
        lo, hi = segments[-1]
        out_specs.append(pl.BlockSpec((hi - lo, tm), lambda i: (0, i)))
        out_shape.append(jax.ShapeDtypeStruct((hi - lo, N_TOK), F32))
    return pl.pallas_call(
        functools.partial(_mixin_kernel, segments=segments, transposed=transposed),
        grid=(N_TOK // tm,),
        in_specs=[pl.BlockSpec((tm, D_MODEL), lambda i: (i, 0)),
                  _layer_vec(layer), _mod_spec(layer, 3, tm), _mod_spec(layer, 4, tm),
                  pl.BlockSpec((None,) + shape, lambda i: (w_index, 0, 0), pipeline_mode=pl.Buffered(1))],
        out_specs=out_specs,
        out_shape=out_shape,
        scratch_shapes=[pltpu.VMEM(shape, BF16)],
        compiler_params=_params("arbitrary"),
        name=f"mixin_{layer}",
    )(x, norm_w.reshape(DEPTH, 1, D_MODEL), mods, mods, w_in)


def _pool_kernel(p_ref, w_ref, scale_ref, o_ref):
    seq = jnp.where(pl.program_id(0) < CTX_TILES, SEQ, DEC_SEQ)
    p = p_ref[...]
    pos = lax.broadcasted_iota(jnp.int32, (TM, A_W), 0) & (seq - 1)

    def before(a, s):
        return jnp.where(pos >= s, pltpu.roll(a, s, axis=0), 0.0)

    def after(a, s):
        return jnp.where(pos < seq - s, pltpu.roll(a, TM - s, axis=0), 0.0)

    sums = []
    f, b = p, before(p, 1)
    for h in (size // 2 for size in POOL_SIZES):
        if h > 1:
            f = f + after(f, h // 2)
            b = b + before(b, h // 2)
        sums.append(f + b)
    group = lax.broadcasted_iota(jnp.int32, (TM, A_W), 1) >> (POOL_CH.bit_length() - 1)
    wsum = jnp.where(group == 0, sums[0], jnp.where(group == 1, sums[1], jnp.where(group == 2, sums[2], sums[3])))
    half = jnp.left_shift(1, group)
    cnt = (jnp.minimum(pos + half, seq) - jnp.maximum(pos - half, 0)).astype(F32)
    d = wsum / cnt - p
    o_ref[...] = (_dot(d, w_ref[...]) * scale_ref[...]).astype(o_ref.dtype)


def _pool(p_pool, w_bd, scale):
    return pl.pallas_call(
        _pool_kernel,
        grid=(N_ROW_TILES,),
        in_specs=[pl.BlockSpec((TM, A_W), lambda i: (i, 0)),
                  pl.BlockSpec((A_W, A_W), lambda i: (0, 0)),
                  pl.BlockSpec((1, A_W), lambda i: (0, 0))],
        out_specs=pl.BlockSpec((TM, A_W), lambda i: (i, 0)),
        out_shape=jax.ShapeDtypeStruct((N_TOK, A_W), BF16),
        compiler_params=_params("arbitrary"),
        name="pool",
    )(p_pool, w_bd, scale)


def _link_mask(r, c, s, upper):
    lg = s.bit_length() - 1
    same = (r >> (lg + 1)) == (c >> (lg + 1))
    r_half = (r >> lg) & 1
    c_half = (c >> lg) & 1
    return same & ((r_half == 0) & (c_half == 1) if upper else (r_half == 1) & (c_half == 0))


def _dn_kernel(q_ref, k_ref, v_ref, z_ref, g_ref, gt_ref, cq_ref, ck_ref, cv_ref, alog_ref, dtb_ref,
               alog_c_ref, dtb_c_ref, nw_ref, *rest, seq, heads, zero_init, state_slot):
    rest = list(rest)
    s0_ref = None if zero_init else rest.pop(0)
    if state_slot:
        rest.pop(0)
    o_ref = rest.pop(0)
    st_ref = rest.pop(0) if state_slot is not None else None
    gct_scr = rest.pop(0)
    h0 = pl.program_id(1) * heads
    n = DN_CHUNK
    n_blk = seq // n
    row = lax.broadcasted_iota(jnp.int32, (seq, LANES), 0)
    g_row = lax.broadcasted_iota(jnp.int32, (seq, GATE_W), 0)
    g_lane = lax.broadcasted_iota(jnp.int32, (seq, GATE_W), 1)
    r2 = lax.broadcasted_iota(jnp.int32, (n, n), 0)
    c2 = lax.broadcasted_iota(jnp.int32, (n, n), 1)
    eye = (r2 == c2).astype(F32)
    incl = (r2 >= c2, r2 <= c2)
    strict = (r2 > c2, r2 < c2)
    levels = [1 << b for b in range(n.bit_length() - 1)]
    link = {(s, d): _link_mask(r2, c2, s, d == 1) for s in levels for d in (0, 1)}
    units = [(hh, blk, d) for hh in range(heads) for blk in range(n_blk) for d in (0, 1)]

    def conv_silu(x, cw):
        prev = jnp.where(row >= 1, pltpu.roll(x, 1, axis=0), 0.0)
        nxt = jnp.where(row <= seq - 2, pltpu.roll(x, seq - 1, axis=0), 0.0)
        return _silu(prev * cw[0:1] + x * cw[1:2] + nxt * cw[2:3])

    def l2n(x):
        return x * lax.rsqrt(jnp.sum(x * x, axis=-1, keepdims=True) + EPS)

    def col(a, idx):
        return jnp.sum(jnp.where(g_lane == idx, a, 0.0), axis=1, keepdims=True)

    def log_decay(a, alog, dtb):
        xg = a + dtb
        return -jnp.exp(alog) * (jnp.maximum(xg, 0.0) + jnp.log1p(jnp.exp(-jnp.abs(xg))))

    gates = g_ref[...]
    beta_all = jax.nn.sigmoid(gates)
    g_all = log_decay(gates, alog_ref[...], dtb_ref[...])
    r_in = g_row & (n - 1)
    pre, suf = g_all, g_all
    s = 1
    while s < n:
        pre = pre + jnp.where(r_in >= s, pltpu.roll(pre, s, axis=0), 0.0)
        suf = suf + jnp.where(r_in < n - s, pltpu.roll(suf, seq - s, axis=0), 0.0)
        s *= 2

    g_all_t = log_decay(gt_ref[...], alog_c_ref[...], dtb_c_ref[...])
    ones_pre = incl[1].astype(BF16)
    ones_suf = incl[0].astype(BF16)
    backward_rows = lax.broadcasted_iota(jnp.int32, (GATE_W, n), 0) >= 3 * DN_HEADS
    for blk in range(n_blk):
        cols = slice(blk * n, (blk + 1) * n)
        rem = g_all_t[:, cols]
        acc_pre = jnp.zeros((GATE_W, n), F32)
        acc_suf = jnp.zeros((GATE_W, n), F32)
        for _ in range(3):
            piece = rem.astype(BF16)
            rem = rem - piece.astype(F32)
            acc_pre = acc_pre + jnp.dot(piece, ones_pre, preferred_element_type=F32)
            acc_suf = acc_suf + jnp.dot(piece, ones_suf, preferred_element_type=F32)
        sums = jnp.where(backward_rows, acc_suf, acc_pre)
        for r in range(2 * DN_HEADS):
            gct_scr[r, :, cols] = sums[2 * DN_HEADS + r:2 * DN_HEADS + r + 1, :]

    qs, kn, vv, beta, gc = [], [], [], [], []
    for hh in range(heads):
        cs = slice(hh * LANES, (hh + 1) * LANES)
        qs.append(l2n(conv_silu(q_ref[:, cs], cq_ref[:, cs])) * (DN_DK ** -0.5))
        kn.append(l2n(conv_silu(k_ref[:, cs], ck_ref[:, cs])))
        vv.append(conv_silu(v_ref[:, cs], cv_ref[:, cs]))
        beta.append((col(beta_all, h0 + hh), col(beta_all, DN_HEADS + h0 + hh)))
        gc.append((col(pre, 2 * DN_HEADS + h0 + hh), col(suf, 3 * DN_HEADS + h0 + hh)))

    kk, qk = {}, {}
    for hh in range(heads):
        for blk in range(n_blk):
            rs = slice(blk * n, (blk + 1) * n)
            kk[hh, blk] = _dot_nt(kn[hh][rs], kn[hh][rs])
            qk[hh, blk] = _dot_nt(qs[hh][rs], kn[hh][rs])

    m, m_hi, t, a_in = {}, {}, {}, {}
    for u in units:
        hh, blk, d = u
        rs = slice(blk * n, (blk + 1) * n)
        g_lanes = gct_scr[d * DN_HEADS + h0 + hh, :, rs]
        decay = jnp.where(incl[d], jnp.exp(gc[hh][d][rs] - g_lanes), 0.0)
        m[u] = jnp.where(strict[d], beta[hh][d][rs] * kk[hh, blk] * decay, 0.0)
        m_hi[u] = m[u].astype(BF16)
        a_in[u] = (qk[hh, blk] * decay).astype(BF16)
        t[u] = eye - jnp.where(link[1, d], m[u], 0.0)

    for s in levels[1:]:
        tb, x = {}, {}
        for u in units:
            tb[u] = t[u].astype(BF16)
            x[u] = jnp.where(link[s, u[2]], jnp.dot(m_hi[u], tb[u], preferred_element_type=F32), 0.0)
        for u in units:
            t[u] = t[u] - jnp.dot(tb[u], x[u].astype(BF16), preferred_element_type=F32)

    tb, resid = {}, {}
    for u in units:
        tb[u] = t[u].astype(BF16)
        m_lo = (m[u] - m_hi[u].astype(F32)).astype(BF16)
        mt = jnp.dot(jnp.concatenate([m_hi[u], m_lo], axis=0), tb[u], preferred_element_type=F32)
        resid[u] = eye - t[u] - (mt[0:n] + mt[n:])
    for u in units:
        t[u] = t[u] + jnp.dot(tb[u], resid[u].astype(BF16), preferred_element_type=F32)

    uw, q_dec, k_dec, g_last = {}, {}, {}, {}
    for u in units:
        hh, blk, d = u
        rs = slice(blk * n, (blk + 1) * n)
        gcol = gc[hh][d][rs]
        bcol = beta[hh][d][rs]
        e_g = jnp.exp(gcol)
        kb = kn[hh][rs]
        uw[u] = _dot(t[u], jnp.concatenate([vv[hh][rs] * bcol, kb * (bcol * e_g)], axis=1))
        g_last[u] = gcol[n - 1:n] if d == 0 else gcol[0:1]
        q_dec[u] = qs[hh][rs] * e_g
        k_dec[u] = kb * jnp.exp(g_last[u] - gcol)

    state = {}
    for hh in range(heads):
        for d in (0, 1):
            state[hh, d] = jnp.zeros((DN_DK, DN_DK), F32) if zero_init else s0_ref[d, hh]
    outs = {}
    for step in range(n_blk):
        chains = [(hh, step if d == 0 else n_blk - 1 - step, d) for hh in range(heads) for d in (0, 1)]
        ws_qs = {}
        for u in chains:
            ws_qs[u] = _dot(jnp.concatenate([uw[u][:, LANES:], q_dec[u]], axis=0), state[u[0], u[2]])
        for u in chains:
            v_new = (uw[u][:, :LANES] - ws_qs[u][0:n]).astype(BF16)
            outs[u] = ws_qs[u][n:] + jnp.dot(a_in[u], v_new, preferred_element_type=F32)
            state[u[0], u[2]] = state[u[0], u[2]] * jnp.exp(g_last[u]) + _dot_tn(k_dec[u], v_new)

    for hh in range(heads):
        cs = slice(hh * LANES, (hh + 1) * LANES)
        o = (jnp.concatenate([outs[hh, blk, 0] for blk in range(n_blk)], axis=0)
             + jnp.concatenate([outs[hh, blk, 1] for blk in range(n_blk)], axis=0))
        o = o * lax.rsqrt(jnp.mean(o * o, axis=-1, keepdims=True) + EPS) * nw_ref[...]
        o_ref[:, cs] = (o * _silu(z_ref[:, cs])).astype(o_ref.dtype)
        if state_slot == 0:
            st_ref[0, 0, hh] = state[hh, 0]
            st_ref[0, 1, hh] = state[hh, 1]
            st_ref[1:, :, hh] = jnp.zeros((N_EVEN - 1, 2, DN_DK, DN_DK), F32)
        elif state_slot:
            st_ref[0, hh] = state[hh, 0]
            st_ref[1, hh] = state[hh, 1]


def _deltanet(qkvz, gates, gates_t, conv_w, e, alog, dtb, norm_w, s0, seq, heads, n_seq, row_block0,
              state_slot=None, states=None):
    zero_init = s0 is None
    width = heads * LANES
    per = DN_HEADS // heads
    alog_row = jnp.concatenate([jnp.zeros((2 * DN_HEADS,), F32), alog.reshape(-1)])[None]
    dtb_row = jnp.concatenate([jnp.zeros((2 * DN_HEADS,), F32), dtb.reshape(-1)])[None]

    def pcol(cb):
        return pl.BlockSpec((seq, width), lambda b, h: (row_block0 + b, cb * per + h))

    def ccol(cb):
        return pl.BlockSpec((None, 3, width), lambda b, h: (e, 0, cb * per + h))

    grow = pl.BlockSpec((1, GATE_W), lambda b, h: (0, 0))
    gcol = pl.BlockSpec((GATE_W, 1), lambda b, h: (0, 0))
    in_specs = [pcol(COL_Q), pcol(COL_K), pcol(COL_V), pcol(COL_Z),
                pl.BlockSpec((seq, GATE_W), lambda b, h: (row_block0 + b, 0)),
                pl.BlockSpec((GATE_W, seq), lambda b, h: (0, row_block0 + b)),
                ccol(0), ccol(1), ccol(2), grow, grow, gcol, gcol,
                pl.BlockSpec((None, 1, LANES), lambda b, h: (e, 0, 0))]
    args = [qkvz, qkvz, qkvz, qkvz, gates, gates_t, conv_w, conv_w, conv_w, alog_row, dtb_row,
            alog_row.T, dtb_row.T, norm_w.reshape(N_EVEN, 1, LANES)]
    if not zero_init:
        in_specs.append(pl.BlockSpec((None, None, 2, heads, DN_DK, DN_DK), lambda b, h: (b, e, 0, h, 0, 0)))
        args.append(s0)
    aliases = {}
    if state_slot:
        aliases = {len(args): 1}
        in_specs.append(pl.BlockSpec(memory_space=pl.ANY))
        args.append(states)
    out_specs = [pl.BlockSpec((seq, width), lambda b, h: (b, h))]
    out_shape = [jax.ShapeDtypeStruct((n_seq * seq, B_W), BF16)]
    if state_slot is not None:
        if state_slot == 0:
            out_specs.append(pl.BlockSpec((None, N_EVEN, 2, heads, DN_DK, DN_DK), lambda b, h: (b, 0, 0, h, 0, 0)))
        else:
            out_specs.append(pl.BlockSpec((None, None, 2, heads, DN_DK, DN_DK),
                                          lambda b, h: (b, state_slot, 0, h, 0, 0)))
        out_shape.append(jax.ShapeDtypeStruct((n_seq, N_EVEN, 2, DN_HEADS, DN_DK, DN_DK), F32))
    res = pl.pallas_call(
        functools.partial(_dn_kernel, seq=seq, heads=heads, zero_init=zero_init, state_slot=state_slot),
        grid=(n_seq, per),
        in_specs=in_specs,
        out_specs=out_specs,
        out_shape=out_shape,
        scratch_shapes=[pltpu.VMEM((2 * DN_HEADS, 1, seq), F32)],
        input_output_aliases=aliases,
        compiler_params=_params("arbitrary", "arbitrary"),
        name=f"deltanet_{seq}",
    )(*args)
    return res if state_slot is not None else (res[0], None)


TS = 512


def _sgu_kernel(u_ref, v_ref, nw_ref, ws_ref, bt_ref, o_ref):
    u = jax.nn.gelu(u_ref[...])
    v = jax.nn.gelu(v_ref[...])
    mu = jnp.mean(v, axis=-1, keepdims=True)
    vc = v - mu
    var = jnp.mean(vc * vc, axis=-1, keepdims=True)
    vn = vc * lax.rsqrt(var + EPS) * nw_ref[...]
    bt = bt_ref[...]
    for c in range(TS // SGU_CHUNK):
        rs = slice(c * SGU_CHUNK, (c + 1) * SGU_CHUNK)
        for hd in range(SGU_HEADS):
            cs = slice(hd * LANES, (hd + 1) * LANES)
            mixed = _dot(ws_ref[hd], vn[rs, cs]) + bt[:, hd:hd + 1]
            o_ref[rs, cs] = (u[rs, cs] * mixed).astype(o_ref.dtype)


def _sgu(uv, norm_w, w_s, b_t):
    return pl.pallas_call(
        _sgu_kernel,
        grid=(N_TOK // TS,),
        in_specs=[pl.BlockSpec((TS, C_W), lambda i: (i, 0)),
                  pl.BlockSpec((TS, C_W), lambda i: (i, 1)),
                  pl.BlockSpec((1, C_W), lambda i: (0, 0)),
                  pl.BlockSpec((SGU_HEADS, SGU_CHUNK, SGU_CHUNK), lambda i: (0, 0, 0)),
                  pl.BlockSpec((SGU_CHUNK, SGU_HEADS), lambda i: (0, 0))],
        out_specs=pl.BlockSpec((TS, C_W), lambda i: (i, 0)),
        out_shape=jax.ShapeDtypeStruct((N_TOK, C_W), BF16),
        compiler_params=_params("arbitrary"),
        name="sgu",
    )(uv, uv, norm_w, w_s, b_t)


def _fnet_kernel(f_ref, ch_hi_ref, ch_lo_ref, ctx_hi_ref, ctx_lo_ref, lat_hi_ref, lat_lo_ref, w_ref, o_ref):
    fcs = _dot3(_split(f_ref[...]), (ch_hi_ref[...], ch_lo_ref[...]))
    wb = w_ref[...].astype(BF16)
    ctx = pl.program_id(0) < CTX_TILES

    def finish(rows, hi_ref, lo_ref):
        stacked = jnp.concatenate([fcs[rows, :F_W], fcs[rows, F_W:]], axis=0)
        spec = _dot3((hi_ref[...], lo_ref[...]), _split(stacked))
        o_ref[rows, :] = jnp.dot(spec.astype(BF16), wb, preferred_element_type=F32).astype(o_ref.dtype)

    @pl.when(ctx)
    def _():
        for s in range(TM // SEQ):
            finish(slice(s * SEQ, (s + 1) * SEQ), ctx_hi_ref, ctx_lo_ref)

    @pl.when(jnp.logical_not(ctx))
    def _():
        finish(slice(0, DEC_SEQ), lat_hi_ref, lat_lo_ref)


def _fnet(f_all, tables, w_bd):
    const = lambda a: pl.BlockSpec(a.shape, lambda i: (0, 0))
    return pl.pallas_call(
        _fnet_kernel,
        grid=(N_ROW_TILES,),
        in_specs=[pl.BlockSpec((TM, F_W), lambda i: (i, 0))] + [const(t) for t in tables] + [const(w_bd)],
        out_specs=pl.BlockSpec((TM, F_W), lambda i: (i, 0)),
        out_shape=jax.ShapeDtypeStruct((N_TOK, F_W), BF16),
        compiler_params=_params("arbitrary"),
        name="fnet",
    )(f_all, *tables, w_bd)


def _grid_pos_embed(n_tok):
    rows = n_tok // GRID_W
    r = np.repeat(np.arange(rows, dtype=np.float64), GRID_W)
    col = np.tile(np.arange(GRID_W, dtype=np.float64), rows)
    quarter = D_MODEL // 4
    freq = np.exp(-math.log(10000.0) * np.arange(quarter, dtype=np.float64) / quarter)

    def emb(p):
        ang = p[:, None] * freq[None, :]
        return np.concatenate([np.sin(ang), np.cos(ang)], axis=-1)

    return np.concatenate([emb(r), emb(col)], axis=-1).astype(np.float32)


def _dft_tables(n):
    k = np.arange(n, dtype=np.int64)
    ang = ((k[:, None] * k[None, :]) % n).astype(np.float64) * (2.0 * math.pi / n)
    scale = n ** -0.5
    return np.cos(ang) * scale, np.sin(ang) * scale


def _host_split(a):
    hi = a.astype(np.float32).astype(BF16)
    lo = (a - hi.astype(np.float64)).astype(np.float32).astype(BF16)
    return hi, lo


def _fnet_tables():
    cc, sc = _dft_tables(FN_CH)
    eye = np.eye(F_W // FN_CH)
    tables = list(_host_split(np.concatenate([np.kron(eye, cc), np.kron(eye, sc)], axis=1)))
    for n in (SEQ, DEC_SEQ):
        cn, sn = _dft_tables(n)
        tables += list(_host_split(np.concatenate([cn, -sn], axis=1)))
    return tables


def _block_diag(blocks):
    g, a, b = blocks.shape
    eye = jnp.eye(g, dtype=blocks.dtype)
    return (eye[:, None, :, None] * blocks[:, :, None, :]).reshape(g * a, g * b)


def kernel(x_prompt, x_sample, state_delta, c, c_ctx, ffn1_norm, ffn1_w_gate, ffn1_w_up, ffn1_w_down,
           mix_norm, ffn2_norm, ffn2_w_gate, ffn2_w_up, ffn2_w_down, ada_w, ada_b, ev_w_in, ev_w_out,
           pool_w, pool_scale, dn_conv_w, dn_a_log, dn_dt_bias, dn_norm_w, od_w_in, od_w_out, sgu_norm,
           sgu_w, sgu_b, fnet_w, final_norm):
    cond8 = jnp.zeros((SUBLANES, D_MODEL), F32).at[0].set(c_ctx).at[1:1 + DEC_BATCH].set(c)
    mods = _ada(cond8, ada_w, ada_b)
    mods = mods[:, :1 + DEC_BATCH].reshape(DEPTH, 1 + DEC_BATCH, N_MOD, 1, D_MODEL).transpose(0, 2, 1, 3, 4)

    fnet_tables = _fnet_tables()
    groups = ((SEQ, BATCH, 0, DN_HEADS), (DEC_SEQ, DEC_BATCH, CTX_TOK // DEC_SEQ, 3))
    even_segments = ((A_W, A_W + 4 * QK_W), (0, A_W), (A_W + 4 * QK_W, P_EVEN))
    odd_segments = ((0, 2 * C_W), (2 * C_W, P_ODD))

    x = (x_prompt.reshape(CTX_TOK, D_MODEL), x_sample.reshape(LAT_TOK, D_MODEL), _grid_pos_embed(DEC_SEQ))
    ev_w_in_t = jnp.swapaxes(ev_w_in, 1, 2)
    states = None
    for layer in range(DEPTH):
        x = _ffn(x, mods, layer, 0, ffn1_norm, ffn1_w_gate, ffn1_w_up, ffn1_w_down)
        if layer % 2 == 0:
            e = layer // 2
            qkvz, p_pool, gates, gates_t = _mixin(x, mods, layer, mix_norm, ev_w_in_t, e, even_segments, 512,
                                                  transposed=True)
            w_bd = _block_diag(pool_w[e])
            ya = _pool(p_pool, w_bd, pool_scale[e][None])
            yb = []
            for seq, n_seq, rb0, dn_heads in groups:
                ctx = rb0 == 0
                o, st = _deltanet(qkvz, gates, gates_t, dn_conv_w, e, dn_a_log[e], dn_dt_bias[e], dn_norm_w,
                                  None if ctx else state_delta, seq, dn_heads, n_seq, rb0,
                                  state_slot=e if ctx else None, states=states)
                yb.append(o)
                if ctx:
                    states = st
            mix = (ya, tuple(yb), ev_w_out, e)
        else:
            j = layer // 2
            uv, f_all = _mixin(x, mods, layer, mix_norm, od_w_in, j, odd_segments, TM)
            yc = _sgu(uv, sgu_norm[j][None], sgu_w[j], sgu_b[j].T)
            w_bd = _block_diag(fnet_w[j])
            yd = _fnet(f_all, fnet_tables, w_bd)
            mix = (yc, yd, od_w_out, j)
        x = _ffn(x, mods, layer, 2, ffn2_norm, ffn2_w_gate, ffn2_w_up, ffn2_w_down, mix=mix,
                 final_norm=final_norm if layer == DEPTH - 1 else None)

    y_prompt, y_sample = x
    return (y_prompt.reshape(BATCH, SEQ, D_MODEL), y_sample.reshape(DEC_BATCH, DEC_SEQ, D_MODEL), states)
```

```python
import functools
import math

import numpy as np
import jax
import jax.numpy as jnp
from jax import lax
from jax.experimental import pallas as pl
from jax.experimental.pallas import tpu as pltpu

F32 = jnp.float32
BF16 = jnp.bfloat16

D_MODEL = 1024
BATCH = 16
SEQ = 256
DEPTH = 4
DEC_BATCH = 2
DEC_SEQ = 1024
GRID_W = 64
EPS = 1e-6
D_FF = 2816
N_MOD = 9
N_EVEN = (DEPTH + 1) // 2
POOL_SIZES = (2, 4, 8, 16)
POOL_CH = 64
A_W = 256
DN_HEADS = 6
DN_DK = 128
QK_W = DN_HEADS * DN_DK
B_W = QK_W
GATE_W = 4 * DN_HEADS
P_EVEN = A_W + 4 * QK_W + GATE_W
SGU_CHUNK = 128
SGU_HEADS = 6
C_W = 768
FN_CH = 64
F_W = 256
P_ODD = 2 * C_W + F_W

CTX_TOK = BATCH * SEQ
LAT_TOK = DEC_BATCH * DEC_SEQ
N_TOK = CTX_TOK + LAT_TOK

LANES = 128
SUBLANES = 8
VMEM_LIMIT = 56 * 1024 * 1024

TM = 1024
N_ROW_TILES = N_TOK // TM
CTX_TILES = CTX_TOK // TM
TF = 256
DN_CHUNK = 128
COL_Q, COL_K, COL_V, COL_Z = 0, 1, 2, 3


def _cond_of_tile(i, tm):
    return jnp.where(i < CTX_TOK // tm, 0, 1 + (i - CTX_TOK // tm) * tm // DEC_SEQ)


def _silu(x):
    return x * jax.nn.sigmoid(x)


def _dot(a, b):
    return jnp.dot(a.astype(BF16), b.astype(BF16), preferred_element_type=F32)


def _dot_nt(a, b):
    return lax.dot_general(a.astype(BF16), b.astype(BF16), (((1,), (1,)), ((), ())),
                           preferred_element_type=F32)


def _dot_tn(a, b):
    return lax.dot_general(a.astype(BF16), b.astype(BF16), (((0,), (0,)), ((), ())),
                           preferred_element_type=F32)


def _split(a):
    hi = a.astype(BF16)
    lo = (a - hi.astype(F32)).astype(BF16)
    return hi, lo


def _dot3(a, b):
    ah, al = a
    bh, bl = b
    return (jnp.dot(ah, bh, preferred_element_type=F32)
            + (jnp.dot(al, bh, preferred_element_type=F32)
               + jnp.dot(ah, bl, preferred_element_type=F32)))


def _rms(x, w):
    return x * lax.rsqrt(jnp.mean(x * x, axis=-1, keepdims=True) + EPS) * w


def _norm_mod(x, nw, scale, shift):
    r = lax.rsqrt(jnp.mean(x * x, axis=-1, keepdims=True) + EPS)
    return ((x * r) * (nw * (1.0 + scale)) + shift).astype(BF16)


def _params(*sem):
    return pltpu.CompilerParams(dimension_semantics=sem, vmem_limit_bytes=VMEM_LIMIT)


def _ctx_rows(tm, width):
    return pl.BlockSpec((tm, width), lambda i, *_: (jnp.minimum(i, CTX_TOK // tm - 1), 0))


def _lat_rows(tm, width):
    return pl.BlockSpec((tm, width), lambda i, *_: (jnp.maximum(i - CTX_TOK // tm, 0), 0))


def _ada_kernel(c_ref, w_ref, b_ref, o_ref):
    o_ref[0] = _dot(_silu(c_ref[...]), w_ref[0]) + b_ref[0]


def _ada(cond8, ada_w, ada_b):
    tn = 1152
    n = N_MOD * D_MODEL
    return pl.pallas_call(
        _ada_kernel,
        grid=(DEPTH, n // tn),
        in_specs=[pl.BlockSpec((SUBLANES, D_MODEL), lambda l, j: (0, 0)),
                  pl.BlockSpec((1, D_MODEL, tn), lambda l, j: (l, 0, j)),
                  pl.BlockSpec((1, 1, tn), lambda l, j: (l, 0, j))],
        out_specs=pl.BlockSpec((1, SUBLANES, tn), lambda l, j: (l, 0, j)),
        out_shape=jax.ShapeDtypeStruct((DEPTH, SUBLANES, n), F32),
        compiler_params=_params("arbitrary", "arbitrary"),
        name="ada",
    )(cond8, ada_w, ada_b.reshape(DEPTH, 1, n))


def _mod_spec(layer, which, tm=TM, n_seg=1, seg=0):
    return pl.BlockSpec((None, None, None, 1, D_MODEL),
                        lambda i, *_: (layer, which, _cond_of_tile(i * n_seg + seg, tm), 0, 0))


def _layer_vec(layer):
    return pl.BlockSpec((None, 1, D_MODEL), lambda i, *_: (layer, 0, 0))


def _ffn_kernel(*refs, first, mix_counts, mix_split, final, n_seg):
    refs = list(refs)
    take = lambda k: [refs.pop(0) for _ in range(k)]
    if first:
        xp_ref, xs_ref, pos_ref = take(3)
    else:
        (x_in_ref,) = take(1)
    if mix_counts:
        mix_groups = [take(c) for c in mix_counts]
        g2_ref, wo_ref = take(2)
    (nw_ref,) = take(1)
    seg_mods = [take(3) for _ in range(n_seg)]
    wg_ref, wu_ref, wd_ref = take(3)
    if final:
        fn_ref, oc_ref, ol_ref = take(3)
    else:
        (o_ref,) = take(1)
    h_scr, acc_scr = take(2)
    x_ref = refs.pop(0) if (first or mix_counts) else x_in_ref
    i = pl.program_id(0)
    j = pl.program_id(1)
    ctx = i < CTX_TILES

    @pl.when(j == 0)
    def _():
        if first:
            @pl.when(ctx)
            def _():
                x_ref[...] = xp_ref[...]

            @pl.when(jnp.logical_not(ctx))
            def _():
                x_ref[...] = xs_ref[...] + pos_ref[...]
        elif mix_counts:
            def pick(rs):
                return rs[0][...] if len(rs) == 1 else jnp.where(ctx, rs[0][...], rs[1][...])

            mix = (_dot(pick(mix_groups[0]), wo_ref[0:mix_split, :])
                   + _dot(pick(mix_groups[1]), wo_ref[mix_split:, :]))
            x_ref[...] = x_in_ref[...] + g2_ref[...] * mix

        for s, (sh_ref, sc_ref, _) in enumerate(seg_mods):
            rows = slice(s * TM, (s + 1) * TM)
            h_scr[rows, :] = _norm_mod(x_ref[rows, :], nw_ref[...], sc_ref[...], sh_ref[...])
        acc_scr[...] = jnp.zeros_like(acc_scr)

    h = h_scr[...]
    g = jnp.dot(h, wg_ref[...].astype(BF16), preferred_element_type=F32)
    u = jnp.dot(h, wu_ref[...].astype(BF16), preferred_element_type=F32)
    acc_scr[...] += _dot(_silu(g) * u, wd_ref[...])

    @pl.when(j == pl.num_programs(1) - 1)
    def _():
        if final:
            y = _rms(x_ref[...] + 0.5 * seg_mods[0][2][...] * acc_scr[...], fn_ref[...])

            @pl.when(ctx)
            def _():
                oc_ref[...] = y

            @pl.when(jnp.logical_not(ctx))
            def _():
                ol_ref[...] = y
        else:
            for s, (_, _, gt_ref) in enumerate(seg_mods):
                rows = slice(s * TM, (s + 1) * TM)
                o_ref[rows, :] = x_ref[rows, :] + 0.5 * gt_ref[...] * acc_scr[rows, :]


def _ffn(x, mods, layer, sub, norm_w, w_gate, w_up, w_down, mix=None, final_norm=None):
    first = isinstance(x, tuple)
    final = final_norm is not None
    n_seg = 2 if (not first and mix is None and not final) else 1
    tm = n_seg * TM
    row = lambda width: pl.BlockSpec((tm, width), lambda i, j: (i, 0))
    if first:
        in_specs = [_ctx_rows(TM, D_MODEL), _lat_rows(TM, D_MODEL),
                    pl.BlockSpec((TM, D_MODEL), lambda i, j: (0, 0))]
        args = list(x)
    else:
        in_specs, args = [row(D_MODEL)], [x]
    mix_counts, mix_split = (), 0
    if mix is not None:
        ya, yb, w_out, w_index = mix
        counts = []
        for y in (ya, yb):
            parts = y if isinstance(y, tuple) else (y,)
            width = parts[0].shape[1]
            in_specs += [row(width)] if len(parts) == 1 else [_ctx_rows(TM, width), _lat_rows(TM, width)]
            args += list(parts)
            counts.append(len(parts))
        mix_counts = tuple(counts)
        mix_split = (ya[0] if isinstance(ya, tuple) else ya).shape[1]
        in_specs += [_mod_spec(layer, 5),
                     pl.BlockSpec((None, D_MODEL, D_MODEL), lambda i, j: (w_index, 0, 0),
                                  pipeline_mode=pl.Buffered(1))]
        args += [mods, w_out]
    in_specs.append(_layer_vec(layer))
    args.append(norm_w.reshape(DEPTH, 1, D_MODEL))
    for seg in range(n_seg):
        in_specs += [_mod_spec(layer, 3 * sub + k, TM, n_seg, seg) for k in range(3)]
        args += [mods, mods, mods]
    in_specs += [pl.BlockSpec((None, D_MODEL, TF), lambda i, j: (layer, 0, j)),
                 pl.BlockSpec((None, D_MODEL, TF), lambda i, j: (layer, 0, j)),
                 pl.BlockSpec((None, TF, D_MODEL), lambda i, j: (layer, j, 0))]
    args += [w_gate, w_up, w_down]
    scratch = [pltpu.VMEM((tm, D_MODEL), BF16), pltpu.VMEM((tm, D_MODEL), F32)]
    if first or mix_counts:
        scratch.append(pltpu.VMEM((TM, D_MODEL), F32))
    if final:
        in_specs.append(pl.BlockSpec((1, D_MODEL), lambda i, j: (0, 0)))
        args.append(final_norm[None])
        out_specs = [_ctx_rows(TM, D_MODEL), _lat_rows(TM, D_MODEL)]
        out_shape = [jax.ShapeDtypeStruct((CTX_TOK, D_MODEL), F32), jax.ShapeDtypeStruct((LAT_TOK, D_MODEL), F32)]
    else:
        out_specs = row(D_MODEL)
        out_shape = jax.ShapeDtypeStruct((N_TOK, D_MODEL), F32)
    return pl.pallas_call(
        functools.partial(_ffn_kernel, first=first, mix_counts=mix_counts, mix_split=mix_split, final=final,
                          n_seg=n_seg),
        grid=(N_TOK // tm, D_FF // TF),
        in_specs=in_specs,
        out_specs=out_specs,
        out_shape=out_shape,
        scratch_shapes=scratch,
        compiler_params=_params("arbitrary", "arbitrary"),
        name=f"ffn{sub}_{layer}",
    )(*args)


def _mixin_kernel(x_ref, nw_ref, sh_ref, sc_ref, w_ref, *rest, segments, transposed):
    o_refs, wb_scr = rest[:-1], rest[-1]

    @pl.when(pl.program_id(0) == 0)
    def _():
        wb_scr[...] = w_ref[...].astype(BF16)

    h = _norm_mod(x_ref[...], nw_ref[...], sc_ref[...], sh_ref[...])
    for (lo, hi), o_ref in zip(segments, o_refs):
        if transposed:
            o_ref[...] = _dot_nt(h, wb_scr[lo:hi, :])
        else:
            o_ref[...] = _dot(h, wb_scr[:, lo:hi])
    if transposed:
        lo, hi = segments[-1]
        o_refs[-1][...] = _dot_nt(wb_scr[lo:hi, :], h)


def _mixin(x, mods, layer, norm_w, w_in, w_index, segments, tm, transposed=False):
    shape = w_in.shape[1:]
    out_specs = [pl.BlockSpec((tm, hi - lo), lambda i: (i, 0)) for lo, hi in segments]
    out_shape = [jax.ShapeDtypeStruct((N_TOK, hi - lo), F32) for lo, hi in segments]
    if transposed:
        lo, hi = segments[-1]
        out_specs.append(pl.BlockSpec((hi - lo, tm), lambda i: (0, i)))
        out_shape.append(jax.ShapeDtypeStruct((hi - lo, N_TOK), F32))
    return pl.pallas_call(
        functools.partial(_mixin_kernel, segments=segments, transposed=transposed),
        grid=(N_TOK // tm,),
        in_specs=[pl.BlockSpec((tm, D_MODEL), lambda i: (i, 0)),
                  _layer_vec(layer), _mod_spec(layer, 3, tm), _mod_spec(layer, 4, tm),
                  pl.BlockSpec((None,) + shape, lambda i: (w_index, 0, 0), pipeline_mode=pl.Buffered(1))],
        out_specs=out_specs,
        out_shape=out_shape,
        scratch_shapes=[pltpu.VMEM(shape, BF16)],
        compiler_params=_params("arbitrary"),
        name=f"mixin_{layer}",
    )(x, norm_w.reshape(DEPTH, 1, D_MODEL), mods, mods, w_in)


def _pool_kernel(p_ref, w_ref, scale_ref, o_ref):
    seq = jnp.where(pl.program_id(0) < CTX_TILES, SEQ, DEC_SEQ)
    p = p_ref[...]
    pos = lax.broadcasted_iota(jnp.int32, (TM, A_W), 0) & (seq - 1)

    def before(a, s):
        return jnp.where(pos >= s, pltpu.roll(a, s, axis=0), 0.0)

    def after(a, s):
        return jnp.where(pos < seq - s, pltpu.roll(a, TM - s, axis=0), 0.0)

    sums = []
    f, b = p, before(p, 1)
    for h in (size // 2 for size in POOL_SIZES):
        if h > 1:
            f = f + after(f, h // 2)
            b = b + before(b, h // 2)
        sums.append(f + b)
    group = lax.broadcasted_iota(jnp.int32, (TM, A_W), 1) >> (POOL_CH.bit_length() - 1)
    wsum = jnp.where(group == 0, sums[0], jnp.where(group == 1, sums[1], jnp.where(group == 2, sums[2], sums[3])))
    half = jnp.left_shift(1, group)
    cnt = (jnp.minimum(pos + half, seq) - jnp.maximum(pos - half, 0)).astype(F32)
    d = wsum / cnt - p
    o_ref[...] = (_dot(d, w_ref[...]) * scale_ref[...]).astype(o_ref.dtype)


def _pool(p_pool, w_bd, scale):
    return pl.pallas_call(
        _pool_kernel,
        grid=(N_ROW_TILES,),
        in_specs=[pl.BlockSpec((TM, A_W), lambda i: (i, 0)),
                  pl.BlockSpec((A_W, A_W), lambda i: (0, 0)),
                  pl.BlockSpec((1, A_W), lambda i: (0, 0))],
        out_specs=pl.BlockSpec((TM, A_W), lambda i: (i, 0)),
        out_shape=jax.ShapeDtypeStruct((N_TOK, A_W), BF16),
        compiler_params=_params("arbitrary"),
        name="pool",
    )(p_pool, w_bd, scale)


def _link_mask(r, c, s, upper):
    lg = s.bit_length() - 1
    same = (r >> (lg + 1)) == (c >> (lg + 1))
    r_half = (r >> lg) & 1
    c_half = (c >> lg) & 1
    return same & ((r_half == 0) & (c_half == 1) if upper else (r_half == 1) & (c_half == 0))


def _dn_kernel(q_ref, k_ref, v_ref, z_ref, g_ref, gt_ref, cq_ref, ck_ref, cv_ref, alog_ref, dtb_ref,
               alog_c_ref, dtb_c_ref, nw_ref, *rest, seq, heads, zero_init, state_slot):
    rest = list(rest)
    s0_ref = None if zero_init else rest.pop(0)
    if state_slot:
        rest.pop(0)
    o_ref = rest.pop(0)
    st_ref = rest.pop(0) if state_slot is not None else None
    gct_scr = rest.pop(0)
    h0 = pl.program_id(1) * heads
    n = DN_CHUNK
    n_blk = seq // n
    row = lax.broadcasted_iota(jnp.int32, (seq, LANES), 0)
    g_lane = lax.broadcasted_iota(jnp.int32, (seq, GATE_W), 1)
    r2 = lax.broadcasted_iota(jnp.int32, (n, n), 0)
    c2 = lax.broadcasted_iota(jnp.int32, (n, n), 1)
    eye = (r2 == c2).astype(F32)
    incl = (r2 >= c2, r2 <= c2)
    strict = (r2 > c2, r2 < c2)
    levels = [1 << b for b in range(n.bit_length() - 1)]
    link = {(s, d): _link_mask(r2, c2, s, d == 1) for s in levels for d in (0, 1)}
    units = [(hh, blk, d) for hh in range(heads) for blk in range(n_blk) for d in (0, 1)]

    def conv_silu(x, cw):
        prev = jnp.where(row >= 1, pltpu.roll(x, 1, axis=0), 0.0)
        nxt = jnp.where(row <= seq - 2, pltpu.roll(x, seq - 1, axis=0), 0.0)
        return _silu(prev * cw[0:1] + x * cw[1:2] + nxt * cw[2:3])

    def l2n(x):
        return x * lax.rsqrt(jnp.sum(x * x, axis=-1, keepdims=True) + EPS)

    def col(a, idx):
        return jnp.sum(jnp.where(g_lane == idx, a, 0.0), axis=1, keepdims=True)

    def log_decay(a, alog, dtb):
        xg = a + dtb
        return -jnp.exp(alog) * (jnp.maximum(xg, 0.0) + jnp.log1p(jnp.exp(-jnp.abs(xg))))

    gates = g_ref[...]
    beta_all = jax.nn.sigmoid(gates)
    g_all = log_decay(gates, alog_ref[...], dtb_ref[...])
    g_all_t = log_decay(gt_ref[...], alog_c_ref[...], dtb_c_ref[...])
    lower = incl[0].astype(BF16)
    upper = incl[1].astype(BF16)
    backward_rows = lax.broadcasted_iota(jnp.int32, (GATE_W, n), 0) >= 3 * DN_HEADS

    def tri_sums(a, left):
        rem, pre_sum, suf_sum = a, 0.0, 0.0
        for _ in range(3):
            piece = rem.astype(BF16)
            rem = rem - piece.astype(F32)
            if left:
                pre_sum = pre_sum + jnp.dot(lower, piece, preferred_element_type=F32)
                suf_sum = suf_sum + jnp.dot(upper, piece, preferred_element_type=F32)
            else:
                pre_sum = pre_sum + jnp.dot(piece, upper, preferred_element_type=F32)
                suf_sum = suf_sum + jnp.dot(piece, lower, preferred_element_type=F32)
        return pre_sum, suf_sum

    pre_blocks, suf_blocks = [], []
    for blk in range(n_blk):
        rs = slice(blk * n, (blk + 1) * n)
        p, s_ = tri_sums(g_all[rs, :], left=True)
        pre_blocks.append(p)
        suf_blocks.append(s_)
        p_t, s_t = tri_sums(g_all_t[:, rs], left=False)
        sums = jnp.where(backward_rows, s_t, p_t)
        for r in range(2 * DN_HEADS):
            gct_scr[r, :, rs] = sums[2 * DN_HEADS + r:2 * DN_HEADS + r + 1, :]
    pre = jnp.concatenate(pre_blocks, axis=0)
    suf = jnp.concatenate(suf_blocks, axis=0)

    qs, kn, vv, beta, gc = [], [], [], [], []
    for hh in range(heads):
        cs = slice(hh * LANES, (hh + 1) * LANES)
        qs.append(l2n(conv_silu(q_ref[:, cs], cq_ref[:, cs])) * (DN_DK ** -0.5))
        kn.append(l2n(conv_silu(k_ref[:, cs], ck_ref[:, cs])))
        vv.append(conv_silu(v_ref[:, cs], cv_ref[:, cs]))
        beta.append((col(beta_all, h0 + hh), col(beta_all, DN_HEADS + h0 + hh)))
        gc.append((col(pre, 2 * DN_HEADS + h0 + hh), col(suf, 3 * DN_HEADS + h0 + hh)))

    kk, qk = {}, {}
    for hh in range(heads):
        for blk in range(n_blk):
            rs = slice(blk * n, (blk + 1) * n)
            kk[hh, blk] = _dot_nt(kn[hh][rs], kn[hh][rs])
            qk[hh, blk] = _dot_nt(qs[hh][rs], kn[hh][rs])

    m, m_hi, t, a_in = {}, {}, {}, {}
    for u in units:
        hh, blk, d = u
        rs = slice(blk * n, (blk + 1) * n)
        g_lanes = gct_scr[d * DN_HEADS + h0 + hh, :, rs]
        decay = jnp.where(incl[d], jnp.exp(gc[hh][d][rs] - g_lanes), 0.0)
        m[u] = jnp.where(strict[d], beta[hh][d][rs] * kk[hh, blk] * decay, 0.0)
        m_hi[u] = m[u].astype(BF16)
        a_in[u] = (qk[hh, blk] * decay).astype(BF16)
        t[u] = eye - jnp.where(link[1, d], m[u], 0.0)

    for s in levels[1:]:
        tb, x = {}, {}
        for u in units:
            tb[u] = t[u].astype(BF16)
            x[u] = jnp.where(link[s, u[2]], jnp.dot(m_hi[u], tb[u], preferred_element_type=F32), 0.0)
        for u in units:
            t[u] = t[u] - jnp.dot(tb[u], x[u].astype(BF16), preferred_element_type=F32)

    uw, q_dec, k_dec, g_last = {}, {}, {}, {}
    for u in units:
        hh, blk, d = u
        rs = slice(blk * n, (blk + 1) * n)
        gcol = gc[hh][d][rs]
        bcol = beta[hh][d][rs]
        e_g = jnp.exp(gcol)
        kb = kn[hh][rs]
        uw[u] = _dot(t[u], jnp.concatenate([vv[hh][rs] * bcol, kb * (bcol * e_g)], axis=1))
        g_last[u] = gcol[n - 1:n] if d == 0 else gcol[0:1]
        q_dec[u] = qs[hh][rs] * e_g
        k_dec[u] = kb * jnp.exp(g_last[u] - gcol)

    state = {}
    for hh in range(heads):
        for d in (0, 1):
            state[hh, d] = jnp.zeros((DN_DK, DN_DK), F32) if zero_init else s0_ref[d, hh]
    outs = {}
    for step in range(n_blk):
        chains = [(hh, step if d == 0 else n_blk - 1 - step, d) for hh in range(heads) for d in (0, 1)]
        ws_qs = {}
        for u in chains:
            ws_qs[u] = _dot(jnp.concatenate([uw[u][:, LANES:], q_dec[u]], axis=0), state[u[0], u[2]])
        for u in chains:
            v_new = (uw[u][:, :LANES] - ws_qs[u][0:n]).astype(BF16)
            outs[u] = ws_qs[u][n:] + jnp.dot(a_in[u], v_new, preferred_element_type=F32)
            state[u[0], u[2]] = state[u[0], u[2]] * jnp.exp(g_last[u]) + _dot_tn(k_dec[u], v_new)

    for hh in range(heads):
        cs = slice(hh * LANES, (hh + 1) * LANES)
        o = (jnp.concatenate([outs[hh, blk, 0] for blk in range(n_blk)], axis=0)
             + jnp.concatenate([outs[hh, blk, 1] for blk in range(n_blk)], axis=0))
        o = o * lax.rsqrt(jnp.mean(o * o, axis=-1, keepdims=True) + EPS) * nw_ref[...]
        o_ref[:, cs] = (o * _silu(z_ref[:, cs])).astype(o_ref.dtype)
        if state_slot == 0:
            st_ref[0, 0, hh] = state[hh, 0]
            st_ref[0, 1, hh] = state[hh, 1]
            st_ref[1:, :, hh] = jnp.zeros((N_EVEN - 1, 2, DN_DK, DN_DK), F32)
        elif state_slot:
            st_ref[0, hh] = state[hh, 0]
            st_ref[1, hh] = state[hh, 1]


def _deltanet(qkvz, gates, gates_t, conv_w, e, alog, dtb, norm_w, s0, seq, heads, n_seq, row_block0,
              state_slot=None, states=None):
    zero_init = s0 is None
    width = heads * LANES
    per = DN_HEADS // heads
    alog_row = jnp.concatenate([jnp.zeros((2 * DN_HEADS,), F32), alog.reshape(-1)])[None]
    dtb_row = jnp.concatenate([jnp.zeros((2 * DN_HEADS,), F32), dtb.reshape(-1)])[None]

    def pcol(cb):
        return pl.BlockSpec((seq, width), lambda b, h: (row_block0 + b, cb * per + h))

    def ccol(cb):
        return pl.BlockSpec((None, 3, width), lambda b, h: (e, 0, cb * per + h))

    grow = pl.BlockSpec((1, GATE_W), lambda b, h: (0, 0))
    gcol = pl.BlockSpec((GATE_W, 1), lambda b, h: (0, 0))
    in_specs = [pcol(COL_Q), pcol(COL_K), pcol(COL_V), pcol(COL_Z),
                pl.BlockSpec((seq, GATE_W), lambda b, h: (row_block0 + b, 0)),
                pl.BlockSpec((GATE_W, seq), lambda b, h: (0, row_block0 + b)),
                ccol(0), ccol(1), ccol(2), grow, grow, gcol, gcol,
                pl.BlockSpec((None, 1, LANES), lambda b, h: (e, 0, 0))]
    args = [qkvz, qkvz, qkvz, qkvz, gates, gates_t, conv_w, conv_w, conv_w, alog_row, dtb_row,
            alog_row.T, dtb_row.T, norm_w.reshape(N_EVEN, 1, LANES)]
    if not zero_init:
        in_specs.append(pl.BlockSpec((None, None, 2, heads, DN_DK, DN_DK), lambda b, h: (b, e, 0, h, 0, 0)))
        args.append(s0)
    aliases = {}
    if state_slot:
        aliases = {len(args): 1}
        in_specs.append(pl.BlockSpec(memory_space=pl.ANY))
        args.append(states)
    out_specs = [pl.BlockSpec((seq, width), lambda b, h: (b, h))]
    out_shape = [jax.ShapeDtypeStruct((n_seq * seq, B_W), BF16)]
    if state_slot is not None:
        if state_slot == 0:
            out_specs.append(pl.BlockSpec((None, N_EVEN, 2, heads, DN_DK, DN_DK), lambda b, h: (b, 0, 0, h, 0, 0)))
        else:
            out_specs.append(pl.BlockSpec((None, None, 2, heads, DN_DK, DN_DK),
                                          lambda b, h: (b, state_slot, 0, h, 0, 0)))
        out_shape.append(jax.ShapeDtypeStruct((n_seq, N_EVEN, 2, DN_HEADS, DN_DK, DN_DK), F32))
    res = pl.pallas_call(
        functools.partial(_dn_kernel, seq=seq, heads=heads, zero_init=zero_init, state_slot=state_slot),
        grid=(n_seq, per),
        in_specs=in_specs,
        out_specs=out_specs,
        out_shape=out_shape,
        scratch_shapes=[pltpu.VMEM((2 * DN_HEADS, 1, seq), F32)],
        input_output_aliases=aliases,
        compiler_params=_params("arbitrary", "arbitrary"),
        name=f"deltanet_{seq}",
    )(*args)
    return res if state_slot is not None else (res[0], None)


TS = 512


def _sgu_kernel(u_ref, v_ref, nw_ref, ws_ref, bt_ref, o_ref):
    u = jax.nn.gelu(u_ref[...])
    v = jax.nn.gelu(v_ref[...])
    mu = jnp.mean(v, axis=-1, keepdims=True)
    vc = v - mu
    var = jnp.mean(vc * vc, axis=-1, keepdims=True)
    vn = vc * lax.rsqrt(var + EPS) * nw_ref[...]
    bt = bt_ref[...]
    for c in range(TS // SGU_CHUNK):
        rs = slice(c * SGU_CHUNK, (c + 1) * SGU_CHUNK)
        for hd in range(SGU_HEADS):
            cs = slice(hd * LANES, (hd + 1) * LANES)
            mixed = _dot(ws_ref[hd], vn[rs, cs]) + bt[:, hd:hd + 1]
            o_ref[rs, cs] = (u[rs, cs] * mixed).astype(o_ref.dtype)


def _sgu(uv, norm_w, w_s, b_t):
    return pl.pallas_call(
        _sgu_kernel,
        grid=(N_TOK // TS,),
        in_specs=[pl.BlockSpec((TS, C_W), lambda i: (i, 0)),
                  pl.BlockSpec((TS, C_W), lambda i: (i, 1)),
                  pl.BlockSpec((1, C_W), lambda i: (0, 0)),
                  pl.BlockSpec((SGU_HEADS, SGU_CHUNK, SGU_CHUNK), lambda i: (0, 0, 0)),
                  pl.BlockSpec((SGU_CHUNK, SGU_HEADS), lambda i: (0, 0))],
        out_specs=pl.BlockSpec((TS, C_W), lambda i: (i, 0)),
        out_shape=jax.ShapeDtypeStruct((N_TOK, C_W), BF16),
        compiler_params=_params("arbitrary"),
        name="sgu",
    )(uv, uv, norm_w, w_s, b_t)


def _fnet_kernel(f_ref, ch_hi_ref, ch_lo_ref, ctx_hi_ref, ctx_lo_ref, lat_hi_ref, lat_lo_ref, w_ref, o_ref):
    fcs = _dot3(_split(f_ref[...]), (ch_hi_ref[...], ch_lo_ref[...]))
    wb = w_ref[...].astype(BF16)
    ctx = pl.program_id(0) < CTX_TILES

    def finish(rows, hi_ref, lo_ref):
        stacked = jnp.concatenate([fcs[rows, :F_W], fcs[rows, F_W:]], axis=0)
        spec = _dot3((hi_ref[...], lo_ref[...]), _split(stacked))
        o_ref[rows, :] = jnp.dot(spec.astype(BF16), wb, preferred_element_type=F32).astype(o_ref.dtype)

    @pl.when(ctx)
    def _():
        for s in range(TM // SEQ):
            finish(slice(s * SEQ, (s + 1) * SEQ), ctx_hi_ref, ctx_lo_ref)

    @pl.when(jnp.logical_not(ctx))
    def _():
        finish(slice(0, DEC_SEQ), lat_hi_ref, lat_lo_ref)


def _fnet(f_all, tables, w_bd):
    const = lambda a: pl.BlockSpec(a.shape, lambda i: (0, 0))
    return pl.pallas_call(
        _fnet_kernel,
        grid=(N_ROW_TILES,),
        in_specs=[pl.BlockSpec((TM, F_W), lambda i: (i, 0))] + [const(t) for t in tables] + [const(w_bd)],
        out_specs=pl.BlockSpec((TM, F_W), lambda i: (i, 0)),
        out_shape=jax.ShapeDtypeStruct((N_TOK, F_W), BF16),
        compiler_params=_params("arbitrary"),
        name="fnet",
    )(f_all, *tables, w_bd)


def _grid_pos_embed(n_tok):
    rows = n_tok // GRID_W
    r = np.repeat(np.arange(rows, dtype=np.float64), GRID_W)
    col = np.tile(np.arange(GRID_W, dtype=np.float64), rows)
    quarter = D_MODEL // 4
    freq = np.exp(-math.log(10000.0) * np.arange(quarter, dtype=np.float64) / quarter)

    def emb(p):
        ang = p[:, None] * freq[None, :]
        return np.concatenate([np.sin(ang), np.cos(ang)], axis=-1)

    return np.concatenate([emb(r), emb(col)], axis=-1).astype(np.float32)


def _dft_tables(n):
    k = np.arange(n, dtype=np.int64)
    ang = ((k[:, None] * k[None, :]) % n).astype(np.float64) * (2.0 * math.pi / n)
    scale = n ** -0.5
    return np.cos(ang) * scale, np.sin(ang) * scale


def _host_split(a):
    hi = a.astype(np.float32).astype(BF16)
    lo = (a - hi.astype(np.float64)).astype(np.float32).astype(BF16)
    return hi, lo


def _fnet_tables():
    cc, sc = _dft_tables(FN_CH)
    eye = np.eye(F_W // FN_CH)
    tables = list(_host_split(np.concatenate([np.kron(eye, cc), np.kron(eye, sc)], axis=1)))
    for n in (SEQ, DEC_SEQ):
        cn, sn = _dft_tables(n)
        tables += list(_host_split(np.concatenate([cn, -sn], axis=1)))
    return tables


def _block_diag(blocks):
    g, a, b = blocks.shape
    eye = jnp.eye(g, dtype=blocks.dtype)
    return (eye[:, None, :, None] * blocks[:, :, None, :]).reshape(g * a, g * b)


def kernel(x_prompt, x_sample, state_delta, c, c_ctx, ffn1_norm, ffn1_w_gate, ffn1_w_up, ffn1_w_down,
           mix_norm, ffn2_norm, ffn2_w_gate, ffn2_w_up, ffn2_w_down, ada_w, ada_b, ev_w_in, ev_w_out,
           pool_w, pool_scale, dn_conv_w, dn_a_log, dn_dt_bias, dn_norm_w, od_w_in, od_w_out, sgu_norm,
           sgu_w, sgu_b, fnet_w, final_norm):
    cond8 = jnp.zeros((SUBLANES, D_MODEL), F32).at[0].set(c_ctx).at[1:1 + DEC_BATCH].set(c)
    mods = _ada(cond8, ada_w, ada_b)
    mods = mods[:, :1 + DEC_BATCH].reshape(DEPTH, 1 + DEC_BATCH, N_MOD, 1, D_MODEL).transpose(0, 2, 1, 3, 4)

    fnet_tables = _fnet_tables()
    groups = ((SEQ, BATCH, 0, DN_HEADS), (DEC_SEQ, DEC_BATCH, CTX_TOK // DEC_SEQ, 3))
    even_segments = ((A_W, A_W + 4 * QK_W), (0, A_W), (A_W + 4 * QK_W, P_EVEN))
    odd_segments = ((0, 2 * C_W), (2 * C_W, P_ODD))

    x = (x_prompt.reshape(CTX_TOK, D_MODEL), x_sample.reshape(LAT_TOK, D_MODEL), _grid_pos_embed(DEC_SEQ))
    ev_w_in_t = jnp.swapaxes(ev_w_in, 1, 2)
    states = None
    for layer in range(DEPTH):
        x = _ffn(x, mods, layer, 0, ffn1_norm, ffn1_w_gate, ffn1_w_up, ffn1_w_down)
        if layer % 2 == 0:
            e = layer // 2
            qkvz, p_pool, gates, gates_t = _mixin(x, mods, layer, mix_norm, ev_w_in_t, e, even_segments, 512,
                                                  transposed=True)
            w_bd = _block_diag(pool_w[e])
            ya = _pool(p_pool, w_bd, pool_scale[e][None])
            yb = []
            for seq, n_seq, rb0, dn_heads in groups:
                ctx = rb0 == 0
                o, st = _deltanet(qkvz, gates, gates_t, dn_conv_w, e, dn_a_log[e], dn_dt_bias[e], dn_norm_w,
                                  None if ctx else state_delta, seq, dn_heads, n_seq, rb0,
                                  state_slot=e if ctx else None, states=states)
                yb.append(o)
                if ctx:
                    states = st
            mix = (ya, tuple(yb), ev_w_out, e)
        else:
            j = layer // 2
            uv, f_all = _mixin(x, mods, layer, mix_norm, od_w_in, j, odd_segments, TM)
            yc = _sgu(uv, sgu_norm[j][None], sgu_w[j], sgu_b[j].T)
            w_bd = _block_diag(fnet_w[j])
            yd = _fnet(f_all, fnet_tables, w_bd)
            mix = (yc, yd, od_w_out, j)
        x = _ffn(x, mods, layer, 2, ffn2_norm, ffn2_w_gate, ffn2_w_up, ffn2_w_down, mix=mix,
                 final_norm=final_norm if layer == DEPTH - 1 else None)

    y_prompt, y_sample = x
    return (y_prompt.reshape(BATCH, SEQ, D_MODEL), y_sample.reshape(DEC_BATCH, DEC_SEQ, D_MODEL), states)
```

```python
import functools
import math

import numpy as np
import jax
import jax.numpy as jnp
from jax import lax
from jax.experimental import pallas as pl
from jax.experimental.pallas import tpu as pltpu

F32 = jnp.float32
BF16 = jnp.bfloat16

D_MODEL = 1024
BATCH = 16
SEQ = 256
DEPTH = 4
DEC_BATCH = 2
DEC_SEQ = 1024
GRID_W = 64
EPS = 1e-6
D_FF = 2816
N_MOD = 9
N_EVEN = (DEPTH + 1) // 2
POOL_SIZES = (2, 4, 8, 16)
POOL_CH = 64
A_W = 256
DN_HEADS = 6
DN_DK = 128
QK_W = DN_HEADS * DN_DK
B_W = QK_W
GATE_W = 4 * DN_HEADS
P_EVEN = A_W + 4 * QK_W + GATE_W
SGU_CHUNK = 128
SGU_HEADS = 6
C_W = 768
FN_CH = 64
F_W = 256
P_ODD = 2 * C_W + F_W

CTX_TOK = BATCH * SEQ
LAT_TOK = DEC_BATCH * DEC_SEQ
N_TOK = CTX_TOK + LAT_TOK

LANES = 128
SUBLANES = 8
VMEM_LIMIT = 56 * 1024 * 1024

TM = 1024
N_ROW_TILES = N_TOK // TM
CTX_TILES = CTX_TOK // TM
TF = 256
DN_CHUNK = 128
COL_Q, COL_K, COL_V, COL_Z = 0, 1, 2, 3


def _cond_of_tile(i, tm):
    return jnp.where(i < CTX_TOK // tm, 0, 1 + (i - CTX_TOK // tm) * tm // DEC_SEQ)


def _silu(x):
    return x * jax.nn.sigmoid(x)


def _dot(a, b):
    return jnp.dot(a.astype(BF16), b.astype(BF16), preferred_element_type=F32)


def _dot_nt(a, b):
    return lax.dot_general(a.astype(BF16), b.astype(BF16), (((1,), (1,)), ((), ())),
                           preferred_element_type=F32)


def _dot_tn(a, b):
    return lax.dot_general(a.astype(BF16), b.astype(BF16), (((0,), (0,)), ((), ())),
                           preferred_element_type=F32)


def _split(a):
    hi = a.astype(BF16)
    lo = (a - hi.astype(F32)).astype(BF16)
    return hi, lo


def _dot3(a, b):
    ah, al = a
    bh, bl = b
    return (jnp.dot(ah, bh, preferred_element_type=F32)
            + (jnp.dot(al, bh, preferred_element_type=F32)
               + jnp.dot(ah, bl, preferred_element_type=F32)))


def _rms(x, w):
    return x * lax.rsqrt(jnp.mean(x * x, axis=-1, keepdims=True) + EPS) * w


def _norm_mod(x, nw, scale, shift):
    r = lax.rsqrt(jnp.mean(x * x, axis=-1, keepdims=True) + EPS)
    return ((x * r) * (nw * (1.0 + scale)) + shift).astype(BF16)


def _params(*sem):
    return pltpu.CompilerParams(dimension_semantics=sem, vmem_limit_bytes=VMEM_LIMIT)


def _ctx_rows(tm, width):
    return pl.BlockSpec((tm, width), lambda i, *_: (jnp.minimum(i, CTX_TOK // tm - 1), 0))


def _lat_rows(tm, width):
    return pl.BlockSpec((tm, width), lambda i, *_: (jnp.maximum(i - CTX_TOK // tm, 0), 0))


def _ada_kernel(c_ref, w_ref, b_ref, o_ref):
    o_ref[0] = _dot(_silu(c_ref[...]), w_ref[0]) + b_ref[0]


def _ada(cond8, ada_w, ada_b):
    tn = 1152
    n = N_MOD * D_MODEL
    return pl.pallas_call(
        _ada_kernel,
        grid=(DEPTH, n // tn),
        in_specs=[pl.BlockSpec((SUBLANES, D_MODEL), lambda l, j: (0, 0)),
                  pl.BlockSpec((1, D_MODEL, tn), lambda l, j: (l, 0, j)),
                  pl.BlockSpec((1, 1, tn), lambda l, j: (l, 0, j))],
        out_specs=pl.BlockSpec((1, SUBLANES, tn), lambda l, j: (l, 0, j)),
        out_shape=jax.ShapeDtypeStruct((DEPTH, SUBLANES, n), F32),
        compiler_params=_params("arbitrary", "arbitrary"),
        name="ada",
    )(cond8, ada_w, ada_b.reshape(DEPTH, 1, n))


def _mod_spec(layer, which, tm=TM, n_seg=1, seg=0):
    return pl.BlockSpec((None, None, None, 1, D_MODEL),
                        lambda i, *_: (layer, which, _cond_of_tile(i * n_seg + seg, tm), 0, 0))


def _layer_vec(layer):
    return pl.BlockSpec((None, 1, D_MODEL), lambda i, *_: (layer, 0, 0))


def _ffn_kernel(*refs, first, prenormed, final, n_seg):
    refs = list(refs)
    take = lambda k: [refs.pop(0) for _ in range(k)]
    if first:
        xp_ref, xs_ref, pos_ref = take(3)
    else:
        (x_in_ref,) = take(1)
    if prenormed:
        (h_ref,) = take(1)
        seg_mods = [(None, None) + tuple(take(1)) for _ in range(n_seg)]
    else:
        (nw_ref,) = take(1)
        seg_mods = [take(3) for _ in range(n_seg)]
    wg_ref, wu_ref, wd_ref = take(3)
    if final:
        fn_ref, oc_ref, ol_ref = take(3)
    else:
        (o_ref,) = take(1)
    if not prenormed:
        (h_ref,) = take(1)
    acc_ref = o_ref if (prenormed and not final) else refs.pop(0)
    x_ref = refs.pop(0) if first else x_in_ref
    i = pl.program_id(0)
    j = pl.program_id(1)
    ctx = i < CTX_TILES

    @pl.when(j == 0)
    def _():
        if first:
            @pl.when(ctx)
            def _():
                x_ref[...] = xp_ref[...]

            @pl.when(jnp.logical_not(ctx))
            def _():
                x_ref[...] = xs_ref[...] + pos_ref[...]
        if not prenormed:
            for s, (sh_ref, sc_ref, _) in enumerate(seg_mods):
                rows = slice(s * TM, (s + 1) * TM)
                h_ref[rows, :] = _norm_mod(x_ref[rows, :], nw_ref[...], sc_ref[...], sh_ref[...])
        acc_ref[...] = jnp.zeros_like(acc_ref)

    h = h_ref[...]
    g = jnp.dot(h, wg_ref[...].astype(BF16), preferred_element_type=F32)
    u = jnp.dot(h, wu_ref[...].astype(BF16), preferred_element_type=F32)
    acc_ref[...] += _dot(_silu(g) * u, wd_ref[...])

    @pl.when(j == pl.num_programs(1) - 1)
    def _():
        if final:
            y = _rms(x_ref[...] + 0.5 * seg_mods[0][2][...] * acc_ref[...], fn_ref[...])

            @pl.when(ctx)
            def _():
                oc_ref[...] = y

            @pl.when(jnp.logical_not(ctx))
            def _():
                ol_ref[...] = y
        else:
            for s, (_, _, gt_ref) in enumerate(seg_mods):
                rows = slice(s * TM, (s + 1) * TM)
                o_ref[rows, :] = x_ref[rows, :] + 0.5 * gt_ref[...] * acc_ref[rows, :]


def _ffn(x, mods, layer, sub, norm_w, w_gate, w_up, w_down, h=None, final_norm=None):
    first = isinstance(x, tuple)
    prenormed = h is not None
    final = final_norm is not None
    n_seg = 1 if (first or final) else 2
    tm = n_seg * TM
    row = lambda width: pl.BlockSpec((tm, width), lambda i, j: (i, 0))
    if first:
        in_specs = [_ctx_rows(TM, D_MODEL), _lat_rows(TM, D_MODEL),
                    pl.BlockSpec((TM, D_MODEL), lambda i, j: (0, 0))]
        args = list(x)
    else:
        in_specs, args = [row(D_MODEL)], [x]
    if prenormed:
        in_specs.append(row(D_MODEL))
        args.append(h)
    else:
        in_specs.append(_layer_vec(layer))
        args.append(norm_w.reshape(DEPTH, 1, D_MODEL))
    for seg in range(n_seg):
        whiches = (3 * sub + 2,) if prenormed else (3 * sub, 3 * sub + 1, 3 * sub + 2)
        in_specs += [_mod_spec(layer, k, TM, n_seg, seg) for k in whiches]
        args += [mods] * len(whiches)
    in_specs += [pl.BlockSpec((None, D_MODEL, TF), lambda i, j: (layer, 0, j)),
                 pl.BlockSpec((None, D_MODEL, TF), lambda i, j: (layer, 0, j)),
                 pl.BlockSpec((None, TF, D_MODEL), lambda i, j: (layer, j, 0))]
    args += [w_gate, w_up, w_down]
    scratch = []
    if not prenormed:
        scratch.append(pltpu.VMEM((tm, D_MODEL), BF16))
    if final or not prenormed:
        scratch.append(pltpu.VMEM((tm, D_MODEL), F32))
    if first:
        scratch.append(pltpu.VMEM((TM, D_MODEL), F32))
    if final:
        in_specs.append(pl.BlockSpec((1, D_MODEL), lambda i, j: (0, 0)))
        args.append(final_norm[None])
        out_specs = [_ctx_rows(TM, D_MODEL), _lat_rows(TM, D_MODEL)]
        out_shape = [jax.ShapeDtypeStruct((CTX_TOK, D_MODEL), F32), jax.ShapeDtypeStruct((LAT_TOK, D_MODEL), F32)]
    else:
        out_specs = row(D_MODEL)
        out_shape = jax.ShapeDtypeStruct((N_TOK, D_MODEL), F32)
    return pl.pallas_call(
        functools.partial(_ffn_kernel, first=first, prenormed=prenormed, final=final, n_seg=n_seg),
        grid=(N_TOK // tm, D_FF // TF),
        in_specs=in_specs,
        out_specs=out_specs,
        out_shape=out_shape,
        scratch_shapes=scratch,
        compiler_params=_params("arbitrary", "arbitrary"),
        name=f"ffn{sub}_{layer}",
    )(*args)


def _mixout_kernel(*refs, counts, split):
    refs = list(refs)
    take = lambda k: [refs.pop(0) for _ in range(k)]
    (x_ref,) = take(1)
    groups = [take(c) for c in counts]
    g2_ref, w_ref, nw_ref, sh_ref, sc_ref, xo_ref, ho_ref, wb_scr = refs
    ctx = pl.program_id(0) < CTX_TILES

    @pl.when(pl.program_id(0) == 0)
    def _():
        wb_scr[...] = w_ref[...].astype(BF16)

    def pick(rs):
        return rs[0][...] if len(rs) == 1 else jnp.where(ctx, rs[0][...], rs[1][...])

    mix = (jnp.dot(pick(groups[0]), wb_scr[0:split, :], preferred_element_type=F32)
           + jnp.dot(pick(groups[1]), wb_scr[split:, :], preferred_element_type=F32))
    x_mid = x_ref[...] + g2_ref[...] * mix
    xo_ref[...] = x_mid
    ho_ref[...] = _norm_mod(x_mid, nw_ref[...], sc_ref[...], sh_ref[...])


def _mixout(x, ya, yb, mods, layer, w_out, w_index, norm_w):
    row = lambda width: pl.BlockSpec((TM, width), lambda i: (i, 0))
    in_specs, args, counts = [row(D_MODEL)], [x], []
    for y in (ya, yb):
        parts = y if isinstance(y, tuple) else (y,)
        width = parts[0].shape[1]
        in_specs += [row(width)] if len(parts) == 1 else [_ctx_rows(TM, width), _lat_rows(TM, width)]
        args += list(parts)
        counts.append(len(parts))
    split = args[1].shape[1]
    in_specs += [_mod_spec(layer, 5),
                 pl.BlockSpec((None, D_MODEL, D_MODEL), lambda i: (w_index, 0, 0), pipeline_mode=pl.Buffered(1)),
                 _layer_vec(layer), _mod_spec(layer, 6), _mod_spec(layer, 7)]
    args += [mods, w_out, norm_w.reshape(DEPTH, 1, D_MODEL), mods, mods]
    return pl.pallas_call(
        functools.partial(_mixout_kernel, counts=tuple(counts), split=split),
        grid=(N_ROW_TILES,),
        in_specs=in_specs,
        out_specs=[row(D_MODEL), row(D_MODEL)],
        out_shape=[jax.ShapeDtypeStruct((N_TOK, D_MODEL), F32), jax.ShapeDtypeStruct((N_TOK, D_MODEL), BF16)],
        scratch_shapes=[pltpu.VMEM((D_MODEL, D_MODEL), BF16)],
        compiler_params=_params("arbitrary"),
        name=f"mixout_{layer}",
    )(*args)


def _mixin_kernel(x_ref, nw_ref, sh_ref, sc_ref, w_ref, *rest, segments, transposed):
    o_refs, wb_scr = rest[:-1], rest[-1]

    @pl.when(pl.program_id(0) == 0)
    def _():
        wb_scr[...] = w_ref[...].astype(BF16)

    h = _norm_mod(x_ref[...], nw_ref[...], sc_ref[...], sh_ref[...])
    for (lo, hi), o_ref in zip(segments, o_refs):
        if transposed:
            o_ref[...] = _dot_nt(h, wb_scr[lo:hi, :])
        else:
            o_ref[...] = _dot(h, wb_scr[:, lo:hi])
    if transposed:
        lo, hi = segments[-1]
        o_refs[-1][...] = _dot_nt(wb_scr[lo:hi, :], h)


def _mixin(x, mods, layer, norm_w, w_in, w_index, segments, tm, transposed=False):
    shape = w_in.shape[1:]
    out_specs = [pl.BlockSpec((tm, hi - lo), lambda i: (i, 0)) for lo, hi in segments]
    out_shape = [jax.ShapeDtypeStruct((N_TOK, hi - lo), F32) for lo, hi in segments]
    if transposed:
        lo, hi = segments[-1]
        out_specs.append(pl.BlockSpec((hi - lo, tm), lambda i: (0, i)))
        out_shape.append(jax.ShapeDtypeStruct((hi - lo, N_TOK), F32))
    return pl.pallas_call(
        functools.partial(_mixin_kernel, segments=segments, transposed=transposed),
        grid=(N_TOK // tm,),
        in_specs=[pl.BlockSpec((tm, D_MODEL), lambda i: (i, 0)),
                  _layer_vec(layer), _mod_spec(layer, 3, tm), _mod_spec(layer, 4, tm),
                  pl.BlockSpec((None,) + shape, lambda i: (w_index, 0, 0), pipeline_mode=pl.Buffered(1))],
        out_specs=out_specs,
        out_shape=out_shape,
        scratch_shapes=[pltpu.VMEM(shape, BF16)],
        compiler_params=_params("arbitrary"),
        name=f"mixin_{layer}",
    )(x, norm_w.reshape(DEPTH, 1, D_MODEL), mods, mods, w_in)


def _pool_kernel(p_ref, w_ref, scale_ref, o_ref):
    seq = jnp.where(pl.program_id(0) < CTX_TILES, SEQ, DEC_SEQ)
    p = p_ref[...]
    pos = lax.broadcasted_iota(jnp.int32, (TM, A_W), 0) & (seq - 1)

    def before(a, s):
        return jnp.where(pos >= s, pltpu.roll(a, s, axis=0), 0.0)

    def after(a, s):
        return jnp.where(pos < seq - s, pltpu.roll(a, TM - s, axis=0), 0.0)

    sums = []
    f, b = p, before(p, 1)
    for h in (size // 2 for size in POOL_SIZES):
        if h > 1:
            f = f + after(f, h // 2)
            b = b + before(b, h // 2)
        sums.append(f + b)
    group = lax.broadcasted_iota(jnp.int32, (TM, A_W), 1) >> (POOL_CH.bit_length() - 1)
    wsum = jnp.where(group == 0, sums[0], jnp.where(group == 1, sums[1], jnp.where(group == 2, sums[2], sums[3])))
    half = jnp.left_shift(1, group)
    cnt = (jnp.minimum(pos + half, seq) - jnp.maximum(pos - half, 0)).astype(F32)
    d = wsum / cnt - p
    o_ref[...] = (_dot(d, w_ref[...]) * scale_ref[...]).astype(o_ref.dtype)


def _pool(p_pool, w_bd, scale):
    return pl.pallas_call(
        _pool_kernel,
        grid=(N_ROW_TILES,),
        in_specs=[pl.BlockSpec((TM, A_W), lambda i: (i, 0)),
                  pl.BlockSpec((A_W, A_W), lambda i: (0, 0)),
                  pl.BlockSpec((1, A_W), lambda i: (0, 0))],
        out_specs=pl.BlockSpec((TM, A_W), lambda i: (i, 0)),
        out_shape=jax.ShapeDtypeStruct((N_TOK, A_W), BF16),
        compiler_params=_params("arbitrary"),
        name="pool",
    )(p_pool, w_bd, scale)


def _link_mask(r, c, s, upper):
    lg = s.bit_length() - 1
    same = (r >> (lg + 1)) == (c >> (lg + 1))
    r_half = (r >> lg) & 1
    c_half = (c >> lg) & 1
    return same & ((r_half == 0) & (c_half == 1) if upper else (r_half == 1) & (c_half == 0))


def _dn_kernel(q_ref, k_ref, v_ref, z_ref, g_ref, gt_ref, cq_ref, ck_ref, cv_ref, alog_ref, dtb_ref,
               alog_c_ref, dtb_c_ref, nw_ref, *rest, seq, heads, zero_init, state_slot):
    rest = list(rest)
    s0_ref = None if zero_init else rest.pop(0)
    if state_slot:
        rest.pop(0)
    o_ref = rest.pop(0)
    st_ref = rest.pop(0) if state_slot is not None else None
    gct_scr = rest.pop(0)
    h0 = pl.program_id(1) * heads
    n = DN_CHUNK
    n_blk = seq // n
    row = lax.broadcasted_iota(jnp.int32, (seq, LANES), 0)
    g_lane = lax.broadcasted_iota(jnp.int32, (seq, GATE_W), 1)
    r2 = lax.broadcasted_iota(jnp.int32, (n, n), 0)
    c2 = lax.broadcasted_iota(jnp.int32, (n, n), 1)
    eye = (r2 == c2).astype(F32)
    incl = (r2 >= c2, r2 <= c2)
    strict = (r2 > c2, r2 < c2)
    levels = [1 << b for b in range(n.bit_length() - 1)]
    link = {(s, d): _link_mask(r2, c2, s, d == 1) for s in levels for d in (0, 1)}
    units = [(hh, blk, d) for hh in range(heads) for blk in range(n_blk) for d in (0, 1)]

    def conv_silu(x, cw):
        prev = jnp.where(row >= 1, pltpu.roll(x, 1, axis=0), 0.0)
        nxt = jnp.where(row <= seq - 2, pltpu.roll(x, seq - 1, axis=0), 0.0)
        return _silu(prev * cw[0:1] + x * cw[1:2] + nxt * cw[2:3])

    def l2n(x):
        return x * lax.rsqrt(jnp.sum(x * x, axis=-1, keepdims=True) + EPS)

    def col(a, idx):
        return jnp.sum(jnp.where(g_lane == idx, a, 0.0), axis=1, keepdims=True)

    def log_decay(a, alog, dtb):
        xg = a + dtb
        return -jnp.exp(alog) * (jnp.maximum(xg, 0.0) + jnp.log1p(jnp.exp(-jnp.abs(xg))))

    gates = g_ref[...]
    beta_all = jax.nn.sigmoid(gates)
    g_all = log_decay(gates, alog_ref[...], dtb_ref[...])
    g_all_t = log_decay(gt_ref[...], alog_c_ref[...], dtb_c_ref[...])
    lower = incl[0].astype(BF16)
    upper = incl[1].astype(BF16)
    backward_rows = lax.broadcasted_iota(jnp.int32, (GATE_W, n), 0) >= 3 * DN_HEADS

    def tri_sums(a, left):
        rem, pre_sum, suf_sum = a, 0.0, 0.0
        for _ in range(3):
            piece = rem.astype(BF16)
            rem = rem - piece.astype(F32)
            if left:
                pre_sum = pre_sum + jnp.dot(lower, piece, preferred_element_type=F32)
                suf_sum = suf_sum + jnp.dot(upper, piece, preferred_element_type=F32)
            else:
                pre_sum = pre_sum + jnp.dot(piece, upper, preferred_element_type=F32)
                suf_sum = suf_sum + jnp.dot(piece, lower, preferred_element_type=F32)
        return pre_sum, suf_sum

    pre_blocks, suf_blocks = [], []
    for blk in range(n_blk):
        rs = slice(blk * n, (blk + 1) * n)
        p, s_ = tri_sums(g_all[rs, :], left=True)
        pre_blocks.append(p)
        suf_blocks.append(s_)
        p_t, s_t = tri_sums(g_all_t[:, rs], left=False)
        sums = jnp.where(backward_rows, s_t, p_t)
        for r in range(2 * DN_HEADS):
            gct_scr[r, :, rs] = sums[2 * DN_HEADS + r:2 * DN_HEADS + r + 1, :]
    pre = jnp.concatenate(pre_blocks, axis=0)
    suf = jnp.concatenate(suf_blocks, axis=0)

    qs, kn, vv, beta, gc = [], [], [], [], []
    for hh in range(heads):
        cs = slice(hh * LANES, (hh + 1) * LANES)
        qs.append(l2n(conv_silu(q_ref[:, cs], cq_ref[:, cs])) * (DN_DK ** -0.5))
        kn.append(l2n(conv_silu(k_ref[:, cs], ck_ref[:, cs])))
        vv.append(conv_silu(v_ref[:, cs], cv_ref[:, cs]))
        beta.append((col(beta_all, h0 + hh), col(beta_all, DN_HEADS + h0 + hh)))
        gc.append((col(pre, 2 * DN_HEADS + h0 + hh), col(suf, 3 * DN_HEADS + h0 + hh)))

    kk, qk = {}, {}
    for hh in range(heads):
        for blk in range(n_blk):
            rs = slice(blk * n, (blk + 1) * n)
            kk[hh, blk] = _dot_nt(kn[hh][rs], kn[hh][rs])
            qk[hh, blk] = _dot_nt(qs[hh][rs], kn[hh][rs])

    m, m_hi, t, a_in = {}, {}, {}, {}
    for u in units:
        hh, blk, d = u
        rs = slice(blk * n, (blk + 1) * n)
        g_lanes = gct_scr[d * DN_HEADS + h0 + hh, :, rs]
        decay = jnp.where(incl[d], jnp.exp(gc[hh][d][rs] - g_lanes), 0.0)
        m[u] = jnp.where(strict[d], beta[hh][d][rs] * kk[hh, blk] * decay, 0.0)
        m_hi[u] = m[u].astype(BF16)
        a_in[u] = (qk[hh, blk] * decay).astype(BF16)
        t[u] = eye - jnp.where(link[1, d], m[u], 0.0)

    for s in levels[1:]:
        tb, x = {}, {}
        for u in units:
            tb[u] = t[u].astype(BF16)
            x[u] = jnp.where(link[s, u[2]], jnp.dot(m_hi[u], tb[u], preferred_element_type=F32), 0.0)
        for u in units:
            t[u] = t[u] - jnp.dot(tb[u], x[u].astype(BF16), preferred_element_type=F32)

    uw, q_dec, k_dec, g_last = {}, {}, {}, {}
    for u in units:
        hh, blk, d = u
        rs = slice(blk * n, (blk + 1) * n)
        gcol = gc[hh][d][rs]
        bcol = beta[hh][d][rs]
        e_g = jnp.exp(gcol)
        kb = kn[hh][rs]
        uw[u] = _dot(t[u], jnp.concatenate([vv[hh][rs] * bcol, kb * (bcol * e_g)], axis=1))
        g_last[u] = gcol[n - 1:n] if d == 0 else gcol[0:1]
        q_dec[u] = qs[hh][rs] * e_g
        k_dec[u] = kb * jnp.exp(g_last[u] - gcol)

    state = {}
    for hh in range(heads):
        for d in (0, 1):
            state[hh, d] = jnp.zeros((DN_DK, DN_DK), F32) if zero_init else s0_ref[d, hh]
    outs = {}
    for step in range(n_blk):
        chains = [(hh, step if d == 0 else n_blk - 1 - step, d) for hh in range(heads) for d in (0, 1)]
        ws_qs = {}
        for u in chains:
            ws_qs[u] = _dot(jnp.concatenate([uw[u][:, LANES:], q_dec[u]], axis=0), state[u[0], u[2]])
        for u in chains:
            v_new = (uw[u][:, :LANES] - ws_qs[u][0:n]).astype(BF16)
            outs[u] = ws_qs[u][n:] + jnp.dot(a_in[u], v_new, preferred_element_type=F32)
            state[u[0], u[2]] = state[u[0], u[2]] * jnp.exp(g_last[u]) + _dot_tn(k_dec[u], v_new)

    for hh in range(heads):
        cs = slice(hh * LANES, (hh + 1) * LANES)
        o = (jnp.concatenate([outs[hh, blk, 0] for blk in range(n_blk)], axis=0)
             + jnp.concatenate([outs[hh, blk, 1] for blk in range(n_blk)], axis=0))
        o = o * lax.rsqrt(jnp.mean(o * o, axis=-1, keepdims=True) + EPS) * nw_ref[...]
        o_ref[:, cs] = (o * _silu(z_ref[:, cs])).astype(o_ref.dtype)
        if state_slot == 0:
            st_ref[0, 0, hh] = state[hh, 0]
            st_ref[0, 1, hh] = state[hh, 1]
            st_ref[1:, :, hh] = jnp.zeros((N_EVEN - 1, 2, DN_DK, DN_DK), F32)
        elif state_slot:
            st_ref[0, hh] = state[hh, 0]
            st_ref[1, hh] = state[hh, 1]


def _deltanet(qkvz, gates, gates_t, conv_w, e, alog, dtb, norm_w, s0, seq, heads, n_seq, row_block0,
              state_slot=None, states=None):
    zero_init = s0 is None
    width = heads * LANES
    per = DN_HEADS // heads
    alog_row = jnp.concatenate([jnp.zeros((2 * DN_HEADS,), F32), alog.reshape(-1)])[None]
    dtb_row = jnp.concatenate([jnp.zeros((2 * DN_HEADS,), F32), dtb.reshape(-1)])[None]

    def pcol(cb):
        return pl.BlockSpec((seq, width), lambda b, h: (row_block0 + b, cb * per + h))

    def ccol(cb):
        return pl.BlockSpec((None, 3, width), lambda b, h: (e, 0, cb * per + h))

    grow = pl.BlockSpec((1, GATE_W), lambda b, h: (0, 0))
    gcol = pl.BlockSpec((GATE_W, 1), lambda b, h: (0, 0))
    in_specs = [pcol(COL_Q), pcol(COL_K), pcol(COL_V), pcol(COL_Z),
                pl.BlockSpec((seq, GATE_W), lambda b, h: (row_block0 + b, 0)),
                pl.BlockSpec((GATE_W, seq), lambda b, h: (0, row_block0 + b)),
                ccol(0), ccol(1), ccol(2), grow, grow, gcol, gcol,
                pl.BlockSpec((None, 1, LANES), lambda b, h: (e, 0, 0))]
    args = [qkvz, qkvz, qkvz, qkvz, gates, gates_t, conv_w, conv_w, conv_w, alog_row, dtb_row,
            alog_row.T, dtb_row.T, norm_w.reshape(N_EVEN, 1, LANES)]
    if not zero_init:
        in_specs.append(pl.BlockSpec((None, None, 2, heads, DN_DK, DN_DK), lambda b, h: (b, e, 0, h, 0, 0)))
        args.append(s0)
    aliases = {}
    if state_slot:
        aliases = {len(args): 1}
        in_specs.append(pl.BlockSpec(memory_space=pl.ANY))
        args.append(states)
    out_specs = [pl.BlockSpec((seq, width), lambda b, h: (b, h))]
    out_shape = [jax.ShapeDtypeStruct((n_seq * seq, B_W), BF16)]
    if state_slot is not None:
        if state_slot == 0:
            out_specs.append(pl.BlockSpec((None, N_EVEN, 2, heads, DN_DK, DN_DK), lambda b, h: (b, 0, 0, h, 0, 0)))
        else:
            out_specs.append(pl.BlockSpec((None, None, 2, heads, DN_DK, DN_DK),
                                          lambda b, h: (b, state_slot, 0, h, 0, 0)))
        out_shape.append(jax.ShapeDtypeStruct((n_seq, N_EVEN, 2, DN_HEADS, DN_DK, DN_DK), F32))
    res = pl.pallas_call(
        functools.partial(_dn_kernel, seq=seq, heads=heads, zero_init=zero_init, state_slot=state_slot),
        grid=(n_seq, per),
        in_specs=in_specs,
        out_specs=out_specs,
        out_shape=out_shape,
        scratch_shapes=[pltpu.VMEM((2 * DN_HEADS, 1, seq), F32)],
        input_output_aliases=aliases,
        compiler_params=_params("arbitrary", "arbitrary"),
        name=f"deltanet_{seq}",
    )(*args)
    return res if state_slot is not None else (res[0], None)


TS = 512


def _sgu_kernel(u_ref, v_ref, nw_ref, ws_ref, bt_ref, o_ref):
    u = jax.nn.gelu(u_ref[...])
    v = jax.nn.gelu(v_ref[...])
    mu = jnp.mean(v, axis=-1, keepdims=True)
    vc = v - mu
    var = jnp.mean(vc * vc, axis=-1, keepdims=True)
    vn = vc * lax.rsqrt(var + EPS) * nw_ref[...]
    bt = bt_ref[...]
    for c in range(TS // SGU_CHUNK):
        rs = slice(c * SGU_CHUNK, (c + 1) * SGU_CHUNK)
        for hd in range(SGU_HEADS):
            cs = slice(hd * LANES, (hd + 1) * LANES)
            mixed = _dot(ws_ref[hd], vn[rs, cs]) + bt[:, hd:hd + 1]
            o_ref[rs, cs] = (u[rs, cs] * mixed).astype(o_ref.dtype)


def _sgu(uv, norm_w, w_s, b_t):
    return pl.pallas_call(
        _sgu_kernel,
        grid=(N_TOK // TS,),
        in_specs=[pl.BlockSpec((TS, C_W), lambda i: (i, 0)),
                  pl.BlockSpec((TS, C_W), lambda i: (i, 1)),
                  pl.BlockSpec((1, C_W), lambda i: (0, 0)),
                  pl.BlockSpec((SGU_HEADS, SGU_CHUNK, SGU_CHUNK), lambda i: (0, 0, 0)),
                  pl.BlockSpec((SGU_CHUNK, SGU_HEADS), lambda i: (0, 0))],
        out_specs=pl.BlockSpec((TS, C_W), lambda i: (i, 0)),
        out_shape=jax.ShapeDtypeStruct((N_TOK, C_W), BF16),
        compiler_params=_params("arbitrary"),
        name="sgu",
    )(uv, uv, norm_w, w_s, b_t)


def _fnet_kernel(f_ref, ch_hi_ref, ch_lo_ref, ctx_hi_ref, ctx_lo_ref, lat_hi_ref, lat_lo_ref, w_ref, o_ref):
    fcs = _dot3(_split(f_ref[...]), (ch_hi_ref[...], ch_lo_ref[...]))
    wb = w_ref[...].astype(BF16)
    ctx = pl.program_id(0) < CTX_TILES

    def finish(rows, hi_ref, lo_ref):
        stacked = jnp.concatenate([fcs[rows, :F_W], fcs[rows, F_W:]], axis=0)
        spec = _dot3((hi_ref[...], lo_ref[...]), _split(stacked))
        o_ref[rows, :] = jnp.dot(spec.astype(BF16), wb, preferred_element_type=F32).astype(o_ref.dtype)

    @pl.when(ctx)
    def _():
        for s in range(TM // SEQ):
            finish(slice(s * SEQ, (s + 1) * SEQ), ctx_hi_ref, ctx_lo_ref)

    @pl.when(jnp.logical_not(ctx))
    def _():
        finish(slice(0, DEC_SEQ), lat_hi_ref, lat_lo_ref)


def _fnet(f_all, tables, w_bd):
    const = lambda a: pl.BlockSpec(a.shape, lambda i: (0, 0))
    return pl.pallas_call(
        _fnet_kernel,
        grid=(N_ROW_TILES,),
        in_specs=[pl.BlockSpec((TM, F_W), lambda i: (i, 0))] + [const(t) for t in tables] + [const(w_bd)],
        out_specs=pl.BlockSpec((TM, F_W), lambda i: (i, 0)),
        out_shape=jax.ShapeDtypeStruct((N_TOK, F_W), BF16),
        compiler_params=_params("arbitrary"),
        name="fnet",
    )(f_all, *tables, w_bd)


def _grid_pos_embed(n_tok):
    rows = n_tok // GRID_W
    r = np.repeat(np.arange(rows, dtype=np.float64), GRID_W)
    col = np.tile(np.arange(GRID_W, dtype=np.float64), rows)
    quarter = D_MODEL // 4
    freq = np.exp(-math.log(10000.0) * np.arange(quarter, dtype=np.float64) / quarter)

    def emb(p):
        ang = p[:, None] * freq[None, :]
        return np.concatenate([np.sin(ang), np.cos(ang)], axis=-1)

    return np.concatenate([emb(r), emb(col)], axis=-1).astype(np.float32)


def _dft_tables(n):
    k = np.arange(n, dtype=np.int64)
    ang = ((k[:, None] * k[None, :]) % n).astype(np.float64) * (2.0 * math.pi / n)
    scale = n ** -0.5
    return np.cos(ang) * scale, np.sin(ang) * scale


def _host_split(a):
    hi = a.astype(np.float32).astype(BF16)
    lo = (a - hi.astype(np.float64)).astype(np.float32).astype(BF16)
    return hi, lo


def _fnet_tables():
    cc, sc = _dft_tables(FN_CH)
    eye = np.eye(F_W // FN_CH)
    tables = list(_host_split(np.concatenate([np.kron(eye, cc), np.kron(eye, sc)], axis=1)))
    for n in (SEQ, DEC_SEQ):
        cn, sn = _dft_tables(n)
        tables += list(_host_split(np.concatenate([cn, -sn], axis=1)))
    return tables


def _block_diag(blocks):
    g, a, b = blocks.shape
    eye = jnp.eye(g, dtype=blocks.dtype)
    return (eye[:, None, :, None] * blocks[:, :, None, :]).reshape(g * a, g * b)


def kernel(x_prompt, x_sample, state_delta, c, c_ctx, ffn1_norm, ffn1_w_gate, ffn1_w_up, ffn1_w_down,
           mix_norm, ffn2_norm, ffn2_w_gate, ffn2_w_up, ffn2_w_down, ada_w, ada_b, ev_w_in, ev_w_out,
           pool_w, pool_scale, dn_conv_w, dn_a_log, dn_dt_bias, dn_norm_w, od_w_in, od_w_out, sgu_norm,
           sgu_w, sgu_b, fnet_w, final_norm):
    cond8 = jnp.zeros((SUBLANES, D_MODEL), F32).at[0].set(c_ctx).at[1:1 + DEC_BATCH].set(c)
    mods = _ada(cond8, ada_w, ada_b)
    mods = mods[:, :1 + DEC_BATCH].reshape(DEPTH, 1 + DEC_BATCH, N_MOD, 1, D_MODEL).transpose(0, 2, 1, 3, 4)

    fnet_tables = _fnet_tables()
    groups = ((SEQ, BATCH, 0, DN_HEADS), (DEC_SEQ, DEC_BATCH, CTX_TOK // DEC_SEQ, 3))
    even_segments = ((A_W, A_W + 4 * QK_W), (0, A_W), (A_W + 4 * QK_W, P_EVEN))
    odd_segments = ((0, 2 * C_W), (2 * C_W, P_ODD))

    x = (x_prompt.reshape(CTX_TOK, D_MODEL), x_sample.reshape(LAT_TOK, D_MODEL), _grid_pos_embed(DEC_SEQ))
    ev_w_in_t = jnp.swapaxes(ev_w_in, 1, 2)
    states = None
    for layer in range(DEPTH):
        x = _ffn(x, mods, layer, 0, ffn1_norm, ffn1_w_gate, ffn1_w_up, ffn1_w_down)
        if layer % 2 == 0:
            e = layer // 2
            qkvz, p_pool, gates, gates_t = _mixin(x, mods, layer, mix_norm, ev_w_in_t, e, even_segments, 512,
                                                  transposed=True)
            w_bd = _block_diag(pool_w[e])
            ya = _pool(p_pool, w_bd, pool_scale[e][None])
            yb = []
            for seq, n_seq, rb0, dn_heads in groups:
                ctx = rb0 == 0
                o, st = _deltanet(qkvz, gates, gates_t, dn_conv_w, e, dn_a_log[e], dn_dt_bias[e], dn_norm_w,
                                  None if ctx else state_delta, seq, dn_heads, n_seq, rb0,
                                  state_slot=e if ctx else None, states=states)
                yb.append(o)
                if ctx:
                    states = st
            x, h = _mixout(x, ya, tuple(yb), mods, layer, ev_w_out, e, ffn2_norm)
        else:
            j = layer // 2
            uv, f_all = _mixin(x, mods, layer, mix_norm, od_w_in, j, odd_segments, TM)
            yc = _sgu(uv, sgu_norm[j][None], sgu_w[j], sgu_b[j].T)
            w_bd = _block_diag(fnet_w[j])
            yd = _fnet(f_all, fnet_tables, w_bd)
            x, h = _mixout(x, yc, yd, mods, layer, od_w_out, j, ffn2_norm)
        x = _ffn(x, mods, layer, 2, ffn2_norm, ffn2_w_gate, ffn2_w_up, ffn2_w_down, h=h,
                 final_norm=final_norm if layer == DEPTH - 1 else None)

    y_prompt, y_sample = x
    return (y_prompt.reshape(BATCH, SEQ, D_MODEL), y_sample.reshape(DEC_BATCH, DEC_SEQ, D_MODEL), states)
```

```python
import functools
import math

import numpy as np
import jax
import jax.numpy as jnp
from jax import lax
from jax.experimental import pallas as pl
from jax.experimental.pallas import tpu as pltpu

F32 = jnp.float32
BF16 = jnp.bfloat16

D_MODEL = 1024
BATCH = 16
SEQ = 256
DEPTH = 4
DEC_BATCH = 2
DEC_SEQ = 1024
GRID_W = 64
EPS = 1e-6
D_FF = 2816
N_MOD = 9
N_EVEN = (DEPTH + 1) // 2
POOL_SIZES = (2, 4, 8, 16)
POOL_CH = 64
A_W = 256
DN_HEADS = 6
DN_DK = 128
QK_W = DN_HEADS * DN_DK
B_W = QK_W
GATE_W = 4 * DN_HEADS
P_EVEN = A_W + 4 * QK_W + GATE_W
SGU_CHUNK = 128
SGU_HEADS = 6
C_W = 768
FN_CH = 64
F_W = 256
P_ODD = 2 * C_W + F_W

CTX_TOK = BATCH * SEQ
LAT_TOK = DEC_BATCH * DEC_SEQ
N_TOK = CTX_TOK + LAT_TOK

LANES = 128
SUBLANES = 8
VMEM_LIMIT = 56 * 1024 * 1024

TM = 1024
N_ROW_TILES = N_TOK // TM
CTX_TILES = CTX_TOK // TM
TF = 256
DN_CHUNK = 128
COL_Q, COL_K, COL_V, COL_Z = 0, 1, 2, 3


def _cond_of_tile(i, tm):
    return jnp.where(i < CTX_TOK // tm, 0, 1 + (i - CTX_TOK // tm) * tm // DEC_SEQ)


def _silu(x):
    return x * jax.nn.sigmoid(x)


def _dot(a, b):
    return jnp.dot(a.astype(BF16), b.astype(BF16), preferred_element_type=F32)


def _dot_nt(a, b):
    return lax.dot_general(a.astype(BF16), b.astype(BF16), (((1,), (1,)), ((), ())),
                           preferred_element_type=F32)


def _dot_tn(a, b):
    return lax.dot_general(a.astype(BF16), b.astype(BF16), (((0,), (0,)), ((), ())),
                           preferred_element_type=F32)


def _split(a):
    hi = a.astype(BF16)
    lo = (a - hi.astype(F32)).astype(BF16)
    return hi, lo


def _dot3(a, b):
    ah, al = a
    bh, bl = b
    return (jnp.dot(ah, bh, preferred_element_type=F32)
            + (jnp.dot(al, bh, preferred_element_type=F32)
               + jnp.dot(ah, bl, preferred_element_type=F32)))


def _rms(x, w):
    return x * lax.rsqrt(jnp.mean(x * x, axis=-1, keepdims=True) + EPS) * w


def _norm_mod(x, nw, scale, shift):
    r = lax.rsqrt(jnp.mean(x * x, axis=-1, keepdims=True) + EPS)
    return ((x * r) * (nw * (1.0 + scale)) + shift).astype(BF16)


def _params(*sem):
    return pltpu.CompilerParams(dimension_semantics=sem, vmem_limit_bytes=VMEM_LIMIT)


def _ctx_rows(tm, width):
    return pl.BlockSpec((tm, width), lambda i, *_: (jnp.minimum(i, CTX_TOK // tm - 1), 0))


def _lat_rows(tm, width):
    return pl.BlockSpec((tm, width), lambda i, *_: (jnp.maximum(i - CTX_TOK // tm, 0), 0))


def _ada_kernel(c_ref, w_ref, b_ref, o_ref):
    o_ref[0] = _dot(_silu(c_ref[...]), w_ref[0]) + b_ref[0]


def _ada(cond8, ada_w, ada_b):
    tn = 2304
    n = N_MOD * D_MODEL
    return pl.pallas_call(
        _ada_kernel,
        grid=(DEPTH, n // tn),
        in_specs=[pl.BlockSpec((SUBLANES, D_MODEL), lambda l, j: (0, 0)),
                  pl.BlockSpec((1, D_MODEL, tn), lambda l, j: (l, 0, j)),
                  pl.BlockSpec((1, 1, tn), lambda l, j: (l, 0, j))],
        out_specs=pl.BlockSpec((1, SUBLANES, tn), lambda l, j: (l, 0, j)),
        out_shape=jax.ShapeDtypeStruct((DEPTH, SUBLANES, n), F32),
        compiler_params=_params("arbitrary", "arbitrary"),
        name="ada",
    )(cond8, ada_w, ada_b.reshape(DEPTH, 1, n))


def _mod_spec(layer, which, tm=TM, n_seg=1, seg=0):
    return pl.BlockSpec((None, None, None, 1, D_MODEL),
                        lambda i, *_: (layer, which, _cond_of_tile(i * n_seg + seg, tm), 0, 0))


def _layer_vec(layer):
    return pl.BlockSpec((None, 1, D_MODEL), lambda i, *_: (layer, 0, 0))


def _ffn_kernel(*refs, first, mix_counts, mix_split, final, n_seg):
    refs = list(refs)
    take = lambda k: [refs.pop(0) for _ in range(k)]
    if first:
        xp_ref, xs_ref, pos_ref = take(3)
    else:
        (x_in_ref,) = take(1)
    if mix_counts:
        mix_groups = [take(c) for c in mix_counts]
        g2_ref, wo_ref = take(2)
    (nw_ref,) = take(1)
    seg_mods = [take(3) for _ in range(n_seg)]
    wg_ref, wu_ref, wd_ref = take(3)
    if final:
        fn_ref, oc_ref, ol_ref = take(3)
    else:
        (o_ref,) = take(1)
    h_scr, acc_scr = take(2)
    x_ref = refs.pop(0) if (first or mix_counts) else x_in_ref
    wob_scr = refs.pop(0) if mix_counts else None
    i = pl.program_id(0)
    j = pl.program_id(1)
    ctx = i < CTX_TILES

    @pl.when(j == 0)
    def _():
        if first:
            @pl.when(ctx)
            def _():
                x_ref[...] = xp_ref[...]

            @pl.when(jnp.logical_not(ctx))
            def _():
                x_ref[...] = xs_ref[...] + pos_ref[...]
        elif mix_counts:
            @pl.when(i == 0)
            def _():
                wob_scr[...] = wo_ref[...].astype(BF16)

            def pick(rs):
                return rs[0][...] if len(rs) == 1 else jnp.where(ctx, rs[0][...], rs[1][...])

            mix = (_dot(pick(mix_groups[0]), wob_scr[0:mix_split, :])
                   + _dot(pick(mix_groups[1]), wob_scr[mix_split:, :]))
            x_ref[...] = x_in_ref[...] + g2_ref[...] * mix

        for s, (sh_ref, sc_ref, _) in enumerate(seg_mods):
            rows = slice(s * TM, (s + 1) * TM)
            h_scr[rows, :] = _norm_mod(x_ref[rows, :], nw_ref[...], sc_ref[...], sh_ref[...])
        acc_scr[...] = jnp.zeros_like(acc_scr)

    h = h_scr[...]
    g = jnp.dot(h, wg_ref[...].astype(BF16), preferred_element_type=F32)
    u = jnp.dot(h, wu_ref[...].astype(BF16), preferred_element_type=F32)
    acc_scr[...] += _dot(_silu(g) * u, wd_ref[...])

    @pl.when(j == pl.num_programs(1) - 1)
    def _():
        if final:
            y = _rms(x_ref[...] + 0.5 * seg_mods[0][2][...] * acc_scr[...], fn_ref[...])

            @pl.when(ctx)
            def _():
                oc_ref[...] = y

            @pl.when(jnp.logical_not(ctx))
            def _():
                ol_ref[...] = y
        else:
            for s, (_, _, gt_ref) in enumerate(seg_mods):
                rows = slice(s * TM, (s + 1) * TM)
                o_ref[rows, :] = x_ref[rows, :] + 0.5 * gt_ref[...] * acc_scr[rows, :]


def _ffn(x, mods, layer, sub, norm_w, w_gate, w_up, w_down, mix=None, final_norm=None):
    first = isinstance(x, tuple)
    final = final_norm is not None
    n_seg = 2 if (not first and mix is None and not final) else 1
    tm = n_seg * TM
    row = lambda width: pl.BlockSpec((tm, width), lambda i, j: (i, 0))
    if first:
        in_specs = [_ctx_rows(TM, D_MODEL), _lat_rows(TM, D_MODEL),
                    pl.BlockSpec((TM, D_MODEL), lambda i, j: (0, 0))]
        args = list(x)
    else:
        in_specs, args = [row(D_MODEL)], [x]
    mix_counts, mix_split = (), 0
    if mix is not None:
        ya, yb, w_out, w_index = mix
        counts = []
        for y in (ya, yb):
            parts = y if isinstance(y, tuple) else (y,)
            width = parts[0].shape[1]
            in_specs += [row(width)] if len(parts) == 1 else [_ctx_rows(TM, width), _lat_rows(TM, width)]
            args += list(parts)
            counts.append(len(parts))
        mix_counts = tuple(counts)
        mix_split = (ya[0] if isinstance(ya, tuple) else ya).shape[1]
        in_specs += [_mod_spec(layer, 5),
                     pl.BlockSpec((None, D_MODEL, D_MODEL), lambda i, j: (w_index, 0, 0),
                                  pipeline_mode=pl.Buffered(1))]
        args += [mods, w_out]
    in_specs.append(_layer_vec(layer))
    args.append(norm_w.reshape(DEPTH, 1, D_MODEL))
    for seg in range(n_seg):
        in_specs += [_mod_spec(layer, 3 * sub + k, TM, n_seg, seg) for k in range(3)]
        args += [mods, mods, mods]
    in_specs += [pl.BlockSpec((None, D_MODEL, TF), lambda i, j: (layer, 0, j)),
                 pl.BlockSpec((None, D_MODEL, TF), lambda i, j: (layer, 0, j)),
                 pl.BlockSpec((None, TF, D_MODEL), lambda i, j: (layer, j, 0))]
    args += [w_gate, w_up, w_down]
    scratch = [pltpu.VMEM((tm, D_MODEL), BF16), pltpu.VMEM((tm, D_MODEL), F32)]
    if first or mix_counts:
        scratch.append(pltpu.VMEM((TM, D_MODEL), F32))
    if mix_counts:
        scratch.append(pltpu.VMEM((D_MODEL, D_MODEL), BF16))
    if final:
        in_specs.append(pl.BlockSpec((1, D_MODEL), lambda i, j: (0, 0)))
        args.append(final_norm[None])
        out_specs = [_ctx_rows(TM, D_MODEL), _lat_rows(TM, D_MODEL)]
        out_shape = [jax.ShapeDtypeStruct((CTX_TOK, D_MODEL), F32), jax.ShapeDtypeStruct((LAT_TOK, D_MODEL), F32)]
    else:
        out_specs = row(D_MODEL)
        out_shape = jax.ShapeDtypeStruct((N_TOK, D_MODEL), F32)
    return pl.pallas_call(
        functools.partial(_ffn_kernel, first=first, mix_counts=mix_counts, mix_split=mix_split, final=final,
                          n_seg=n_seg),
        grid=(N_TOK // tm, D_FF // TF),
        in_specs=in_specs,
        out_specs=out_specs,
        out_shape=out_shape,
        scratch_shapes=scratch,
        compiler_params=_params("arbitrary", "arbitrary"),
        name=f"ffn{sub}_{layer}",
    )(*args)


def _mixin_kernel(x_ref, nw_ref, sh_ref, sc_ref, w_ref, *rest, segments, transposed):
    o_refs, wb_scr = rest[:-1], rest[-1]

    @pl.when(pl.program_id(0) == 0)
    def _():
        wb_scr[...] = w_ref[...].astype(BF16)

    h = _norm_mod(x_ref[...], nw_ref[...], sc_ref[...], sh_ref[...])
    for (lo, hi), o_ref in zip(segments, o_refs):
        if transposed:
            o_ref[...] = _dot_nt(h, wb_scr[lo:hi, :])
        else:
            o_ref[...] = _dot(h, wb_scr[:, lo:hi])
    if transposed:
        lo, hi = segments[-1]
        o_refs[-1][...] = _dot_nt(wb_scr[lo:hi, :], h)


def _mixin(x, mods, layer, norm_w, w_in, w_index, segments, tm, transposed=False):
    shape = w_in.shape[1:]
    out_specs = [pl.BlockSpec((tm, hi - lo), lambda i: (i, 0)) for lo, hi in segments]
    out_shape = [jax.ShapeDtypeStruct((N_TOK, hi - lo), F32) for lo, hi in segments]
    if transposed:
        lo, hi = segments[-1]
        out_specs.append(pl.BlockSpec((hi - lo, tm), lambda i: (0, i)))
        out_shape.append(jax.ShapeDtypeStruct((hi - lo, N_TOK), F32))
    return pl.pallas_call(
        functools.partial(_mixin_kernel, segments=segments, transposed=transposed),
        grid=(N_TOK // tm,),
        in_specs=[pl.BlockSpec((tm, D_MODEL), lambda i: (i, 0)),
                  _layer_vec(layer), _mod_spec(layer, 3, tm), _mod_spec(layer, 4, tm),
                  pl.BlockSpec((None,) + shape, lambda i: (w_index, 0, 0), pipeline_mode=pl.Buffered(1))],
        out_specs=out_specs,
        out_shape=out_shape,
        scratch_shapes=[pltpu.VMEM(shape, BF16)],
        compiler_params=_params("arbitrary"),
        name=f"mixin_{layer}",
    )(x, norm_w.reshape(DEPTH, 1, D_MODEL), mods, mods, w_in)


def _pool_kernel(p_ref, w_ref, scale_ref, o_ref):
    seq = jnp.where(pl.program_id(0) < CTX_TILES, SEQ, DEC_SEQ)
    p = p_ref[...]
    pos = lax.broadcasted_iota(jnp.int32, (TM, A_W), 0) & (seq - 1)

    def before(a, s):
        return jnp.where(pos >= s, pltpu.roll(a, s, axis=0), 0.0)

    def after(a, s):
        return jnp.where(pos < seq - s, pltpu.roll(a, TM - s, axis=0), 0.0)

    sums = []
    f, b = p, before(p, 1)
    for h in (size // 2 for size in POOL_SIZES):
        if h > 1:
            f = f + after(f, h // 2)
            b = b + before(b, h // 2)
        sums.append(f + b)
    group = lax.broadcasted_iota(jnp.int32, (TM, A_W), 1) >> (POOL_CH.bit_length() - 1)
    wsum = jnp.where(group == 0, sums[0], jnp.where(group == 1, sums[1], jnp.where(group == 2, sums[2], sums[3])))
    half = jnp.left_shift(1, group)
    cnt = (jnp.minimum(pos + half, seq) - jnp.maximum(pos - half, 0)).astype(F32)
    d = wsum / cnt - p
    o_ref[...] = (_dot(d, w_ref[...]) * scale_ref[...]).astype(o_ref.dtype)


def _pool(p_pool, w_bd, scale):
    return pl.pallas_call(
        _pool_kernel,
        grid=(N_ROW_TILES,),
        in_specs=[pl.BlockSpec((TM, A_W), lambda i: (i, 0)),
                  pl.BlockSpec((A_W, A_W), lambda i: (0, 0)),
                  pl.BlockSpec((1, A_W), lambda i: (0, 0))],
        out_specs=pl.BlockSpec((TM, A_W), lambda i: (i, 0)),
        out_shape=jax.ShapeDtypeStruct((N_TOK, A_W), BF16),
        compiler_params=_params("arbitrary"),
        name="pool",
    )(p_pool, w_bd, scale)


def _link_mask(r, c, s, upper):
    lg = s.bit_length() - 1
    same = (r >> (lg + 1)) == (c >> (lg + 1))
    r_half = (r >> lg) & 1
    c_half = (c >> lg) & 1
    return same & ((r_half == 0) & (c_half == 1) if upper else (r_half == 1) & (c_half == 0))


def _dn_kernel(q_ref, k_ref, v_ref, z_ref, g_ref, gt_ref, cq_ref, ck_ref, cv_ref, alog_ref, dtb_ref,
               alog_c_ref, dtb_c_ref, nw_ref, *rest, seq, heads, zero_init, state_slot):
    rest = list(rest)
    s0_ref = None if zero_init else rest.pop(0)
    if state_slot:
        rest.pop(0)
    o_ref = rest.pop(0)
    st_ref = rest.pop(0) if state_slot is not None else None
    gct_scr = rest.pop(0)
    h0 = pl.program_id(1) * heads
    n = DN_CHUNK
    n_blk = seq // n
    row = lax.broadcasted_iota(jnp.int32, (seq, LANES), 0)
    g_lane = lax.broadcasted_iota(jnp.int32, (seq, GATE_W), 1)
    r2 = lax.broadcasted_iota(jnp.int32, (n, n), 0)
    c2 = lax.broadcasted_iota(jnp.int32, (n, n), 1)
    eye = (r2 == c2).astype(F32)
    incl = (r2 >= c2, r2 <= c2)
    strict = (r2 > c2, r2 < c2)
    levels = [1 << b for b in range(n.bit_length() - 1)]
    link = {(s, d): _link_mask(r2, c2, s, d == 1) for s in levels for d in (0, 1)}
    units = [(hh, blk, d) for hh in range(heads) for blk in range(n_blk) for d in (0, 1)]

    def conv_silu(x, cw):
        prev = jnp.where(row >= 1, pltpu.roll(x, 1, axis=0), 0.0)
        nxt = jnp.where(row <= seq - 2, pltpu.roll(x, seq - 1, axis=0), 0.0)
        return _silu(prev * cw[0:1] + x * cw[1:2] + nxt * cw[2:3])

    def l2n(x):
        return x * lax.rsqrt(jnp.sum(x * x, axis=-1, keepdims=True) + EPS)

    def col(a, idx):
        return jnp.sum(jnp.where(g_lane == idx, a, 0.0), axis=1, keepdims=True)

    def log_decay(a, alog, dtb):
        xg = a + dtb
        return -jnp.exp(alog) * (jnp.maximum(xg, 0.0) + jnp.log1p(jnp.exp(-jnp.abs(xg))))

    gates = g_ref[...]
    beta_all = jax.nn.sigmoid(gates)
    g_all = log_decay(gates, alog_ref[...], dtb_ref[...])
    g_all_t = log_decay(gt_ref[...], alog_c_ref[...], dtb_c_ref[...])
    lower = incl[0].astype(BF16)
    upper = incl[1].astype(BF16)
    backward_rows = lax.broadcasted_iota(jnp.int32, (GATE_W, n), 0) >= 3 * DN_HEADS

    def tri_sums(a, left):
        rem, pre_sum, suf_sum = a, 0.0, 0.0
        for _ in range(3):
            piece = rem.astype(BF16)
            rem = rem - piece.astype(F32)
            if left:
                pre_sum = pre_sum + jnp.dot(lower, piece, preferred_element_type=F32)
                suf_sum = suf_sum + jnp.dot(upper, piece, preferred_element_type=F32)
            else:
                pre_sum = pre_sum + jnp.dot(piece, upper, preferred_element_type=F32)
                suf_sum = suf_sum + jnp.dot(piece, lower, preferred_element_type=F32)
        return pre_sum, suf_sum

    pre_blocks, suf_blocks = [], []
    for blk in range(n_blk):
        rs = slice(blk * n, (blk + 1) * n)
        p, s_ = tri_sums(g_all[rs, :], left=True)
        pre_blocks.append(p)
        suf_blocks.append(s_)
        p_t, s_t = tri_sums(g_all_t[:, rs], left=False)
        sums = jnp.where(backward_rows, s_t, p_t)
        for r in range(2 * DN_HEADS):
            gct_scr[r, :, rs] = sums[2 * DN_HEADS + r:2 * DN_HEADS + r + 1, :]
    pre = jnp.concatenate(pre_blocks, axis=0)
    suf = jnp.concatenate(suf_blocks, axis=0)

    qs, kn, vv, beta, gc = [], [], [], [], []
    for hh in range(heads):
        cs = slice(hh * LANES, (hh + 1) * LANES)
        qs.append(l2n(conv_silu(q_ref[:, cs], cq_ref[:, cs])) * (DN_DK ** -0.5))
        kn.append(l2n(conv_silu(k_ref[:, cs], ck_ref[:, cs])))
        vv.append(conv_silu(v_ref[:, cs], cv_ref[:, cs]))
        beta.append((col(beta_all, h0 + hh), col(beta_all, DN_HEADS + h0 + hh)))
        gc.append((col(pre, 2 * DN_HEADS + h0 + hh), col(suf, 3 * DN_HEADS + h0 + hh)))

    kk, qk = {}, {}
    for hh in range(heads):
        for blk in range(n_blk):
            rs = slice(blk * n, (blk + 1) * n)
            kk[hh, blk] = _dot_nt(kn[hh][rs], kn[hh][rs])
            qk[hh, blk] = _dot_nt(qs[hh][rs], kn[hh][rs])

    m, m_hi, t, a_in = {}, {}, {}, {}
    for u in units:
        hh, blk, d = u
        rs = slice(blk * n, (blk + 1) * n)
        g_lanes = gct_scr[d * DN_HEADS + h0 + hh, :, rs]
        decay = jnp.where(incl[d], jnp.exp(gc[hh][d][rs] - g_lanes), 0.0)
        m[u] = jnp.where(strict[d], beta[hh][d][rs] * kk[hh, blk] * decay, 0.0)
        m_hi[u] = m[u].astype(BF16)
        a_in[u] = (qk[hh, blk] * decay).astype(BF16)
        t[u] = eye - jnp.where(link[1, d], m[u], 0.0)

    for s in levels[1:]:
        tb, x = {}, {}
        for u in units:
            tb[u] = t[u].astype(BF16)
            x[u] = jnp.where(link[s, u[2]], jnp.dot(m_hi[u], tb[u], preferred_element_type=F32), 0.0)
        for u in units:
            t[u] = t[u] - jnp.dot(tb[u], x[u].astype(BF16), preferred_element_type=F32)

    uw, q_dec, k_dec, g_last = {}, {}, {}, {}
    for u in units:
        hh, blk, d = u
        rs = slice(blk * n, (blk + 1) * n)
        gcol = gc[hh][d][rs]
        bcol = beta[hh][d][rs]
        e_g = jnp.exp(gcol)
        kb = kn[hh][rs]
        uw[u] = _dot(t[u], jnp.concatenate([vv[hh][rs] * bcol, kb * (bcol * e_g)], axis=1))
        g_last[u] = gcol[n - 1:n] if d == 0 else gcol[0:1]
        q_dec[u] = qs[hh][rs] * e_g
        k_dec[u] = kb * jnp.exp(g_last[u] - gcol)

    state = {}
    for hh in range(heads):
        for d in (0, 1):
            state[hh, d] = jnp.zeros((DN_DK, DN_DK), F32) if zero_init else s0_ref[d, hh]
    outs = {}
    for step in range(n_blk):
        chains = [(hh, step if d == 0 else n_blk - 1 - step, d) for hh in range(heads) for d in (0, 1)]
        ws_qs = {}
        for u in chains:
            ws_qs[u] = _dot(jnp.concatenate([uw[u][:, LANES:], q_dec[u]], axis=0), state[u[0], u[2]])
        for u in chains:
            v_new = (uw[u][:, :LANES] - ws_qs[u][0:n]).astype(BF16)
            outs[u] = ws_qs[u][n:] + jnp.dot(a_in[u], v_new, preferred_element_type=F32)
            state[u[0], u[2]] = state[u[0], u[2]] * jnp.exp(g_last[u]) + _dot_tn(k_dec[u], v_new)

    for hh in range(heads):
        cs = slice(hh * LANES, (hh + 1) * LANES)
        o = (jnp.concatenate([outs[hh, blk, 0] for blk in range(n_blk)], axis=0)
             + jnp.concatenate([outs[hh, blk, 1] for blk in range(n_blk)], axis=0))
        o = o * lax.rsqrt(jnp.mean(o * o, axis=-1, keepdims=True) + EPS) * nw_ref[...]
        o_ref[:, cs] = (o * _silu(z_ref[:, cs])).astype(o_ref.dtype)
        if state_slot == 0:
            st_ref[0, 0, hh] = state[hh, 0]
            st_ref[0, 1, hh] = state[hh, 1]
            st_ref[1:, :, hh] = jnp.zeros((N_EVEN - 1, 2, DN_DK, DN_DK), F32)
        elif state_slot:
            st_ref[0, hh] = state[hh, 0]
            st_ref[1, hh] = state[hh, 1]


def _deltanet(qkvz, gates, gates_t, conv_w, e, alog, dtb, norm_w, s0, seq, heads, n_seq, row_block0,
              state_slot=None, states=None):
    zero_init = s0 is None
    width = heads * LANES
    per = DN_HEADS // heads
    alog_row = jnp.concatenate([jnp.zeros((2 * DN_HEADS,), F32), alog.reshape(-1)])[None]
    dtb_row = jnp.concatenate([jnp.zeros((2 * DN_HEADS,), F32), dtb.reshape(-1)])[None]

    def pcol(cb):
        return pl.BlockSpec((seq, width), lambda b, h: (row_block0 + b, cb * per + h))

    def ccol(cb):
        return pl.BlockSpec((None, 3, width), lambda b, h: (e, 0, cb * per + h))

    grow = pl.BlockSpec((1, GATE_W), lambda b, h: (0, 0))
    gcol = pl.BlockSpec((GATE_W, 1), lambda b, h: (0, 0))
    in_specs = [pcol(COL_Q), pcol(COL_K), pcol(COL_V), pcol(COL_Z),
                pl.BlockSpec((seq, GATE_W), lambda b, h: (row_block0 + b, 0)),
                pl.BlockSpec((GATE_W, seq), lambda b, h: (0, row_block0 + b)),
                ccol(0), ccol(1), ccol(2), grow, grow, gcol, gcol,
                pl.BlockSpec((None, 1, LANES), lambda b, h: (e, 0, 0))]
    args = [qkvz, qkvz, qkvz, qkvz, gates, gates_t, conv_w, conv_w, conv_w, alog_row, dtb_row,
            alog_row.T, dtb_row.T, norm_w.reshape(N_EVEN, 1, LANES)]
    if not zero_init:
        in_specs.append(pl.BlockSpec((None, None, 2, heads, DN_DK, DN_DK), lambda b, h: (b, e, 0, h, 0, 0)))
        args.append(s0)
    aliases = {}
    if state_slot:
        aliases = {len(args): 1}
        in_specs.append(pl.BlockSpec(memory_space=pl.ANY))
        args.append(states)
    out_specs = [pl.BlockSpec((seq, width), lambda b, h: (b, h))]
    out_shape = [jax.ShapeDtypeStruct((n_seq * seq, B_W), BF16)]
    if state_slot is not None:
        if state_slot == 0:
            out_specs.append(pl.BlockSpec((None, N_EVEN, 2, heads, DN_DK, DN_DK), lambda b, h: (b, 0, 0, h, 0, 0)))
        else:
            out_specs.append(pl.BlockSpec((None, None, 2, heads, DN_DK, DN_DK),
                                          lambda b, h: (b, state_slot, 0, h, 0, 0)))
        out_shape.append(jax.ShapeDtypeStruct((n_seq, N_EVEN, 2, DN_HEADS, DN_DK, DN_DK), F32))
    res = pl.pallas_call(
        functools.partial(_dn_kernel, seq=seq, heads=heads, zero_init=zero_init, state_slot=state_slot),
        grid=(n_seq, per),
        in_specs=in_specs,
        out_specs=out_specs,
        out_shape=out_shape,
        scratch_shapes=[pltpu.VMEM((2 * DN_HEADS, 1, seq), F32)],
        input_output_aliases=aliases,
        compiler_params=_params("arbitrary", "arbitrary"),
        name=f"deltanet_{seq}",
    )(*args)
    return res if state_slot is not None else (res[0], None)


TS = 1024


def _sgu_kernel(u_ref, v_ref, nw_ref, ws_ref, bt_ref, o_ref):
    u = jax.nn.gelu(u_ref[...])
    v = jax.nn.gelu(v_ref[...])
    mu = jnp.mean(v, axis=-1, keepdims=True)
    vc = v - mu
    var = jnp.mean(vc * vc, axis=-1, keepdims=True)
    vn = vc * lax.rsqrt(var + EPS) * nw_ref[...]
    bt = bt_ref[...]
    for c in range(TS // SGU_CHUNK):
        rs = slice(c * SGU_CHUNK, (c + 1) * SGU_CHUNK)
        for hd in range(SGU_HEADS):
            cs = slice(hd * LANES, (hd + 1) * LANES)
            mixed = _dot(ws_ref[hd], vn[rs, cs]) + bt[:, hd:hd + 1]
            o_ref[rs, cs] = (u[rs, cs] * mixed).astype(o_ref.dtype)


def _sgu(uv, norm_w, w_s, b_t):
    return pl.pallas_call(
        _sgu_kernel,
        grid=(N_TOK // TS,),
        in_specs=[pl.BlockSpec((TS, C_W), lambda i: (i, 0)),
                  pl.BlockSpec((TS, C_W), lambda i: (i, 1)),
                  pl.BlockSpec((1, C_W), lambda i: (0, 0)),
                  pl.BlockSpec((SGU_HEADS, SGU_CHUNK, SGU_CHUNK), lambda i: (0, 0, 0)),
                  pl.BlockSpec((SGU_CHUNK, SGU_HEADS), lambda i: (0, 0))],
        out_specs=pl.BlockSpec((TS, C_W), lambda i: (i, 0)),
        out_shape=jax.ShapeDtypeStruct((N_TOK, C_W), BF16),
        compiler_params=_params("arbitrary"),
        name="sgu",
    )(uv, uv, norm_w, w_s, b_t)


def _fnet_kernel(f_ref, ch_hi_ref, ch_lo_ref, ctx_hi_ref, ctx_lo_ref, lat_hi_ref, lat_lo_ref, w_ref, o_ref):
    fcs = _dot3(_split(f_ref[...]), (ch_hi_ref[...], ch_lo_ref[...]))
    wb = w_ref[...].astype(BF16)
    ctx = pl.program_id(0) < CTX_TILES

    def finish(rows, hi_ref, lo_ref):
        stacked = jnp.concatenate([fcs[rows, :F_W], fcs[rows, F_W:]], axis=0)
        spec = _dot3((hi_ref[...], lo_ref[...]), _split(stacked))
        o_ref[rows, :] = jnp.dot(spec.astype(BF16), wb, preferred_element_type=F32).astype(o_ref.dtype)

    @pl.when(ctx)
    def _():
        for s in range(TM // SEQ):
            finish(slice(s * SEQ, (s + 1) * SEQ), ctx_hi_ref, ctx_lo_ref)

    @pl.when(jnp.logical_not(ctx))
    def _():
        finish(slice(0, DEC_SEQ), lat_hi_ref, lat_lo_ref)


def _fnet(f_all, tables, w_bd):
    const = lambda a: pl.BlockSpec(a.shape, lambda i: (0, 0))
    return pl.pallas_call(
        _fnet_kernel,
        grid=(N_ROW_TILES,),
        in_specs=[pl.BlockSpec((TM, F_W), lambda i: (i, 0))] + [const(t) for t in tables] + [const(w_bd)],
        out_specs=pl.BlockSpec((TM, F_W), lambda i: (i, 0)),
        out_shape=jax.ShapeDtypeStruct((N_TOK, F_W), BF16),
        compiler_params=_params("arbitrary"),
        name="fnet",
    )(f_all, *tables, w_bd)


def _grid_pos_embed(n_tok):
    rows = n_tok // GRID_W
    r = np.repeat(np.arange(rows, dtype=np.float64), GRID_W)
    col = np.tile(np.arange(GRID_W, dtype=np.float64), rows)
    quarter = D_MODEL // 4
    freq = np.exp(-math.log(10000.0) * np.arange(quarter, dtype=np.float64) / quarter)

    def emb(p):
        ang = p[:, None] * freq[None, :]
        return np.concatenate([np.sin(ang), np.cos(ang)], axis=-1)

    return np.concatenate([emb(r), emb(col)], axis=-1).astype(np.float32)


def _dft_tables(n):
    k = np.arange(n, dtype=np.int64)
    ang = ((k[:, None] * k[None, :]) % n).astype(np.float64) * (2.0 * math.pi / n)
    scale = n ** -0.5
    return np.cos(ang) * scale, np.sin(ang) * scale


def _host_split(a):
    hi = a.astype(np.float32).astype(BF16)
    lo = (a - hi.astype(np.float64)).astype(np.float32).astype(BF16)
    return hi, lo


def _fnet_tables():
    cc, sc = _dft_tables(FN_CH)
    eye = np.eye(F_W // FN_CH)
    tables = list(_host_split(np.concatenate([np.kron(eye, cc), np.kron(eye, sc)], axis=1)))
    for n in (SEQ, DEC_SEQ):
        cn, sn = _dft_tables(n)
        tables += list(_host_split(np.concatenate([cn, -sn], axis=1)))
    return tables


def _block_diag(blocks):
    g, a, b = blocks.shape
    eye = jnp.eye(g, dtype=blocks.dtype)
    return (eye[:, None, :, None] * blocks[:, :, None, :]).reshape(g * a, g * b)


def kernel(x_prompt, x_sample, state_delta, c, c_ctx, ffn1_norm, ffn1_w_gate, ffn1_w_up, ffn1_w_down,
           mix_norm, ffn2_norm, ffn2_w_gate, ffn2_w_up, ffn2_w_down, ada_w, ada_b, ev_w_in, ev_w_out,
           pool_w, pool_scale, dn_conv_w, dn_a_log, dn_dt_bias, dn_norm_w, od_w_in, od_w_out, sgu_norm,
           sgu_w, sgu_b, fnet_w, final_norm):
    cond8 = jnp.zeros((SUBLANES, D_MODEL), F32).at[0].set(c_ctx).at[1:1 + DEC_BATCH].set(c)
    mods = _ada(cond8, ada_w, ada_b)
    mods = mods[:, :1 + DEC_BATCH].reshape(DEPTH, 1 + DEC_BATCH, N_MOD, 1, D_MODEL).transpose(0, 2, 1, 3, 4)

    fnet_tables = _fnet_tables()
    groups = ((SEQ, BATCH, 0, DN_HEADS), (DEC_SEQ, DEC_BATCH, CTX_TOK // DEC_SEQ, 3))
    even_segments = ((A_W, A_W + 4 * QK_W), (0, A_W), (A_W + 4 * QK_W, P_EVEN))
    odd_segments = ((0, 2 * C_W), (2 * C_W, P_ODD))

    x = (x_prompt.reshape(CTX_TOK, D_MODEL), x_sample.reshape(LAT_TOK, D_MODEL), _grid_pos_embed(DEC_SEQ))
    ev_w_in_t = jnp.swapaxes(ev_w_in, 1, 2)
    states = None
    for layer in range(DEPTH):
        x = _ffn(x, mods, layer, 0, ffn1_norm, ffn1_w_gate, ffn1_w_up, ffn1_w_down)
        if layer % 2 == 0:
            e = layer // 2
            qkvz, p_pool, gates, gates_t = _mixin(x, mods, layer, mix_norm, ev_w_in_t, e, even_segments, 512,
                                                  transposed=True)
            w_bd = _block_diag(pool_w[e])
            ya = _pool(p_pool, w_bd, pool_scale[e][None])
            yb = []
            for seq, n_seq, rb0, dn_heads in groups:
                ctx = rb0 == 0
                o, st = _deltanet(qkvz, gates, gates_t, dn_conv_w, e, dn_a_log[e], dn_dt_bias[e], dn_norm_w,
                                  None if ctx else state_delta, seq, dn_heads, n_seq, rb0,
                                  state_slot=e if ctx else None, states=states)
                yb.append(o)
                if ctx:
                    states = st
            mix = (ya, tuple(yb), ev_w_out, e)
        else:
            j = layer // 2
            uv, f_all = _mixin(x, mods, layer, mix_norm, od_w_in, j, odd_segments, TM)
            yc = _sgu(uv, sgu_norm[j][None], sgu_w[j], sgu_b[j].T)
            w_bd = _block_diag(fnet_w[j])
            yd = _fnet(f_all, fnet_tables, w_bd)
            mix = (yc, yd, od_w_out, j)
        x = _ffn(x, mods, layer, 2, ffn2_norm, ffn2_w_gate, ffn2_w_up, ffn2_w_down, mix=mix,
                 final_norm=final_norm if layer == DEPTH - 1 else None)

    y_prompt, y_sample = x
    return (y_prompt.reshape(BATCH, SEQ, D_MODEL), y_sample.reshape(DEC_BATCH, DEC_SEQ, D_MODEL), states)
```

```python
import functools
import math

import numpy as np
import jax
import jax.numpy as jnp
from jax import lax
from jax.experimental import pallas as pl
from jax.experimental.pallas import tpu as pltpu

F32 = jnp.float32
BF16 = jnp.bfloat16

D_MODEL = 1024
BATCH = 16
SEQ = 256
DEPTH = 4
DEC_BATCH = 2
DEC_SEQ = 1024
GRID_W = 64
EPS = 1e-6
D_FF = 2816
N_MOD = 9
N_EVEN = (DEPTH + 1) // 2
POOL_SIZES = (2, 4, 8, 16)
POOL_CH = 64
A_W = 256
DN_HEADS = 6
DN_DK = 128
QK_W = DN_HEADS * DN_DK
B_W = QK_W
GATE_W = 4 * DN_HEADS
P_EVEN = A_W + 4 * QK_W + GATE_W
SGU_CHUNK = 128
SGU_HEADS = 6
C_W = 768
FN_CH = 64
F_W = 256
P_ODD = 2 * C_W + F_W

CTX_TOK = BATCH * SEQ
LAT_TOK = DEC_BATCH * DEC_SEQ
N_TOK = CTX_TOK + LAT_TOK

LANES = 128
SUBLANES = 8
VMEM_LIMIT = 56 * 1024 * 1024

TM = 1024
N_ROW_TILES = N_TOK // TM
CTX_TILES = CTX_TOK // TM
TF = 256
DN_CHUNK = 128
COL_Q, COL_K, COL_V, COL_Z = 0, 1, 2, 3


def _cond_of_tile(i, tm):
    return jnp.where(i < CTX_TOK // tm, 0, 1 + (i - CTX_TOK // tm) * tm // DEC_SEQ)


def _silu(x):
    half = 0.5 * x
    return half + half * jnp.tanh(half)


def _dot(a, b):
    return jnp.dot(a.astype(BF16), b.astype(BF16), preferred_element_type=F32)


def _dot_nt(a, b):
    return lax.dot_general(a.astype(BF16), b.astype(BF16), (((1,), (1,)), ((), ())),
                           preferred_element_type=F32)


def _dot_tn(a, b):
    return lax.dot_general(a.astype(BF16), b.astype(BF16), (((0,), (0,)), ((), ())),
                           preferred_element_type=F32)


def _split(a):
    hi = a.astype(BF16)
    lo = (a - hi.astype(F32)).astype(BF16)
    return hi, lo


def _dot3(a, b):
    ah, al = a
    bh, bl = b
    return (jnp.dot(ah, bh, preferred_element_type=F32)
            + (jnp.dot(al, bh, preferred_element_type=F32)
               + jnp.dot(ah, bl, preferred_element_type=F32)))


def _rms(x, w):
    return x * lax.rsqrt(jnp.mean(x * x, axis=-1, keepdims=True) + EPS) * w


def _norm_mod(x, nw, scale, shift):
    r = lax.rsqrt(jnp.mean(x * x, axis=-1, keepdims=True) + EPS)
    return ((x * r) * (nw * (1.0 + scale)) + shift).astype(BF16)


def _params(*sem):
    return pltpu.CompilerParams(dimension_semantics=sem, vmem_limit_bytes=VMEM_LIMIT)


def _ctx_rows(tm, width):
    return pl.BlockSpec((tm, width), lambda i, *_: (jnp.minimum(i, CTX_TOK // tm - 1), 0))


def _lat_rows(tm, width):
    return pl.BlockSpec((tm, width), lambda i, *_: (jnp.maximum(i - CTX_TOK // tm, 0), 0))


N_ADA = N_MOD * D_MODEL


def _ada_specs(layer, tn):
    return ([pl.BlockSpec((SUBLANES, D_MODEL), lambda i: (0, 0)),
             pl.BlockSpec((None, D_MODEL, tn), lambda i: (layer, 0, i)),
             pl.BlockSpec((None, 1, tn), lambda i: (layer, 0, i))],
            pl.BlockSpec((SUBLANES, tn), lambda i: (0, i)))


def _ada_block(c_ref, w_ref, b_ref, o_ref):
    o_ref[...] = _dot(_silu(c_ref[...]), w_ref[...]) + b_ref[...]


def _ada(cond8, ada_w, ada_b3, layer):
    tn = 2304
    in_specs, out_spec = _ada_specs(layer, tn)
    return pl.pallas_call(
        _ada_block,
        grid=(N_ADA // tn,),
        in_specs=in_specs,
        out_specs=out_spec,
        out_shape=jax.ShapeDtypeStruct((SUBLANES, N_ADA), F32),
        compiler_params=_params("arbitrary"),
        name="ada",
    )(cond8, ada_w, ada_b3)


def _mod_layout(raw):
    n_cond = 1 + DEC_BATCH
    return raw[:n_cond].reshape(n_cond, N_MOD, 1, D_MODEL).transpose(1, 0, 2, 3)


def _mod_spec(which, tm=TM, n_seg=1, seg=0):
    return pl.BlockSpec((None, None, 1, D_MODEL),
                        lambda i, *_: (which, _cond_of_tile(i * n_seg + seg, tm), 0, 0))


def _layer_vec(layer):
    return pl.BlockSpec((None, 1, D_MODEL), lambda i, *_: (layer, 0, 0))


def _ffn_kernel(*refs, first, mix_counts, mix_split, final, n_seg):
    refs = list(refs)
    take = lambda k: [refs.pop(0) for _ in range(k)]
    if first:
        xp_ref, xs_ref, pos_ref = take(3)
    else:
        (x_in_ref,) = take(1)
    if mix_counts:
        mix_groups = [take(c) for c in mix_counts]
        g2_ref, wo_ref = take(2)
    (nw_ref,) = take(1)
    seg_mods = [take(3) for _ in range(n_seg)]
    wg_ref, wu_ref, wd_ref = take(3)
    if final:
        fn_ref, oc_ref, ol_ref = take(3)
    else:
        (o_ref,) = take(1)
    h_scr, acc_scr = take(2)
    x_ref = refs.pop(0) if (first or mix_counts) else x_in_ref
    i = pl.program_id(0)
    j = pl.program_id(1)
    ctx = i < CTX_TILES

    @pl.when(j == 0)
    def _():
        if first:
            @pl.when(ctx)
            def _():
                x_ref[...] = xp_ref[...]

            @pl.when(jnp.logical_not(ctx))
            def _():
                x_ref[...] = xs_ref[...] + pos_ref[...]
        elif mix_counts:
            def pick(rs):
                return rs[0][...] if len(rs) == 1 else jnp.where(ctx, rs[0][...], rs[1][...])

            mix = (_dot(pick(mix_groups[0]), wo_ref[0:mix_split, :])
                   + _dot(pick(mix_groups[1]), wo_ref[mix_split:, :]))
            x_ref[...] = x_in_ref[...] + g2_ref[...] * mix

        for s, (sh_ref, sc_ref, _) in enumerate(seg_mods):
            rows = slice(s * TM, (s + 1) * TM)
            h_scr[rows, :] = _norm_mod(x_ref[rows, :], nw_ref[...], sc_ref[...], sh_ref[...])
        acc_scr[...] = jnp.zeros_like(acc_scr)

    h = h_scr[...]
    g = jnp.dot(h, wg_ref[...].astype(BF16), preferred_element_type=F32)
    u = jnp.dot(h, wu_ref[...].astype(BF16), preferred_element_type=F32)
    acc_scr[...] += _dot(_silu(g) * u, wd_ref[...])

    @pl.when(j == pl.num_programs(1) - 1)
    def _():
        if final:
            y = _rms(x_ref[...] + 0.5 * seg_mods[0][2][...] * acc_scr[...], fn_ref[...])

            @pl.when(ctx)
            def _():
                oc_ref[...] = y

            @pl.when(jnp.logical_not(ctx))
            def _():
                ol_ref[...] = y
        else:
            for s, (_, _, gt_ref) in enumerate(seg_mods):
                rows = slice(s * TM, (s + 1) * TM)
                o_ref[rows, :] = x_ref[rows, :] + 0.5 * gt_ref[...] * acc_scr[rows, :]


def _ffn(x, mods, layer, sub, norm_w, w_gate, w_up, w_down, mix=None, final_norm=None):
    first = isinstance(x, tuple)
    final = final_norm is not None
    n_seg = 2 if (not first and mix is None and not final) else 1
    tm = n_seg * TM
    row = lambda width: pl.BlockSpec((tm, width), lambda i, j: (i, 0))
    if first:
        in_specs = [_ctx_rows(TM, D_MODEL), _lat_rows(TM, D_MODEL),
                    pl.BlockSpec((TM, D_MODEL), lambda i, j: (0, 0))]
        args = list(x)
    else:
        in_specs, args = [row(D_MODEL)], [x]
    mix_counts, mix_split = (), 0
    if mix is not None:
        ya, yb, w_out, w_index = mix
        counts = []
        for y in (ya, yb):
            parts = y if isinstance(y, tuple) else (y,)
            width = parts[0].shape[1]
            in_specs += [row(width)] if len(parts) == 1 else [_ctx_rows(TM, width), _lat_rows(TM, width)]
            args += list(parts)
            counts.append(len(parts))
        mix_counts = tuple(counts)
        mix_split = (ya[0] if isinstance(ya, tuple) else ya).shape[1]
        in_specs += [_mod_spec(5),
                     pl.BlockSpec((None, D_MODEL, D_MODEL), lambda i, j: (w_index, 0, 0),
                                  pipeline_mode=pl.Buffered(1))]
        args += [mods, w_out]
    in_specs.append(_layer_vec(layer))
    args.append(norm_w.reshape(DEPTH, 1, D_MODEL))
    for seg in range(n_seg):
        in_specs += [_mod_spec(3 * sub + k, TM, n_seg, seg) for k in range(3)]
        args += [mods, mods, mods]
    in_specs += [pl.BlockSpec((None, D_MODEL, TF), lambda i, j: (layer, 0, j)),
                 pl.BlockSpec((None, D_MODEL, TF), lambda i, j: (layer, 0, j)),
                 pl.BlockSpec((None, TF, D_MODEL), lambda i, j: (layer, j, 0))]
    args += [w_gate, w_up, w_down]
    scratch = [pltpu.VMEM((tm, D_MODEL), BF16), pltpu.VMEM((tm, D_MODEL), F32)]
    if first or mix_counts:
        scratch.append(pltpu.VMEM((TM, D_MODEL), F32))
    if final:
        in_specs.append(pl.BlockSpec((1, D_MODEL), lambda i, j: (0, 0)))
        args.append(final_norm[None])
        out_specs = [_ctx_rows(TM, D_MODEL), _lat_rows(TM, D_MODEL)]
        out_shape = [jax.ShapeDtypeStruct((CTX_TOK, D_MODEL), F32), jax.ShapeDtypeStruct((LAT_TOK, D_MODEL), F32)]
    else:
        out_specs = row(D_MODEL)
        out_shape = jax.ShapeDtypeStruct((N_TOK, D_MODEL), F32)
    return pl.pallas_call(
        functools.partial(_ffn_kernel, first=first, mix_counts=mix_counts, mix_split=mix_split, final=final,
                          n_seg=n_seg),
        grid=(N_TOK // tm, D_FF // TF),
        in_specs=in_specs,
        out_specs=out_specs,
        out_shape=out_shape,
        scratch_shapes=scratch,
        compiler_params=_params("arbitrary", "arbitrary"),
        name=f"ffn{sub}_{layer}",
    )(*args)


def _mixin_kernel(x_ref, nw_ref, sh_ref, sc_ref, w_ref, *rest, segments, transposed, with_ada):
    if with_ada:
        _ada_block(*rest[:3], rest[-2])
        rest = rest[3:-2] + rest[-1:]
    o_refs, wb_scr = rest[:-1], rest[-1]

    @pl.when(pl.program_id(0) == 0)
    def _():
        wb_scr[...] = w_ref[...].astype(BF16)

    h = _norm_mod(x_ref[...], nw_ref[...], sc_ref[...], sh_ref[...])
    for (lo, hi), o_ref in zip(segments, o_refs):
        if transposed:
            o_ref[...] = _dot_nt(h, wb_scr[lo:hi, :])
        else:
            o_ref[...] = _dot(h, wb_scr[:, lo:hi])
    if transposed:
        lo, hi = segments[-1]
        o_refs[-1][...] = _dot_nt(wb_scr[lo:hi, :], h)


def _mixin(x, mods, layer, norm_w, w_in, w_index, segments, tm, transposed=False, ada=None):
    shape = w_in.shape[1:]
    n_tiles = N_TOK // tm
    in_specs = [pl.BlockSpec((tm, D_MODEL), lambda i: (i, 0)),
                _layer_vec(layer), _mod_spec(3, tm), _mod_spec(4, tm),
                pl.BlockSpec((None,) + shape, lambda i: (w_index, 0, 0), pipeline_mode=pl.Buffered(1))]
    args = [x, norm_w.reshape(DEPTH, 1, D_MODEL), mods, mods, w_in]
    out_specs = [pl.BlockSpec((tm, hi - lo), lambda i: (i, 0)) for lo, hi in segments]
    out_shape = [jax.ShapeDtypeStruct((N_TOK, hi - lo), F32) for lo, hi in segments]
    if transposed:
        lo, hi = segments[-1]
        out_specs.append(pl.BlockSpec((hi - lo, tm), lambda i: (0, i)))
        out_shape.append(jax.ShapeDtypeStruct((hi - lo, N_TOK), F32))
    if ada is not None:
        ada_in, ada_out = _ada_specs(layer + 1, N_ADA // n_tiles)
        in_specs += ada_in
        args += list(ada)
        out_specs.append(ada_out)
        out_shape.append(jax.ShapeDtypeStruct((SUBLANES, N_ADA), F32))
    return pl.pallas_call(
        functools.partial(_mixin_kernel, segments=segments, transposed=transposed, with_ada=ada is not None),
        grid=(n_tiles,),
        in_specs=in_specs,
        out_specs=out_specs,
        out_shape=out_shape,
        scratch_shapes=[pltpu.VMEM(shape, BF16)],
        compiler_params=_params("arbitrary"),
        name=f"mixin_{layer}",
    )(*args)


def _pool_kernel(p_ref, w_ref, scale_ref, o_ref):
    seq = jnp.where(pl.program_id(0) < CTX_TILES, SEQ, DEC_SEQ)
    p = p_ref[...]
    pos = lax.broadcasted_iota(jnp.int32, (TM, A_W), 0) & (seq - 1)

    def before(a, s):
        return jnp.where(pos >= s, pltpu.roll(a, s, axis=0), 0.0)

    def after(a, s):
        return jnp.where(pos < seq - s, pltpu.roll(a, TM - s, axis=0), 0.0)

    sums = []
    f, b = p, before(p, 1)
    for h in (size // 2 for size in POOL_SIZES):
        if h > 1:
            f = f + after(f, h // 2)
            b = b + before(b, h // 2)
        sums.append(f + b)
    group = lax.broadcasted_iota(jnp.int32, (TM, A_W), 1) >> (POOL_CH.bit_length() - 1)
    wsum = jnp.where(group == 0, sums[0], jnp.where(group == 1, sums[1], jnp.where(group == 2, sums[2], sums[3])))
    half = jnp.left_shift(1, group)
    cnt = (jnp.minimum(pos + half, seq) - jnp.maximum(pos - half, 0)).astype(F32)
    d = wsum / cnt - p
    o_ref[...] = (_dot(d, w_ref[...]) * scale_ref[...]).astype(o_ref.dtype)


def _pool(p_pool, w_bd, scale):
    return pl.pallas_call(
        _pool_kernel,
        grid=(N_ROW_TILES,),
        in_specs=[pl.BlockSpec((TM, A_W), lambda i: (i, 0)),
                  pl.BlockSpec((A_W, A_W), lambda i: (0, 0)),
                  pl.BlockSpec((1, A_W), lambda i: (0, 0))],
        out_specs=pl.BlockSpec((TM, A_W), lambda i: (i, 0)),
        out_shape=jax.ShapeDtypeStruct((N_TOK, A_W), BF16),
        compiler_params=_params("arbitrary"),
        name="pool",
    )(p_pool, w_bd, scale)


def _link_mask(r, c, s, upper):
    lg = s.bit_length() - 1
    same = (r >> (lg + 1)) == (c >> (lg + 1))
    r_half = (r >> lg) & 1
    c_half = (c >> lg) & 1
    return same & ((r_half == 0) & (c_half == 1) if upper else (r_half == 1) & (c_half == 0))


def _dn_kernel(q_ref, k_ref, v_ref, z_ref, g_ref, gt_ref, cq_ref, ck_ref, cv_ref, alog_ref, dtb_ref,
               alog_c_ref, dtb_c_ref, nw_ref, *rest, seq, heads, zero_init, state_slot):
    rest = list(rest)
    s0_ref = None if zero_init else rest.pop(0)
    if state_slot:
        rest.pop(0)
    o_ref = rest.pop(0)
    st_ref = rest.pop(0) if state_slot is not None else None
    gct_scr = rest.pop(0)
    h0 = pl.program_id(1) * heads
    n = DN_CHUNK
    n_blk = seq // n
    row = lax.broadcasted_iota(jnp.int32, (seq, LANES), 0)
    g_lane = lax.broadcasted_iota(jnp.int32, (seq, GATE_W), 1)
    r2 = lax.broadcasted_iota(jnp.int32, (n, n), 0)
    c2 = lax.broadcasted_iota(jnp.int32, (n, n), 1)
    eye = (r2 == c2).astype(F32)
    incl = (r2 >= c2, r2 <= c2)
    strict = (r2 > c2, r2 < c2)
    levels = [1 << b for b in range(n.bit_length() - 1)]
    link = {(s, d): _link_mask(r2, c2, s, d == 1) for s in levels for d in (0, 1)}
    units = [(hh, blk, d) for hh in range(heads) for blk in range(n_blk) for d in (0, 1)]

    def conv_silu(x, cw):
        prev = jnp.where(row >= 1, pltpu.roll(x, 1, axis=0), 0.0)
        nxt = jnp.where(row <= seq - 2, pltpu.roll(x, seq - 1, axis=0), 0.0)
        return _silu(prev * cw[0:1] + x * cw[1:2] + nxt * cw[2:3])

    def l2n(x):
        return x * lax.rsqrt(jnp.sum(x * x, axis=-1, keepdims=True) + EPS)

    def col(a, idx):
        return jnp.sum(jnp.where(g_lane == idx, a, 0.0), axis=1, keepdims=True)

    def log_decay(a, alog, dtb):
        xg = a + dtb
        return -jnp.exp(alog) * (jnp.maximum(xg, 0.0) + jnp.log1p(jnp.exp(-jnp.abs(xg))))

    gates = g_ref[...]
    beta_all = jax.nn.sigmoid(gates)
    g_all = log_decay(gates, alog_ref[...], dtb_ref[...])
    g_all_t = log_decay(gt_ref[...], alog_c_ref[...], dtb_c_ref[...])
    lower = incl[0].astype(BF16)
    upper = incl[1].astype(BF16)
    backward_rows = lax.broadcasted_iota(jnp.int32, (GATE_W, n), 0) >= 3 * DN_HEADS

    def tri_sums(a, left):
        rem, pre_sum, suf_sum = a, 0.0, 0.0
        for _ in range(3):
            piece = rem.astype(BF16)
            rem = rem - piece.astype(F32)
            if left:
                pre_sum = pre_sum + jnp.dot(lower, piece, preferred_element_type=F32)
                suf_sum = suf_sum + jnp.dot(upper, piece, preferred_element_type=F32)
            else:
                pre_sum = pre_sum + jnp.dot(piece, upper, preferred_element_type=F32)
                suf_sum = suf_sum + jnp.dot(piece, lower, preferred_element_type=F32)
        return pre_sum, suf_sum

    pre_blocks, suf_blocks = [], []
    for blk in range(n_blk):
        rs = slice(blk * n, (blk + 1) * n)
        p, s_ = tri_sums(g_all[rs, :], left=True)
        pre_blocks.append(p)
        suf_blocks.append(s_)
        p_t, s_t = tri_sums(g_all_t[:, rs], left=False)
        sums = jnp.where(backward_rows, s_t, p_t)
        for r in range(2 * DN_HEADS):
            gct_scr[r, :, rs] = sums[2 * DN_HEADS + r:2 * DN_HEADS + r + 1, :]
    pre = jnp.concatenate(pre_blocks, axis=0)
    suf = jnp.concatenate(suf_blocks, axis=0)

    qs, kn, vv, beta, gc = [], [], [], [], []
    for hh in range(heads):
        cs = slice(hh * LANES, (hh + 1) * LANES)
        qs.append(l2n(conv_silu(q_ref[:, cs], cq_ref[:, cs])) * (DN_DK ** -0.5))
        kn.append(l2n(conv_silu(k_ref[:, cs], ck_ref[:, cs])))
        vv.append(conv_silu(v_ref[:, cs], cv_ref[:, cs]))
        beta.append((col(beta_all, h0 + hh), col(beta_all, DN_HEADS + h0 + hh)))
        gc.append((col(pre, 2 * DN_HEADS + h0 + hh), col(suf, 3 * DN_HEADS + h0 + hh)))

    kk, qk = {}, {}
    for hh in range(heads):
        for blk in range(n_blk):
            rs = slice(blk * n, (blk + 1) * n)
            kk[hh, blk] = _dot_nt(kn[hh][rs], kn[hh][rs])
            qk[hh, blk] = _dot_nt(qs[hh][rs], kn[hh][rs])

    m, m_hi, t, a_in = {}, {}, {}, {}
    for u in units:
        hh, blk, d = u
        rs = slice(blk * n, (blk + 1) * n)
        g_lanes = gct_scr[d * DN_HEADS + h0 + hh, :, rs]
        decay = jnp.where(incl[d], jnp.exp(gc[hh][d][rs] - g_lanes), 0.0)
        m[u] = jnp.where(strict[d], beta[hh][d][rs] * kk[hh, blk] * decay, 0.0)
        m_hi[u] = m[u].astype(BF16)
        a_in[u] = (qk[hh, blk] * decay).astype(BF16)
        t[u] = eye - jnp.where(link[1, d], m[u], 0.0)

    for s in levels[1:]:
        tb, x = {}, {}
        for u in units:
            tb[u] = t[u].astype(BF16)
            x[u] = jnp.where(link[s, u[2]], jnp.dot(m_hi[u], tb[u], preferred_element_type=F32), 0.0)
        for u in units:
            t[u] = t[u] - jnp.dot(tb[u], x[u].astype(BF16), preferred_element_type=F32)

    uw, q_dec, k_dec, g_last = {}, {}, {}, {}
    for u in units:
        hh, blk, d = u
        rs = slice(blk * n, (blk + 1) * n)
        gcol = gc[hh][d][rs]
        bcol = beta[hh][d][rs]
        e_g = jnp.exp(gcol)
        kb = kn[hh][rs]
        uw[u] = _dot(t[u], jnp.concatenate([vv[hh][rs] * bcol, kb * (bcol * e_g)], axis=1))
        g_last[u] = gcol[n - 1:n] if d == 0 else gcol[0:1]
        q_dec[u] = qs[hh][rs] * e_g
        k_dec[u] = kb * jnp.exp(g_last[u] - gcol)

    state = {}
    for hh in range(heads):
        for d in (0, 1):
            state[hh, d] = jnp.zeros((DN_DK, DN_DK), F32) if zero_init else s0_ref[d, hh]
    outs = {}
    for step in range(n_blk):
        chains = [(hh, step if d == 0 else n_blk - 1 - step, d) for hh in range(heads) for d in (0, 1)]
        ws_qs = {}
        for u in chains:
            ws_qs[u] = _dot(jnp.concatenate([uw[u][:, LANES:], q_dec[u]], axis=0), state[u[0], u[2]])
        for u in chains:
            v_new = (uw[u][:, :LANES] - ws_qs[u][0:n]).astype(BF16)
            outs[u] = ws_qs[u][n:] + jnp.dot(a_in[u], v_new, preferred_element_type=F32)
            state[u[0], u[2]] = state[u[0], u[2]] * jnp.exp(g_last[u]) + _dot_tn(k_dec[u], v_new)

    for hh in range(heads):
        cs = slice(hh * LANES, (hh + 1) * LANES)
        o = (jnp.concatenate([outs[hh, blk, 0] for blk in range(n_blk)], axis=0)
             + jnp.concatenate([outs[hh, blk, 1] for blk in range(n_blk)], axis=0))
        o = o * lax.rsqrt(jnp.mean(o * o, axis=-1, keepdims=True) + EPS) * nw_ref[...]
        o_ref[:, cs] = (o * _silu(z_ref[:, cs])).astype(o_ref.dtype)
        if state_slot == 0:
            st_ref[0, 0, hh] = state[hh, 0]
            st_ref[0, 1, hh] = state[hh, 1]
            st_ref[1:, :, hh] = jnp.zeros((N_EVEN - 1, 2, DN_DK, DN_DK), F32)
        elif state_slot:
            st_ref[0, hh] = state[hh, 0]
            st_ref[1, hh] = state[hh, 1]


def _deltanet(qkvz, gates, gates_t, conv_w, e, alog, dtb, norm_w, s0, seq, heads, n_seq, row_block0,
              state_slot=None, states=None):
    zero_init = s0 is None
    width = heads * LANES
    per = DN_HEADS // heads
    alog_row = jnp.concatenate([jnp.zeros((2 * DN_HEADS,), F32), alog.reshape(-1)])[None]
    dtb_row = jnp.concatenate([jnp.zeros((2 * DN_HEADS,), F32), dtb.reshape(-1)])[None]

    def pcol(cb):
        return pl.BlockSpec((seq, width), lambda b, h: (row_block0 + b, cb * per + h))

    def ccol(cb):
        return pl.BlockSpec((None, 3, width), lambda b, h: (e, 0, cb * per + h))

    grow = pl.BlockSpec((1, GATE_W), lambda b, h: (0, 0))
    gcol = pl.BlockSpec((GATE_W, 1), lambda b, h: (0, 0))
    in_specs = [pcol(COL_Q), pcol(COL_K), pcol(COL_V), pcol(COL_Z),
                pl.BlockSpec((seq, GATE_W), lambda b, h: (row_block0 + b, 0)),
                pl.BlockSpec((GATE_W, seq), lambda b, h: (0, row_block0 + b)),
                ccol(0), ccol(1), ccol(2), grow, grow, gcol, gcol,
                pl.BlockSpec((None, 1, LANES), lambda b, h: (e, 0, 0))]
    args = [qkvz, qkvz, qkvz, qkvz, gates, gates_t, conv_w, conv_w, conv_w, alog_row, dtb_row,
            alog_row.T, dtb_row.T, norm_w.reshape(N_EVEN, 1, LANES)]
    if not zero_init:
        in_specs.append(pl.BlockSpec((None, None, 2, heads, DN_DK, DN_DK), lambda b, h: (b, e, 0, h, 0, 0)))
        args.append(s0)
    aliases = {}
    if state_slot:
        aliases = {len(args): 1}
        in_specs.append(pl.BlockSpec(memory_space=pl.ANY))
        args.append(states)
    out_specs = [pl.BlockSpec((seq, width), lambda b, h: (b, h))]
    out_shape = [jax.ShapeDtypeStruct((n_seq * seq, B_W), BF16)]
    if state_slot is not None:
        if state_slot == 0:
            out_specs.append(pl.BlockSpec((None, N_EVEN, 2, heads, DN_DK, DN_DK), lambda b, h: (b, 0, 0, h, 0, 0)))
        else:
            out_specs.append(pl.BlockSpec((None, None, 2, heads, DN_DK, DN_DK),
                                          lambda b, h: (b, state_slot, 0, h, 0, 0)))
        out_shape.append(jax.ShapeDtypeStruct((n_seq, N_EVEN, 2, DN_HEADS, DN_DK, DN_DK), F32))
    res = pl.pallas_call(
        functools.partial(_dn_kernel, seq=seq, heads=heads, zero_init=zero_init, state_slot=state_slot),
        grid=(n_seq, per),
        in_specs=in_specs,
        out_specs=out_specs,
        out_shape=out_shape,
        scratch_shapes=[pltpu.VMEM((2 * DN_HEADS, 1, seq), F32)],
        input_output_aliases=aliases,
        compiler_params=_params("arbitrary", "arbitrary"),
        name=f"deltanet_{seq}",
    )(*args)
    return res if state_slot is not None else (res[0], None)


TS = 1024


def _sgu_kernel(u_ref, v_ref, nw_ref, ws_ref, bt_ref, o_ref):
    u = jax.nn.gelu(u_ref[...])
    v = jax.nn.gelu(v_ref[...])
    mu = jnp.mean(v, axis=-1, keepdims=True)
    vc = v - mu
    var = jnp.mean(vc * vc, axis=-1, keepdims=True)
    vn = vc * lax.rsqrt(var + EPS) * nw_ref[...]
    bt = bt_ref[...]
    for c in range(TS // SGU_CHUNK):
        rs = slice(c * SGU_CHUNK, (c + 1) * SGU_CHUNK)
        for hd in range(SGU_HEADS):
            cs = slice(hd * LANES, (hd + 1) * LANES)
            mixed = _dot(ws_ref[hd], vn[rs, cs]) + bt[:, hd:hd + 1]
            o_ref[rs, cs] = (u[rs, cs] * mixed).astype(o_ref.dtype)


def _sgu(uv, norm_w, w_s, b_t):
    return pl.pallas_call(
        _sgu_kernel,
        grid=(N_TOK // TS,),
        in_specs=[pl.BlockSpec((TS, C_W), lambda i: (i, 0)),
                  pl.BlockSpec((TS, C_W), lambda i: (i, 1)),
                  pl.BlockSpec((1, C_W), lambda i: (0, 0)),
                  pl.BlockSpec((SGU_HEADS, SGU_CHUNK, SGU_CHUNK), lambda i: (0, 0, 0)),
                  pl.BlockSpec((SGU_CHUNK, SGU_HEADS), lambda i: (0, 0))],
        out_specs=pl.BlockSpec((TS, C_W), lambda i: (i, 0)),
        out_shape=jax.ShapeDtypeStruct((N_TOK, C_W), BF16),
        compiler_params=_params("arbitrary"),
        name="sgu",
    )(uv, uv, norm_w, w_s, b_t)


def _fnet_kernel(f_ref, ch_hi_ref, ch_lo_ref, ctx_hi_ref, ctx_lo_ref, lat_hi_ref, lat_lo_ref, w_ref, o_ref):
    fcs = _dot3(_split(f_ref[...]), (ch_hi_ref[...], ch_lo_ref[...]))
    wb = w_ref[...].astype(BF16)
    ctx = pl.program_id(0) < CTX_TILES

    def finish(rows, hi_ref, lo_ref):
        stacked = jnp.concatenate([fcs[rows, :F_W], fcs[rows, F_W:]], axis=0)
        spec = _dot3((hi_ref[...], lo_ref[...]), _split(stacked))
        o_ref[rows, :] = jnp.dot(spec.astype(BF16), wb, preferred_element_type=F32).astype(o_ref.dtype)

    @pl.when(ctx)
    def _():
        for s in range(TM // SEQ):
            finish(slice(s * SEQ, (s + 1) * SEQ), ctx_hi_ref, ctx_lo_ref)

    @pl.when(jnp.logical_not(ctx))
    def _():
        finish(slice(0, DEC_SEQ), lat_hi_ref, lat_lo_ref)


def _fnet(f_all, tables, w_bd):
    const = lambda a: pl.BlockSpec(a.shape, lambda i: (0, 0))
    return pl.pallas_call(
        _fnet_kernel,
        grid=(N_ROW_TILES,),
        in_specs=[pl.BlockSpec((TM, F_W), lambda i: (i, 0))] + [const(t) for t in tables] + [const(w_bd)],
        out_specs=pl.BlockSpec((TM, F_W), lambda i: (i, 0)),
        out_shape=jax.ShapeDtypeStruct((N_TOK, F_W), BF16),
        compiler_params=_params("arbitrary"),
        name="fnet",
    )(f_all, *tables, w_bd)


def _grid_pos_embed(n_tok):
    rows = n_tok // GRID_W
    r = np.repeat(np.arange(rows, dtype=np.float64), GRID_W)
    col = np.tile(np.arange(GRID_W, dtype=np.float64), rows)
    quarter = D_MODEL // 4
    freq = np.exp(-math.log(10000.0) * np.arange(quarter, dtype=np.float64) / quarter)

    def emb(p):
        ang = p[:, None] * freq[None, :]
        return np.concatenate([np.sin(ang), np.cos(ang)], axis=-1)

    return np.concatenate([emb(r), emb(col)], axis=-1).astype(np.float32)


def _dft_tables(n):
    k = np.arange(n, dtype=np.int64)
    ang = ((k[:, None] * k[None, :]) % n).astype(np.float64) * (2.0 * math.pi / n)
    scale = n ** -0.5
    return np.cos(ang) * scale, np.sin(ang) * scale


def _host_split(a):
    hi = a.astype(np.float32).astype(BF16)
    lo = (a - hi.astype(np.float64)).astype(np.float32).astype(BF16)
    return hi, lo


def _fnet_tables():
    cc, sc = _dft_tables(FN_CH)
    eye = np.eye(F_W // FN_CH)
    tables = list(_host_split(np.concatenate([np.kron(eye, cc), np.kron(eye, sc)], axis=1)))
    for n in (SEQ, DEC_SEQ):
        cn, sn = _dft_tables(n)
        tables += list(_host_split(np.concatenate([cn, -sn], axis=1)))
    return tables


def _block_diag(blocks):
    g, a, b = blocks.shape
    eye = jnp.eye(g, dtype=blocks.dtype)
    return (eye[:, None, :, None] * blocks[:, :, None, :]).reshape(g * a, g * b)


def kernel(x_prompt, x_sample, state_delta, c, c_ctx, ffn1_norm, ffn1_w_gate, ffn1_w_up, ffn1_w_down,
           mix_norm, ffn2_norm, ffn2_w_gate, ffn2_w_up, ffn2_w_down, ada_w, ada_b, ev_w_in, ev_w_out,
           pool_w, pool_scale, dn_conv_w, dn_a_log, dn_dt_bias, dn_norm_w, od_w_in, od_w_out, sgu_norm,
           sgu_w, sgu_b, fnet_w, final_norm):
    cond8 = jnp.zeros((SUBLANES, D_MODEL), F32).at[0].set(c_ctx).at[1:1 + DEC_BATCH].set(c)
    ada = (cond8, ada_w, ada_b.reshape(DEPTH, 1, N_ADA))
    mods = _mod_layout(_ada(*ada, 0))

    fnet_tables = _fnet_tables()
    groups = ((SEQ, BATCH, 0, DN_HEADS), (DEC_SEQ, DEC_BATCH, CTX_TOK // DEC_SEQ, 3))
    even_segments = ((A_W, A_W + 4 * QK_W), (0, A_W), (A_W + 4 * QK_W, P_EVEN))
    odd_segments = ((0, 2 * C_W), (2 * C_W, P_ODD))

    x = (x_prompt.reshape(CTX_TOK, D_MODEL), x_sample.reshape(LAT_TOK, D_MODEL), _grid_pos_embed(DEC_SEQ))
    ev_w_in_t = jnp.swapaxes(ev_w_in, 1, 2)
    states = None
    for layer in range(DEPTH):
        x = _ffn(x, mods, layer, 0, ffn1_norm, ffn1_w_gate, ffn1_w_up, ffn1_w_down)
        next_ada = ada if layer + 1 < DEPTH else None
        if layer % 2 == 0:
            e = layer // 2
            qkvz, p_pool, gates, gates_t, *next_mods = _mixin(x, mods, layer, mix_norm, ev_w_in_t, e, even_segments,
                                                              512, transposed=True, ada=next_ada)
            w_bd = _block_diag(pool_w[e])
            ya = _pool(p_pool, w_bd, pool_scale[e][None])
            yb = []
            for seq, n_seq, rb0, dn_heads in groups:
                ctx = rb0 == 0
                o, st = _deltanet(qkvz, gates, gates_t, dn_conv_w, e, dn_a_log[e], dn_dt_bias[e], dn_norm_w,
                                  None if ctx else state_delta, seq, dn_heads, n_seq, rb0,
                                  state_slot=e if ctx else None, states=states)
                yb.append(o)
                if ctx:
                    states = st
            mix = (ya, tuple(yb), ev_w_out, e)
        else:
            j = layer // 2
            uv, f_all, *next_mods = _mixin(x, mods, layer, mix_norm, od_w_in, j, odd_segments, TM, ada=next_ada)
            yc = _sgu(uv, sgu_norm[j][None], sgu_w[j], sgu_b[j].T)
            w_bd = _block_diag(fnet_w[j])
            yd = _fnet(f_all, fnet_tables, w_bd)
            mix = (yc, yd, od_w_out, j)
        x = _ffn(x, mods, layer, 2, ffn2_norm, ffn2_w_gate, ffn2_w_up, ffn2_w_down, mix=mix,
                 final_norm=final_norm if layer == DEPTH - 1 else None)
        if next_mods:
            mods = _mod_layout(next_mods[0])

    y_prompt, y_sample = x
    return (y_prompt.reshape(BATCH, SEQ, D_MODEL), y_sample.reshape(DEC_BATCH, DEC_SEQ, D_MODEL), states)
```

```python
import functools
import math

import numpy as np
import jax
import jax.numpy as jnp
from jax import lax
from jax.experimental import pallas as pl
from jax.experimental.pallas import tpu as pltpu

F32 = jnp.float32
BF16 = jnp.bfloat16

D_MODEL = 1024
BATCH = 16
SEQ = 256
DEPTH = 4
DEC_BATCH = 2
DEC_SEQ = 1024
GRID_W = 64
EPS = 1e-6
D_FF = 2816
N_MOD = 9
N_EVEN = (DEPTH + 1) // 2
POOL_SIZES = (2, 4, 8, 16)
POOL_CH = 64
A_W = 256
DN_HEADS = 6
DN_DK = 128
QK_W = DN_HEADS * DN_DK
B_W = QK_W
GATE_W = 4 * DN_HEADS
P_EVEN = A_W + 4 * QK_W + GATE_W
SGU_CHUNK = 128
SGU_HEADS = 6
C_W = 768
FN_CH = 64
F_W = 256
P_ODD = 2 * C_W + F_W

CTX_TOK = BATCH * SEQ
LAT_TOK = DEC_BATCH * DEC_SEQ
N_TOK = CTX_TOK + LAT_TOK

LANES = 128
SUBLANES = 8
VMEM_LIMIT = 56 * 1024 * 1024

TM = 1024
N_ROW_TILES = N_TOK // TM
CTX_TILES = CTX_TOK // TM
TF = 256
TM_EVEN_PROJ = 512
ADA_TN = 2304
DN_CHUNK = 128
DN_HEADS_PER_STEP = {SEQ: DN_HEADS, DEC_SEQ: 3}
COL_Q, COL_K, COL_V, COL_Z = 0, 1, 2, 3


def _cond_of_tile(i, tm):
    return jnp.where(i < CTX_TOK // tm, 0, 1 + (i - CTX_TOK // tm) * tm // DEC_SEQ)


def _silu(x):
    half = 0.5 * x
    return half + half * jnp.tanh(half)


def _dot(a, b):
    return jnp.dot(a.astype(BF16), b.astype(BF16), preferred_element_type=F32)


def _dot_nt(a, b):
    return lax.dot_general(a.astype(BF16), b.astype(BF16), (((1,), (1,)), ((), ())),
                           preferred_element_type=F32)


def _dot_tn(a, b):
    return lax.dot_general(a.astype(BF16), b.astype(BF16), (((0,), (0,)), ((), ())),
                           preferred_element_type=F32)


def _split(a):
    hi = a.astype(BF16)
    lo = (a - hi.astype(F32)).astype(BF16)
    return hi, lo


def _dot3(a, b):
    ah, al = a
    bh, bl = b
    return (jnp.dot(ah, bh, preferred_element_type=F32)
            + (jnp.dot(al, bh, preferred_element_type=F32)
               + jnp.dot(ah, bl, preferred_element_type=F32)))


def _rms(x, w):
    return x * lax.rsqrt(jnp.mean(x * x, axis=-1, keepdims=True) + EPS) * w


def _norm_mod(x, nw, scale, shift):
    r = lax.rsqrt(jnp.mean(x * x, axis=-1, keepdims=True) + EPS)
    return ((x * r) * (nw * (1.0 + scale)) + shift).astype(BF16)


def _params(*sem):
    return pltpu.CompilerParams(dimension_semantics=sem, vmem_limit_bytes=VMEM_LIMIT)


def _ctx_rows(tm, width):
    return pl.BlockSpec((tm, width), lambda i, *_: (jnp.minimum(i, CTX_TOK // tm - 1), 0))


def _lat_rows(tm, width):
    return pl.BlockSpec((tm, width), lambda i, *_: (jnp.maximum(i - CTX_TOK // tm, 0), 0))


def _ada_kernel(c_ref, w_ref, b_ref, o_ref):
    o_ref[0] = _dot(_silu(c_ref[...]), w_ref[0]) + b_ref[0]


def _ada(cond8, ada_w, ada_b):
    tn = ADA_TN
    n = N_MOD * D_MODEL
    return pl.pallas_call(
        _ada_kernel,
        grid=(DEPTH, n // tn),
        in_specs=[pl.BlockSpec((SUBLANES, D_MODEL), lambda l, j: (0, 0)),
                  pl.BlockSpec((1, D_MODEL, tn), lambda l, j: (l, 0, j)),
                  pl.BlockSpec((1, 1, tn), lambda l, j: (l, 0, j))],
        out_specs=pl.BlockSpec((1, SUBLANES, tn), lambda l, j: (l, 0, j)),
        out_shape=jax.ShapeDtypeStruct((DEPTH, SUBLANES, n), F32),
        compiler_params=_params("arbitrary", "arbitrary"),
        name="ada",
    )(cond8, ada_w, ada_b.reshape(DEPTH, 1, n))


def _mod_spec(layer, which, tm=TM, n_seg=1, seg=0):
    return pl.BlockSpec((None, None, None, 1, D_MODEL),
                        lambda i, *_: (layer, which, _cond_of_tile(i * n_seg + seg, tm), 0, 0))


def _layer_vec(layer):
    return pl.BlockSpec((None, 1, D_MODEL), lambda i, *_: (layer, 0, 0))


def _ffn_kernel(*refs, first, mix_counts, mix_split, final, n_seg, per_step):
    refs = list(refs)
    take = lambda k: [refs.pop(0) for _ in range(k)]
    if first:
        xp_ref, xs_ref, pos_ref = take(3)
    else:
        (x_in_ref,) = take(1)
    if mix_counts:
        mix_groups = [take(c) for c in mix_counts]
        g2_ref, wo_ref = take(2)
    (nw_ref,) = take(1)
    seg_mods = [take(3) for _ in range(n_seg)]
    w_blocks = [take(3) for _ in range(per_step)]
    if final:
        fn_ref, oc_ref, ol_ref = take(3)
    else:
        (o_ref,) = take(1)
    h_scr, acc_scr = take(2)
    x_ref = refs.pop(0) if (first or mix_counts) else x_in_ref
    i = pl.program_id(0)
    j = pl.program_id(1)
    ctx = i < CTX_TILES

    @pl.when(j == 0)
    def _():
        if first:
            @pl.when(ctx)
            def _():
                x_ref[...] = xp_ref[...]

            @pl.when(jnp.logical_not(ctx))
            def _():
                x_ref[...] = xs_ref[...] + pos_ref[...]
        elif mix_counts:
            def pick(rs):
                return rs[0][...] if len(rs) == 1 else jnp.where(ctx, rs[0][...], rs[1][...])

            mix = (_dot(pick(mix_groups[0]), wo_ref[0:mix_split, :])
                   + _dot(pick(mix_groups[1]), wo_ref[mix_split:, :]))
            x_ref[...] = x_in_ref[...] + g2_ref[...] * mix

        for s, (sh_ref, sc_ref, _) in enumerate(seg_mods):
            rows = slice(s * TM, (s + 1) * TM)
            h_scr[rows, :] = _norm_mod(x_ref[rows, :], nw_ref[...], sc_ref[...], sh_ref[...])
        acc_scr[...] = jnp.zeros_like(acc_scr)

    def accumulate(blocks):
        h = h_scr[...]
        total = None
        for wg_ref, wu_ref, wd_ref in blocks:
            g = jnp.dot(h, wg_ref[...].astype(BF16), preferred_element_type=F32)
            u = jnp.dot(h, wu_ref[...].astype(BF16), preferred_element_type=F32)
            part = _dot(_silu(g) * u, wd_ref[...])
            total = part if total is None else total + part
        acc_scr[...] += total

    last = pl.num_programs(1) - 1
    if per_step == 1:
        accumulate(w_blocks)
    else:
        @pl.when(j < last)
        def _():
            accumulate(w_blocks)

        @pl.when(j == last)
        def _():
            accumulate(w_blocks[:1])

    @pl.when(j == last)
    def _():
        if final:
            y = _rms(x_ref[...] + 0.5 * seg_mods[0][2][...] * acc_scr[...], fn_ref[...])

            @pl.when(ctx)
            def _():
                oc_ref[...] = y

            @pl.when(jnp.logical_not(ctx))
            def _():
                ol_ref[...] = y
        else:
            for s, (_, _, gt_ref) in enumerate(seg_mods):
                rows = slice(s * TM, (s + 1) * TM)
                o_ref[rows, :] = x_ref[rows, :] + 0.5 * gt_ref[...] * acc_scr[rows, :]


def _ffn(x, mods, layer, sub, norm_w, w_gate, w_up, w_down, mix=None, final_norm=None):
    first = isinstance(x, tuple)
    final = final_norm is not None
    n_seg = 2 if (not first and mix is None and not final) else 1
    tm = n_seg * TM
    row = lambda width: pl.BlockSpec((tm, width), lambda i, j: (i, 0))
    if first:
        in_specs = [_ctx_rows(TM, D_MODEL), _lat_rows(TM, D_MODEL),
                    pl.BlockSpec((TM, D_MODEL), lambda i, j: (0, 0))]
        args = list(x)
    else:
        in_specs, args = [row(D_MODEL)], [x]
    mix_counts, mix_split = (), 0
    if mix is not None:
        ya, yb, w_out, w_index = mix
        counts = []
        for y in (ya, yb):
            parts = y if isinstance(y, tuple) else (y,)
            width = parts[0].shape[1]
            in_specs += [row(width)] if len(parts) == 1 else [_ctx_rows(TM, width), _lat_rows(TM, width)]
            args += list(parts)
            counts.append(len(parts))
        mix_counts = tuple(counts)
        mix_split = (ya[0] if isinstance(ya, tuple) else ya).shape[1]
        in_specs += [_mod_spec(layer, 5),
                     pl.BlockSpec((None, D_MODEL, D_MODEL), lambda i, j: (w_index, 0, 0),
                                  pipeline_mode=pl.Buffered(1))]
        args += [mods, w_out]
    in_specs.append(_layer_vec(layer))
    args.append(norm_w.reshape(DEPTH, 1, D_MODEL))
    for seg in range(n_seg):
        in_specs += [_mod_spec(layer, 3 * sub + k, TM, n_seg, seg) for k in range(3)]
        args += [mods, mods, mods]
    per_step = 2 if (n_seg == 1 and not final) else 1
    n_blocks = D_FF // TF
    for k in range(per_step):
        blk = lambda j, k=k: jnp.minimum(per_step * j + k, n_blocks - 1)
        in_specs += [pl.BlockSpec((None, D_MODEL, TF), lambda i, j, blk=blk: (layer, 0, blk(j))),
                     pl.BlockSpec((None, D_MODEL, TF), lambda i, j, blk=blk: (layer, 0, blk(j))),
                     pl.BlockSpec((None, TF, D_MODEL), lambda i, j, blk=blk: (layer, blk(j), 0))]
        args += [w_gate, w_up, w_down]
    scratch = [pltpu.VMEM((tm, D_MODEL), BF16), pltpu.VMEM((tm, D_MODEL), F32)]
    if first or mix_counts:
        scratch.append(pltpu.VMEM((TM, D_MODEL), F32))
    if final:
        in_specs.append(pl.BlockSpec((1, D_MODEL), lambda i, j: (0, 0)))
        args.append(final_norm[None])
        out_specs = [_ctx_rows(TM, D_MODEL), _lat_rows(TM, D_MODEL)]
        out_shape = [jax.ShapeDtypeStruct((CTX_TOK, D_MODEL), F32), jax.ShapeDtypeStruct((LAT_TOK, D_MODEL), F32)]
    else:
        out_specs = row(D_MODEL)
        out_shape = jax.ShapeDtypeStruct((N_TOK, D_MODEL), F32)
    return pl.pallas_call(
        functools.partial(_ffn_kernel, first=first, mix_counts=mix_counts, mix_split=mix_split, final=final,
                          n_seg=n_seg, per_step=per_step),
        grid=(N_TOK // tm, pl.cdiv(n_blocks, per_step)),
        in_specs=in_specs,
        out_specs=out_specs,
        out_shape=out_shape,
        scratch_shapes=scratch,
        compiler_params=_params("arbitrary", "arbitrary"),
        name=f"ffn{sub}_{layer}",
    )(*args)


def _mixin_kernel(x_ref, nw_ref, sh_ref, sc_ref, w_ref, *rest, segments, transposed):
    o_refs, wb_scr = rest[:-1], rest[-1]

    @pl.when(pl.program_id(0) == 0)
    def _():
        wb_scr[...] = w_ref[...].astype(BF16)

    h = _norm_mod(x_ref[...], nw_ref[...], sc_ref[...], sh_ref[...])
    for (lo, hi), o_ref in zip(segments, o_refs):
        if transposed:
            o_ref[...] = _dot_nt(h, wb_scr[lo:hi, :])
        else:
            o_ref[...] = _dot(h, wb_scr[:, lo:hi])
    if transposed:
        lo, hi = segments[-1]
        o_refs[-1][...] = _dot_nt(wb_scr[lo:hi, :], h)


def _mixin(x, mods, layer, norm_w, w_in, w_index, segments, tm, transposed=False):
    shape = w_in.shape[1:]
    out_specs = [pl.BlockSpec((tm, hi - lo), lambda i: (i, 0)) for lo, hi in segments]
    out_shape = [jax.ShapeDtypeStruct((N_TOK, hi - lo), F32) for lo, hi in segments]
    if transposed:
        lo, hi = segments[-1]
        out_specs.append(pl.BlockSpec((hi - lo, tm), lambda i: (0, i)))
        out_shape.append(jax.ShapeDtypeStruct((hi - lo, N_TOK), F32))
    return pl.pallas_call(
        functools.partial(_mixin_kernel, segments=segments, transposed=transposed),
        grid=(N_TOK // tm,),
        in_specs=[pl.BlockSpec((tm, D_MODEL), lambda i: (i, 0)),
                  _layer_vec(layer), _mod_spec(layer, 3, tm), _mod_spec(layer, 4, tm),
                  pl.BlockSpec((None,) + shape, lambda i: (w_index, 0, 0), pipeline_mode=pl.Buffered(1))],
        out_specs=out_specs,
        out_shape=out_shape,
        scratch_shapes=[pltpu.VMEM(shape, BF16)],
        compiler_params=_params("arbitrary"),
        name=f"mixin_{layer}",
    )(x, norm_w.reshape(DEPTH, 1, D_MODEL), mods, mods, w_in)


def _pool_kernel(p_ref, w_ref, scale_ref, o_ref):
    seq = jnp.where(pl.program_id(0) < CTX_TILES, SEQ, DEC_SEQ)
    p = p_ref[...]
    pos = lax.broadcasted_iota(jnp.int32, (TM, A_W), 0) & (seq - 1)

    def before(a, s):
        return jnp.where(pos >= s, pltpu.roll(a, s, axis=0), 0.0)

    def after(a, s):
        return jnp.where(pos < seq - s, pltpu.roll(a, TM - s, axis=0), 0.0)

    sums = []
    f, b = p, before(p, 1)
    for h in (size // 2 for size in POOL_SIZES):
        if h > 1:
            f = f + after(f, h // 2)
            b = b + before(b, h // 2)
        sums.append(f + b)
    group = lax.broadcasted_iota(jnp.int32, (TM, A_W), 1) >> (POOL_CH.bit_length() - 1)
    wsum = jnp.where(group == 0, sums[0], jnp.where(group == 1, sums[1], jnp.where(group == 2, sums[2], sums[3])))
    half = jnp.left_shift(1, group)
    cnt = (jnp.minimum(pos + half, seq) - jnp.maximum(pos - half, 0)).astype(F32)
    d = wsum / cnt - p
    o_ref[...] = (_dot(d, w_ref[...]) * scale_ref[...]).astype(o_ref.dtype)


def _pool(p_pool, w_bd, scale):
    return pl.pallas_call(
        _pool_kernel,
        grid=(N_ROW_TILES,),
        in_specs=[pl.BlockSpec((TM, A_W), lambda i: (i, 0)),
                  pl.BlockSpec((A_W, A_W), lambda i: (0, 0)),
                  pl.BlockSpec((1, A_W), lambda i: (0, 0))],
        out_specs=pl.BlockSpec((TM, A_W), lambda i: (i, 0)),
        out_shape=jax.ShapeDtypeStruct((N_TOK, A_W), BF16),
        compiler_params=_params("arbitrary"),
        name="pool",
    )(p_pool, w_bd, scale)


def _link_mask(r, c, s, upper):
    lg = s.bit_length() - 1
    same = (r >> (lg + 1)) == (c >> (lg + 1))
    r_half = (r >> lg) & 1
    c_half = (c >> lg) & 1
    return same & ((r_half == 0) & (c_half == 1) if upper else (r_half == 1) & (c_half == 0))


def _dn_kernel(q_ref, k_ref, v_ref, z_ref, g_ref, gt_ref, cq_ref, ck_ref, cv_ref, alog_ref, dtb_ref,
               alog_c_ref, dtb_c_ref, nw_ref, *rest, seq, heads, zero_init, state_slot):
    rest = list(rest)
    s0_ref = None if zero_init else rest.pop(0)
    if state_slot:
        rest.pop(0)
    o_ref = rest.pop(0)
    st_ref = rest.pop(0) if state_slot is not None else None
    gct_scr = rest.pop(0)
    h0 = pl.program_id(1) * heads
    n = DN_CHUNK
    n_blk = seq // n
    row = lax.broadcasted_iota(jnp.int32, (seq, LANES), 0)
    g_lane = lax.broadcasted_iota(jnp.int32, (seq, GATE_W), 1)
    r2 = lax.broadcasted_iota(jnp.int32, (n, n), 0)
    c2 = lax.broadcasted_iota(jnp.int32, (n, n), 1)
    eye = (r2 == c2).astype(F32)
    incl = (r2 >= c2, r2 <= c2)
    strict = (r2 > c2, r2 < c2)
    levels = [1 << b for b in range(n.bit_length() - 1)]
    link = {(s, d): _link_mask(r2, c2, s, d == 1) for s in levels for d in (0, 1)}
    units = [(hh, blk, d) for hh in range(heads) for blk in range(n_blk) for d in (0, 1)]

    def conv_silu(x, cw):
        prev = jnp.where(row >= 1, pltpu.roll(x, 1, axis=0), 0.0)
        nxt = jnp.where(row <= seq - 2, pltpu.roll(x, seq - 1, axis=0), 0.0)
        return _silu(prev * cw[0:1] + x * cw[1:2] + nxt * cw[2:3])

    def l2n(x):
        return x * lax.rsqrt(jnp.sum(x * x, axis=-1, keepdims=True) + EPS)

    def col(a, idx):
        return jnp.sum(jnp.where(g_lane == idx, a, 0.0), axis=1, keepdims=True)

    def log_decay(a, alog, dtb):
        xg = a + dtb
        return -jnp.exp(alog) * (jnp.maximum(xg, 0.0) + jnp.log1p(jnp.exp(-jnp.abs(xg))))

    gates = g_ref[...]
    beta_all = jax.nn.sigmoid(gates)
    g_all = log_decay(gates, alog_ref[...], dtb_ref[...])
    g_all_t = log_decay(gt_ref[...], alog_c_ref[...], dtb_c_ref[...])
    lower = incl[0].astype(BF16)
    upper = incl[1].astype(BF16)
    backward_rows = lax.broadcasted_iota(jnp.int32, (GATE_W, n), 0) >= 3 * DN_HEADS

    def tri_sums(a, left):
        rem, pre_sum, suf_sum = a, 0.0, 0.0
        for _ in range(3):
            piece = rem.astype(BF16)
            rem = rem - piece.astype(F32)
            if left:
                pre_sum = pre_sum + jnp.dot(lower, piece, preferred_element_type=F32)
                suf_sum = suf_sum + jnp.dot(upper, piece, preferred_element_type=F32)
            else:
                pre_sum = pre_sum + jnp.dot(piece, upper, preferred_element_type=F32)
                suf_sum = suf_sum + jnp.dot(piece, lower, preferred_element_type=F32)
        return pre_sum, suf_sum

    pre_blocks, suf_blocks = [], []
    for blk in range(n_blk):
        rs = slice(blk * n, (blk + 1) * n)
        p, s_ = tri_sums(g_all[rs, :], left=True)
        pre_blocks.append(p)
        suf_blocks.append(s_)
        p_t, s_t = tri_sums(g_all_t[:, rs], left=False)
        sums = jnp.where(backward_rows, s_t, p_t)
        for r in range(2 * DN_HEADS):
            gct_scr[r, :, rs] = sums[2 * DN_HEADS + r:2 * DN_HEADS + r + 1, :]
    pre = jnp.concatenate(pre_blocks, axis=0)
    suf = jnp.concatenate(suf_blocks, axis=0)

    qs, kn, vv, beta, gc = [], [], [], [], []
    for hh in range(heads):
        cs = slice(hh * LANES, (hh + 1) * LANES)
        qs.append(l2n(conv_silu(q_ref[:, cs], cq_ref[:, cs])) * (DN_DK ** -0.5))
        kn.append(l2n(conv_silu(k_ref[:, cs], ck_ref[:, cs])))
        vv.append(conv_silu(v_ref[:, cs], cv_ref[:, cs]))
        beta.append((col(beta_all, h0 + hh), col(beta_all, DN_HEADS + h0 + hh)))
        gc.append((col(pre, 2 * DN_HEADS + h0 + hh), col(suf, 3 * DN_HEADS + h0 + hh)))

    kk, qk = {}, {}
    for hh in range(heads):
        for blk in range(n_blk):
            rs = slice(blk * n, (blk + 1) * n)
            kk[hh, blk] = _dot_nt(kn[hh][rs], kn[hh][rs])
            qk[hh, blk] = _dot_nt(qs[hh][rs], kn[hh][rs])

    m, m_hi, t, a_in = {}, {}, {}, {}
    for u in units:
        hh, blk, d = u
        rs = slice(blk * n, (blk + 1) * n)
        g_lanes = gct_scr[d * DN_HEADS + h0 + hh, :, rs]
        decay = jnp.where(incl[d], jnp.exp(gc[hh][d][rs] - g_lanes), 0.0)
        m[u] = jnp.where(strict[d], beta[hh][d][rs] * kk[hh, blk] * decay, 0.0)
        m_hi[u] = m[u].astype(BF16)
        a_in[u] = (qk[hh, blk] * decay).astype(BF16)
        t[u] = eye - jnp.where(link[1, d], m[u], 0.0)

    for s in levels[1:]:
        tb, x = {}, {}
        for u in units:
            tb[u] = t[u].astype(BF16)
            c_s = jnp.where(link[s, u[2]], m_hi[u], jnp.zeros_like(m_hi[u]))
            x[u] = jnp.dot(c_s, tb[u], preferred_element_type=F32).astype(BF16)
        for u in units:
            t[u] = t[u] - jnp.dot(tb[u], x[u], preferred_element_type=F32)

    uw, q_dec, k_dec, g_last = {}, {}, {}, {}
    for u in units:
        hh, blk, d = u
        rs = slice(blk * n, (blk + 1) * n)
        gcol = gc[hh][d][rs]
        bcol = beta[hh][d][rs]
        e_g = jnp.exp(gcol)
        kb = kn[hh][rs]
        uw[u] = _dot(t[u], jnp.concatenate([vv[hh][rs] * bcol, kb * (bcol * e_g)], axis=1))
        g_last[u] = gcol[n - 1:n] if d == 0 else gcol[0:1]
        q_dec[u] = qs[hh][rs] * e_g
        k_dec[u] = kb * jnp.exp(g_last[u] - gcol)

    state = {}
    for hh in range(heads):
        for d in (0, 1):
            state[hh, d] = jnp.zeros((DN_DK, DN_DK), F32) if zero_init else s0_ref[d, hh]
    outs = {}
    for step in range(n_blk):
        chains = [(hh, step if d == 0 else n_blk - 1 - step, d) for hh in range(heads) for d in (0, 1)]
        ws_qs = {}
        for u in chains:
            ws_qs[u] = _dot(jnp.concatenate([uw[u][:, LANES:], q_dec[u]], axis=0), state[u[0], u[2]])
        for u in chains:
            v_new = (uw[u][:, :LANES] - ws_qs[u][0:n]).astype(BF16)
            outs[u] = ws_qs[u][n:] + jnp.dot(a_in[u], v_new, preferred_element_type=F32)
            state[u[0], u[2]] = state[u[0], u[2]] * jnp.exp(g_last[u]) + _dot_tn(k_dec[u], v_new)

    for hh in range(heads):
        cs = slice(hh * LANES, (hh + 1) * LANES)
        o = (jnp.concatenate([outs[hh, blk, 0] for blk in range(n_blk)], axis=0)
             + jnp.concatenate([outs[hh, blk, 1] for blk in range(n_blk)], axis=0))
        o = o * lax.rsqrt(jnp.mean(o * o, axis=-1, keepdims=True) + EPS) * nw_ref[...]
        o_ref[:, cs] = (o * _silu(z_ref[:, cs])).astype(o_ref.dtype)
        if state_slot == 0:
            st_ref[0, 0, hh] = state[hh, 0]
            st_ref[0, 1, hh] = state[hh, 1]
            st_ref[1:, :, hh] = jnp.zeros((N_EVEN - 1, 2, DN_DK, DN_DK), F32)
        elif state_slot:
            st_ref[0, hh] = state[hh, 0]
            st_ref[1, hh] = state[hh, 1]


def _deltanet(qkvz, gates, gates_t, conv_w, e, alog, dtb, norm_w, s0, seq, heads, n_seq, row_block0,
              state_slot=None, states=None):
    zero_init = s0 is None
    width = heads * LANES
    per = DN_HEADS // heads
    alog_row = jnp.concatenate([jnp.zeros((2 * DN_HEADS,), F32), alog.reshape(-1)])[None]
    dtb_row = jnp.concatenate([jnp.zeros((2 * DN_HEADS,), F32), dtb.reshape(-1)])[None]

    def pcol(cb):
        return pl.BlockSpec((seq, width), lambda b, h: (row_block0 + b, cb * per + h))

    def ccol(cb):
        return pl.BlockSpec((None, 3, width), lambda b, h: (e, 0, cb * per + h))

    grow = pl.BlockSpec((1, GATE_W), lambda b, h: (0, 0))
    gcol = pl.BlockSpec((GATE_W, 1), lambda b, h: (0, 0))
    in_specs = [pcol(COL_Q), pcol(COL_K), pcol(COL_V), pcol(COL_Z),
                pl.BlockSpec((seq, GATE_W), lambda b, h: (row_block0 + b, 0)),
                pl.BlockSpec((GATE_W, seq), lambda b, h: (0, row_block0 + b)),
                ccol(0), ccol(1), ccol(2), grow, grow, gcol, gcol,
                pl.BlockSpec((None, 1, LANES), lambda b, h: (e, 0, 0))]
    args = [qkvz, qkvz, qkvz, qkvz, gates, gates_t, conv_w, conv_w, conv_w, alog_row, dtb_row,
            alog_row.T, dtb_row.T, norm_w.reshape(N_EVEN, 1, LANES)]
    if not zero_init:
        in_specs.append(pl.BlockSpec((None, None, 2, heads, DN_DK, DN_DK), lambda b, h: (b, e, 0, h, 0, 0)))
        args.append(s0)
    aliases = {}
    if state_slot:
        aliases = {len(args): 1}
        in_specs.append(pl.BlockSpec(memory_space=pl.ANY))
        args.append(states)
    out_specs = [pl.BlockSpec((seq, width), lambda b, h: (b, h))]
    out_shape = [jax.ShapeDtypeStruct((n_seq * seq, B_W), BF16)]
    if state_slot is not None:
        if state_slot == 0:
            out_specs.append(pl.BlockSpec((None, N_EVEN, 2, heads, DN_DK, DN_DK), lambda b, h: (b, 0, 0, h, 0, 0)))
        else:
            out_specs.append(pl.BlockSpec((None, None, 2, heads, DN_DK, DN_DK),
                                          lambda b, h: (b, state_slot, 0, h, 0, 0)))
        out_shape.append(jax.ShapeDtypeStruct((n_seq, N_EVEN, 2, DN_HEADS, DN_DK, DN_DK), F32))
    res = pl.pallas_call(
        functools.partial(_dn_kernel, seq=seq, heads=heads, zero_init=zero_init, state_slot=state_slot),
        grid=(n_seq, per),
        in_specs=in_specs,
        out_specs=out_specs,
        out_shape=out_shape,
        scratch_shapes=[pltpu.VMEM((2 * DN_HEADS, 1, seq), F32)],
        input_output_aliases=aliases,
        compiler_params=_params("arbitrary", "arbitrary"),
        name=f"deltanet_{seq}",
    )(*args)
    return res if state_slot is not None else (res[0], None)


TS = 1024


def _sgu_kernel(u_ref, v_ref, nw_ref, ws_ref, bt_ref, o_ref):
    u = jax.nn.gelu(u_ref[...])
    v = jax.nn.gelu(v_ref[...])
    mu = jnp.mean(v, axis=-1, keepdims=True)
    vc = v - mu
    var = jnp.mean(vc * vc, axis=-1, keepdims=True)
    vn = vc * lax.rsqrt(var + EPS) * nw_ref[...]
    bt = bt_ref[...]
    for c in range(TS // SGU_CHUNK):
        rs = slice(c * SGU_CHUNK, (c + 1) * SGU_CHUNK)
        for hd in range(SGU_HEADS):
            cs = slice(hd * LANES, (hd + 1) * LANES)
            mixed = _dot(ws_ref[hd], vn[rs, cs]) + bt[:, hd:hd + 1]
            o_ref[rs, cs] = (u[rs, cs] * mixed).astype(o_ref.dtype)


def _sgu(uv, norm_w, w_s, b_t):
    return pl.pallas_call(
        _sgu_kernel,
        grid=(N_TOK // TS,),
        in_specs=[pl.BlockSpec((TS, C_W), lambda i: (i, 0)),
                  pl.BlockSpec((TS, C_W), lambda i: (i, 1)),
                  pl.BlockSpec((1, C_W), lambda i: (0, 0)),
                  pl.BlockSpec((SGU_HEADS, SGU_CHUNK, SGU_CHUNK), lambda i: (0, 0, 0)),
                  pl.BlockSpec((SGU_CHUNK, SGU_HEADS), lambda i: (0, 0))],
        out_specs=pl.BlockSpec((TS, C_W), lambda i: (i, 0)),
        out_shape=jax.ShapeDtypeStruct((N_TOK, C_W), BF16),
        compiler_params=_params("arbitrary"),
        name="sgu",
    )(uv, uv, norm_w, w_s, b_t)


def _fnet_kernel(f_ref, ch_hi_ref, ch_lo_ref, ctx_hi_ref, ctx_lo_ref, lat_hi_ref, lat_lo_ref, w_ref, o_ref):
    fcs = _dot3(_split(f_ref[...]), (ch_hi_ref[...], ch_lo_ref[...]))
    wb = w_ref[...].astype(BF16)
    ctx = pl.program_id(0) < CTX_TILES

    def finish(rows, hi_ref, lo_ref):
        stacked = jnp.concatenate([fcs[rows, :F_W], fcs[rows, F_W:]], axis=0)
        spec = _dot3((hi_ref[...], lo_ref[...]), _split(stacked))
        o_ref[rows, :] = jnp.dot(spec.astype(BF16), wb, preferred_element_type=F32).astype(o_ref.dtype)

    @pl.when(ctx)
    def _():
        for s in range(TM // SEQ):
            finish(slice(s * SEQ, (s + 1) * SEQ), ctx_hi_ref, ctx_lo_ref)

    @pl.when(jnp.logical_not(ctx))
    def _():
        finish(slice(0, DEC_SEQ), lat_hi_ref, lat_lo_ref)


def _fnet(f_all, tables, w_bd):
    const = lambda a: pl.BlockSpec(a.shape, lambda i: (0, 0))
    return pl.pallas_call(
        _fnet_kernel,
        grid=(N_ROW_TILES,),
        in_specs=[pl.BlockSpec((TM, F_W), lambda i: (i, 0))] + [const(t) for t in tables] + [const(w_bd)],
        out_specs=pl.BlockSpec((TM, F_W), lambda i: (i, 0)),
        out_shape=jax.ShapeDtypeStruct((N_TOK, F_W), BF16),
        compiler_params=_params("arbitrary"),
        name="fnet",
    )(f_all, *tables, w_bd)


def _grid_pos_embed(n_tok):
    rows = n_tok // GRID_W
    r = np.repeat(np.arange(rows, dtype=np.float64), GRID_W)
    col = np.tile(np.arange(GRID_W, dtype=np.float64), rows)
    quarter = D_MODEL // 4
    freq = np.exp(-math.log(10000.0) * np.arange(quarter, dtype=np.float64) / quarter)

    def emb(p):
        ang = p[:, None] * freq[None, :]
        return np.concatenate([np.sin(ang), np.cos(ang)], axis=-1)

    return np.concatenate([emb(r), emb(col)], axis=-1).astype(np.float32)


def _dft_tables(n):
    k = np.arange(n, dtype=np.int64)
    ang = ((k[:, None] * k[None, :]) % n).astype(np.float64) * (2.0 * math.pi / n)
    scale = n ** -0.5
    return np.cos(ang) * scale, np.sin(ang) * scale


def _host_split(a):
    hi = a.astype(np.float32).astype(BF16)
    lo = (a - hi.astype(np.float64)).astype(np.float32).astype(BF16)
    return hi, lo


def _fnet_tables():
    cc, sc = _dft_tables(FN_CH)
    eye = np.eye(F_W // FN_CH)
    tables = list(_host_split(np.concatenate([np.kron(eye, cc), np.kron(eye, sc)], axis=1)))
    for n in (SEQ, DEC_SEQ):
        cn, sn = _dft_tables(n)
        tables += list(_host_split(np.concatenate([cn, -sn], axis=1)))
    return tables


def _block_diag(blocks):
    g, a, b = blocks.shape
    eye = jnp.eye(g, dtype=blocks.dtype)
    return (eye[:, None, :, None] * blocks[:, :, None, :]).reshape(g * a, g * b)


def kernel(x_prompt, x_sample, state_delta, c, c_ctx, ffn1_norm, ffn1_w_gate, ffn1_w_up, ffn1_w_down,
           mix_norm, ffn2_norm, ffn2_w_gate, ffn2_w_up, ffn2_w_down, ada_w, ada_b, ev_w_in, ev_w_out,
           pool_w, pool_scale, dn_conv_w, dn_a_log, dn_dt_bias, dn_norm_w, od_w_in, od_w_out, sgu_norm,
           sgu_w, sgu_b, fnet_w, final_norm):
    cond8 = jnp.zeros((SUBLANES, D_MODEL), F32).at[0].set(c_ctx).at[1:1 + DEC_BATCH].set(c)
    mods = _ada(cond8, ada_w, ada_b)
    mods = mods[:, :1 + DEC_BATCH].reshape(DEPTH, 1 + DEC_BATCH, N_MOD, 1, D_MODEL).transpose(0, 2, 1, 3, 4)

    fnet_tables = _fnet_tables()
    groups = ((SEQ, BATCH, 0, DN_HEADS_PER_STEP[SEQ]),
              (DEC_SEQ, DEC_BATCH, CTX_TOK // DEC_SEQ, DN_HEADS_PER_STEP[DEC_SEQ]))
    even_segments = ((A_W, A_W + 4 * QK_W), (0, A_W), (A_W + 4 * QK_W, P_EVEN))
    odd_segments = ((0, 2 * C_W), (2 * C_W, P_ODD))

    x = (x_prompt.reshape(CTX_TOK, D_MODEL), x_sample.reshape(LAT_TOK, D_MODEL), _grid_pos_embed(DEC_SEQ))
    ev_w_in_t = jnp.swapaxes(ev_w_in, 1, 2)
    states = None
    for layer in range(DEPTH):
        x = _ffn(x, mods, layer, 0, ffn1_norm, ffn1_w_gate, ffn1_w_up, ffn1_w_down)
        if layer % 2 == 0:
            e = layer // 2
            qkvz, p_pool, gates, gates_t = _mixin(x, mods, layer, mix_norm, ev_w_in_t, e, even_segments,
                                                  TM_EVEN_PROJ, transposed=True)
            w_bd = _block_diag(pool_w[e])
            ya = _pool(p_pool, w_bd, pool_scale[e][None])
            yb = []
            for seq, n_seq, rb0, dn_heads in groups:
                ctx = rb0 == 0
                o, st = _deltanet(qkvz, gates, gates_t, dn_conv_w, e, dn_a_log[e], dn_dt_bias[e], dn_norm_w,
                                  None if ctx else state_delta, seq, dn_heads, n_seq, rb0,
                                  state_slot=e if ctx else None, states=states)
                yb.append(o)
                if ctx:
                    states = st
            mix = (ya, tuple(yb), ev_w_out, e)
        else:
            j = layer // 2
            uv, f_all = _mixin(x, mods, layer, mix_norm, od_w_in, j, odd_segments, TM)
            yc = _sgu(uv, sgu_norm[j][None], sgu_w[j], sgu_b[j].T)
            w_bd = _block_diag(fnet_w[j])
            yd = _fnet(f_all, fnet_tables, w_bd)
            mix = (yc, yd, od_w_out, j)
        x = _ffn(x, mods, layer, 2, ffn2_norm, ffn2_w_gate, ffn2_w_up, ffn2_w_down, mix=mix,
                 final_norm=final_norm if layer == DEPTH - 1 else None)

    y_prompt, y_sample = x
    return (y_prompt.reshape(BATCH, SEQ, D_MODEL), y_sample.reshape(DEC_BATCH, DEC_SEQ, D_MODEL), states)
```

```python
import functools
import math

import numpy as np
import jax
import jax.numpy as jnp
from jax import lax
from jax.experimental import pallas as pl
from jax.experimental.pallas import tpu as pltpu

F32 = jnp.float32
BF16 = jnp.bfloat16

D_MODEL = 1024
BATCH = 16
SEQ = 256
DEPTH = 4
DEC_BATCH = 2
DEC_SEQ = 1024
GRID_W = 64
EPS = 1e-6
D_FF = 2816
N_MOD = 9
N_EVEN = (DEPTH + 1) // 2
POOL_SIZES = (2, 4, 8, 16)
POOL_CH = 64
A_W = 256
DN_HEADS = 6
DN_DK = 128
QK_W = DN_HEADS * DN_DK
B_W = QK_W
GATE_W = 4 * DN_HEADS
P_EVEN = A_W + 4 * QK_W + GATE_W
SGU_CHUNK = 128
SGU_HEADS = 6
C_W = 768
FN_CH = 64
F_W = 256
P_ODD = 2 * C_W + F_W

CTX_TOK = BATCH * SEQ
LAT_TOK = DEC_BATCH * DEC_SEQ
N_TOK = CTX_TOK + LAT_TOK

LANES = 128
SUBLANES = 8
VMEM_LIMIT = 56 * 1024 * 1024

TM = 1024
N_ROW_TILES = N_TOK // TM
CTX_TILES = CTX_TOK // TM
TF = 256
TM_EVEN_PROJ = 512
ADA_TN = 2304
DN_CHUNK = 128
DN_HEADS_PER_STEP = {SEQ: DN_HEADS, DEC_SEQ: 3}
COL_Q, COL_K, COL_V, COL_Z = 0, 1, 2, 3


def _cond_of_tile(i, tm):
    return jnp.where(i < CTX_TOK // tm, 0, 1 + (i - CTX_TOK // tm) * tm // DEC_SEQ)


def _silu(x):
    half = 0.5 * x
    return half + half * jnp.tanh(half)


def _dot(a, b):
    return jnp.dot(a.astype(BF16), b.astype(BF16), preferred_element_type=F32)


def _dot_nt(a, b):
    return lax.dot_general(a.astype(BF16), b.astype(BF16), (((1,), (1,)), ((), ())),
                           preferred_element_type=F32)


def _dot_tn(a, b):
    return lax.dot_general(a.astype(BF16), b.astype(BF16), (((0,), (0,)), ((), ())),
                           preferred_element_type=F32)


def _split(a):
    hi = a.astype(BF16)
    lo = (a - hi.astype(F32)).astype(BF16)
    return hi, lo


def _dot3(a, b):
    ah, al = a
    bh, bl = b
    return (jnp.dot(ah, bh, preferred_element_type=F32)
            + (jnp.dot(al, bh, preferred_element_type=F32)
               + jnp.dot(ah, bl, preferred_element_type=F32)))


def _rms(x, w):
    return x * lax.rsqrt(jnp.mean(x * x, axis=-1, keepdims=True) + EPS) * w


def _norm_mod(x, nw, scale, shift):
    r = lax.rsqrt(jnp.mean(x * x, axis=-1, keepdims=True) + EPS)
    return ((x * r) * (nw * (1.0 + scale)) + shift).astype(BF16)


def _params(*sem):
    return pltpu.CompilerParams(dimension_semantics=sem, vmem_limit_bytes=VMEM_LIMIT)


def _ctx_rows(tm, width):
    return pl.BlockSpec((tm, width), lambda i, *_: (jnp.minimum(i, CTX_TOK // tm - 1), 0))


def _lat_rows(tm, width):
    return pl.BlockSpec((tm, width), lambda i, *_: (jnp.maximum(i - CTX_TOK // tm, 0), 0))


def _ada_kernel(c_ref, w_ref, b_ref, o_ref):
    o_ref[0] = _dot(_silu(c_ref[...]), w_ref[0]) + b_ref[0]


def _ada(cond8, ada_w, ada_b):
    tn = ADA_TN
    n = N_MOD * D_MODEL
    return pl.pallas_call(
        _ada_kernel,
        grid=(DEPTH, n // tn),
        in_specs=[pl.BlockSpec((SUBLANES, D_MODEL), lambda l, j: (0, 0)),
                  pl.BlockSpec((1, D_MODEL, tn), lambda l, j: (l, 0, j)),
                  pl.BlockSpec((1, 1, tn), lambda l, j: (l, 0, j))],
        out_specs=pl.BlockSpec((1, SUBLANES, tn), lambda l, j: (l, 0, j)),
        out_shape=jax.ShapeDtypeStruct((DEPTH, SUBLANES, n), F32),
        compiler_params=_params("arbitrary", "arbitrary"),
        name="ada",
    )(cond8, ada_w, ada_b.reshape(DEPTH, 1, n))


def _mod_spec(layer, which, tm=TM, n_seg=1, seg=0):
    return pl.BlockSpec((None, None, None, 1, D_MODEL),
                        lambda i, *_: (layer, which, _cond_of_tile(i * n_seg + seg, tm), 0, 0))


def _layer_vec(layer):
    return pl.BlockSpec((None, 1, D_MODEL), lambda i, *_: (layer, 0, 0))


def _ffn_kernel(*refs, first, mix_counts, mix_split, final, n_seg, per_step):
    refs = list(refs)
    take = lambda k: [refs.pop(0) for _ in range(k)]
    if first:
        xp_ref, xs_ref, pos_ref = take(3)
    else:
        (x_in_ref,) = take(1)
    if mix_counts:
        mix_groups = [take(c) for c in mix_counts]
        g2_ref, wo_ref = take(2)
    (nw_ref,) = take(1)
    seg_mods = [take(3) for _ in range(n_seg)]
    w_blocks = [take(3) for _ in range(per_step)]
    if final:
        fn_ref, oc_ref, ol_ref = take(3)
    else:
        (o_ref,) = take(1)
    (h_scr,) = take(1)
    acc_scr = o_ref if n_seg > 1 else refs.pop(0)
    x_ref = refs.pop(0) if (first or mix_counts) else x_in_ref
    i = pl.program_id(0)
    j = pl.program_id(1)
    ctx = i < CTX_TILES

    @pl.when(j == 0)
    def _():
        if first:
            @pl.when(ctx)
            def _():
                x_ref[...] = xp_ref[...]

            @pl.when(jnp.logical_not(ctx))
            def _():
                x_ref[...] = xs_ref[...] + pos_ref[...]
        elif mix_counts:
            def pick(rs):
                return rs[0][...] if len(rs) == 1 else jnp.where(ctx, rs[0][...], rs[1][...])

            mix = (_dot(pick(mix_groups[0]), wo_ref[0:mix_split, :])
                   + _dot(pick(mix_groups[1]), wo_ref[mix_split:, :]))
            x_ref[...] = x_in_ref[...] + g2_ref[...] * mix

        for s, (sh_ref, sc_ref, _) in enumerate(seg_mods):
            rows = slice(s * TM, (s + 1) * TM)
            h_scr[rows, :] = _norm_mod(x_ref[rows, :], nw_ref[...], sc_ref[...], sh_ref[...])
        acc_scr[...] = jnp.zeros_like(acc_scr)

    def accumulate(blocks):
        h = h_scr[...]
        total = None
        for wg_ref, wu_ref, wd_ref in blocks:
            g = jnp.dot(h, wg_ref[...].astype(BF16), preferred_element_type=F32)
            u = jnp.dot(h, wu_ref[...].astype(BF16), preferred_element_type=F32)
            part = _dot(_silu(g) * u, wd_ref[...])
            total = part if total is None else total + part
        acc_scr[...] += total

    last = pl.num_programs(1) - 1
    if per_step == 1:
        accumulate(w_blocks)
    else:
        @pl.when(j < last)
        def _():
            accumulate(w_blocks)

        @pl.when(j == last)
        def _():
            accumulate(w_blocks[:1])

    @pl.when(j == last)
    def _():
        if final:
            y = _rms(x_ref[...] + 0.5 * seg_mods[0][2][...] * acc_scr[...], fn_ref[...])

            @pl.when(ctx)
            def _():
                oc_ref[...] = y

            @pl.when(jnp.logical_not(ctx))
            def _():
                ol_ref[...] = y
        else:
            for s, (_, _, gt_ref) in enumerate(seg_mods):
                rows = slice(s * TM, (s + 1) * TM)
                o_ref[rows, :] = x_ref[rows, :] + 0.5 * gt_ref[...] * acc_scr[rows, :]


def _ffn(x, mods, layer, sub, norm_w, w_gate, w_up, w_down, mix=None, final_norm=None):
    first = isinstance(x, tuple)
    final = final_norm is not None
    n_seg = 2 if (not first and mix is None and not final) else 1
    tm = n_seg * TM
    row = lambda width: pl.BlockSpec((tm, width), lambda i, j: (i, 0))
    if first:
        in_specs = [_ctx_rows(TM, D_MODEL), _lat_rows(TM, D_MODEL),
                    pl.BlockSpec((TM, D_MODEL), lambda i, j: (0, 0))]
        args = list(x)
    else:
        in_specs, args = [row(D_MODEL)], [x]
    mix_counts, mix_split = (), 0
    if mix is not None:
        ya, yb, w_out, w_index = mix
        counts = []
        for y in (ya, yb):
            parts = y if isinstance(y, tuple) else (y,)
            width = parts[0].shape[1]
            in_specs += [row(width)] if len(parts) == 1 else [_ctx_rows(TM, width), _lat_rows(TM, width)]
            args += list(parts)
            counts.append(len(parts))
        mix_counts = tuple(counts)
        mix_split = (ya[0] if isinstance(ya, tuple) else ya).shape[1]
        in_specs += [_mod_spec(layer, 5),
                     pl.BlockSpec((None, D_MODEL, D_MODEL), lambda i, j: (w_index, 0, 0),
                                  pipeline_mode=pl.Buffered(1))]
        args += [mods, w_out]
    in_specs.append(_layer_vec(layer))
    args.append(norm_w.reshape(DEPTH, 1, D_MODEL))
    for seg in range(n_seg):
        in_specs += [_mod_spec(layer, 3 * sub + k, TM, n_seg, seg) for k in range(3)]
        args += [mods, mods, mods]
    per_step = 1 if final else 2
    n_blocks = D_FF // TF
    for k in range(per_step):
        blk = lambda j, k=k: jnp.minimum(per_step * j + k, n_blocks - 1)
        in_specs += [pl.BlockSpec((None, D_MODEL, TF), lambda i, j, blk=blk: (layer, 0, blk(j))),
                     pl.BlockSpec((None, D_MODEL, TF), lambda i, j, blk=blk: (layer, 0, blk(j))),
                     pl.BlockSpec((None, TF, D_MODEL), lambda i, j, blk=blk: (layer, blk(j), 0))]
        args += [w_gate, w_up, w_down]
    scratch = [pltpu.VMEM((tm, D_MODEL), BF16)]
    if n_seg == 1:
        scratch.append(pltpu.VMEM((tm, D_MODEL), F32))
    if first or mix_counts:
        scratch.append(pltpu.VMEM((TM, D_MODEL), F32))
    if final:
        in_specs.append(pl.BlockSpec((1, D_MODEL), lambda i, j: (0, 0)))
        args.append(final_norm[None])
        out_specs = [_ctx_rows(TM, D_MODEL), _lat_rows(TM, D_MODEL)]
        out_shape = [jax.ShapeDtypeStruct((CTX_TOK, D_MODEL), F32), jax.ShapeDtypeStruct((LAT_TOK, D_MODEL), F32)]
    else:
        out_specs = row(D_MODEL)
        out_shape = jax.ShapeDtypeStruct((N_TOK, D_MODEL), F32)
    return pl.pallas_call(
        functools.partial(_ffn_kernel, first=first, mix_counts=mix_counts, mix_split=mix_split, final=final,
                          n_seg=n_seg, per_step=per_step),
        grid=(N_TOK // tm, pl.cdiv(n_blocks, per_step)),
        in_specs=in_specs,
        out_specs=out_specs,
        out_shape=out_shape,
        scratch_shapes=scratch,
        compiler_params=_params("arbitrary", "arbitrary"),
        name=f"ffn{sub}_{layer}",
    )(*args)


def _mixin_kernel(x_ref, nw_ref, sh_ref, sc_ref, w_ref, *rest, segments, transposed):
    o_refs, wb_scr = rest[:-1], rest[-1]

    @pl.when(pl.program_id(0) == 0)
    def _():
        wb_scr[...] = w_ref[...].astype(BF16)

    h = _norm_mod(x_ref[...], nw_ref[...], sc_ref[...], sh_ref[...])
    for (lo, hi), o_ref in zip(segments, o_refs):
        if transposed:
            o_ref[...] = _dot_nt(h, wb_scr[lo:hi, :])
        else:
            o_ref[...] = _dot(h, wb_scr[:, lo:hi])
    if transposed:
        lo, hi = segments[-1]
        o_refs[-1][...] = _dot_nt(wb_scr[lo:hi, :], h)


def _mixin(x, mods, layer, norm_w, w_in, w_index, segments, tm, transposed=False):
    shape = w_in.shape[1:]
    out_specs = [pl.BlockSpec((tm, hi - lo), lambda i: (i, 0)) for lo, hi in segments]
    out_shape = [jax.ShapeDtypeStruct((N_TOK, hi - lo), F32) for lo, hi in segments]
    if transposed:
        lo, hi = segments[-1]
        out_specs.append(pl.BlockSpec((hi - lo, tm), lambda i: (0, i)))
        out_shape.append(jax.ShapeDtypeStruct((hi - lo, N_TOK), F32))
    return pl.pallas_call(
        functools.partial(_mixin_kernel, segments=segments, transposed=transposed),
        grid=(N_TOK // tm,),
        in_specs=[pl.BlockSpec((tm, D_MODEL), lambda i: (i, 0)),
                  _layer_vec(layer), _mod_spec(layer, 3, tm), _mod_spec(layer, 4, tm),
                  pl.BlockSpec((None,) + shape, lambda i: (w_index, 0, 0), pipeline_mode=pl.Buffered(1))],
        out_specs=out_specs,
        out_shape=out_shape,
        scratch_shapes=[pltpu.VMEM(shape, BF16)],
        compiler_params=_params("arbitrary"),
        name=f"mixin_{layer}",
    )(x, norm_w.reshape(DEPTH, 1, D_MODEL), mods, mods, w_in)


def _pool_kernel(p_ref, w_ref, scale_ref, o_ref):
    seq = jnp.where(pl.program_id(0) < CTX_TILES, SEQ, DEC_SEQ)
    p = p_ref[...]
    pos = lax.broadcasted_iota(jnp.int32, (TM, A_W), 0) & (seq - 1)

    def before(a, s):
        return jnp.where(pos >= s, pltpu.roll(a, s, axis=0), 0.0)

    def after(a, s):
        return jnp.where(pos < seq - s, pltpu.roll(a, TM - s, axis=0), 0.0)

    sums = []
    f, b = p, before(p, 1)
    for h in (size // 2 for size in POOL_SIZES):
        if h > 1:
            f = f + after(f, h // 2)
            b = b + before(b, h // 2)
        sums.append(f + b)
    group = lax.broadcasted_iota(jnp.int32, (TM, A_W), 1) >> (POOL_CH.bit_length() - 1)
    wsum = jnp.where(group == 0, sums[0], jnp.where(group == 1, sums[1], jnp.where(group == 2, sums[2], sums[3])))
    half = jnp.left_shift(1, group)
    cnt = (jnp.minimum(pos + half, seq) - jnp.maximum(pos - half, 0)).astype(F32)
    d = wsum / cnt - p
    o_ref[...] = (_dot(d, w_ref[...]) * scale_ref[...]).astype(o_ref.dtype)


def _pool(p_pool, w_bd, scale):
    return pl.pallas_call(
        _pool_kernel,
        grid=(N_ROW_TILES,),
        in_specs=[pl.BlockSpec((TM, A_W), lambda i: (i, 0)),
                  pl.BlockSpec((A_W, A_W), lambda i: (0, 0)),
                  pl.BlockSpec((1, A_W), lambda i: (0, 0))],
        out_specs=pl.BlockSpec((TM, A_W), lambda i: (i, 0)),
        out_shape=jax.ShapeDtypeStruct((N_TOK, A_W), BF16),
        compiler_params=_params("arbitrary"),
        name="pool",
    )(p_pool, w_bd, scale)


def _link_mask(r, c, s, upper):
    lg = s.bit_length() - 1
    same = (r >> (lg + 1)) == (c >> (lg + 1))
    r_half = (r >> lg) & 1
    c_half = (c >> lg) & 1
    return same & ((r_half == 0) & (c_half == 1) if upper else (r_half == 1) & (c_half == 0))


def _dn_kernel(q_ref, k_ref, v_ref, z_ref, g_ref, gt_ref, cq_ref, ck_ref, cv_ref, alog_ref, dtb_ref,
               alog_c_ref, dtb_c_ref, nw_ref, *rest, seq, heads, zero_init, state_slot):
    rest = list(rest)
    s0_ref = None if zero_init else rest.pop(0)
    if state_slot:
        rest.pop(0)
    o_ref = rest.pop(0)
    st_ref = rest.pop(0) if state_slot is not None else None
    gct_scr = rest.pop(0)
    h0 = pl.program_id(1) * heads
    n = DN_CHUNK
    n_blk = seq // n
    row = lax.broadcasted_iota(jnp.int32, (seq, LANES), 0)
    g_lane = lax.broadcasted_iota(jnp.int32, (seq, GATE_W), 1)
    r2 = lax.broadcasted_iota(jnp.int32, (n, n), 0)
    c2 = lax.broadcasted_iota(jnp.int32, (n, n), 1)
    eye = (r2 == c2).astype(F32)
    incl = (r2 >= c2, r2 <= c2)
    strict = (r2 > c2, r2 < c2)
    levels = [1 << b for b in range(n.bit_length() - 1)]
    link = {(s, d): _link_mask(r2, c2, s, d == 1) for s in levels for d in (0, 1)}
    units = [(hh, blk, d) for hh in range(heads) for blk in range(n_blk) for d in (0, 1)]

    def conv_silu(x, cw):
        prev = jnp.where(row >= 1, pltpu.roll(x, 1, axis=0), 0.0)
        nxt = jnp.where(row <= seq - 2, pltpu.roll(x, seq - 1, axis=0), 0.0)
        return _silu(prev * cw[0:1] + x * cw[1:2] + nxt * cw[2:3])

    def l2n(x):
        return x * lax.rsqrt(jnp.sum(x * x, axis=-1, keepdims=True) + EPS)

    def col(a, idx):
        return jnp.sum(jnp.where(g_lane == idx, a, 0.0), axis=1, keepdims=True)

    def log_decay(a, alog, dtb):
        xg = a + dtb
        return -jnp.exp(alog) * (jnp.maximum(xg, 0.0) + jnp.log1p(jnp.exp(-jnp.abs(xg))))

    gates = g_ref[...]
    beta_all = jax.nn.sigmoid(gates)
    g_all = log_decay(gates, alog_ref[...], dtb_ref[...])
    g_all_t = log_decay(gt_ref[...], alog_c_ref[...], dtb_c_ref[...])
    lower = incl[0].astype(BF16)
    upper = incl[1].astype(BF16)
    backward_rows = lax.broadcasted_iota(jnp.int32, (GATE_W, n), 0) >= 3 * DN_HEADS

    def tri_sums(a, left):
        rem, pre_sum, suf_sum = a, 0.0, 0.0
        for _ in range(3):
            piece = rem.astype(BF16)
            rem = rem - piece.astype(F32)
            if left:
                pre_sum = pre_sum + jnp.dot(lower, piece, preferred_element_type=F32)
                suf_sum = suf_sum + jnp.dot(upper, piece, preferred_element_type=F32)
            else:
                pre_sum = pre_sum + jnp.dot(piece, upper, preferred_element_type=F32)
                suf_sum = suf_sum + jnp.dot(piece, lower, preferred_element_type=F32)
        return pre_sum, suf_sum

    pre_blocks, suf_blocks = [], []
    for blk in range(n_blk):
        rs = slice(blk * n, (blk + 1) * n)
        p, s_ = tri_sums(g_all[rs, :], left=True)
        pre_blocks.append(p)
        suf_blocks.append(s_)
        p_t, s_t = tri_sums(g_all_t[:, rs], left=False)
        sums = jnp.where(backward_rows, s_t, p_t)
        for r in range(2 * DN_HEADS):
            gct_scr[r, :, rs] = sums[2 * DN_HEADS + r:2 * DN_HEADS + r + 1, :]
    pre = jnp.concatenate(pre_blocks, axis=0)
    suf = jnp.concatenate(suf_blocks, axis=0)

    qs, kn, vv, beta, gc = [], [], [], [], []
    for hh in range(heads):
        cs = slice(hh * LANES, (hh + 1) * LANES)
        qs.append(l2n(conv_silu(q_ref[:, cs], cq_ref[:, cs])) * (DN_DK ** -0.5))
        kn.append(l2n(conv_silu(k_ref[:, cs], ck_ref[:, cs])))
        vv.append(conv_silu(v_ref[:, cs], cv_ref[:, cs]))
        beta.append((col(beta_all, h0 + hh), col(beta_all, DN_HEADS + h0 + hh)))
        gc.append((col(pre, 2 * DN_HEADS + h0 + hh), col(suf, 3 * DN_HEADS + h0 + hh)))

    kk, qk = {}, {}
    for hh in range(heads):
        for blk in range(n_blk):
            rs = slice(blk * n, (blk + 1) * n)
            kk[hh, blk] = _dot_nt(kn[hh][rs], kn[hh][rs])
            qk[hh, blk] = _dot_nt(qs[hh][rs], kn[hh][rs])

    m, m_hi, t, a_in = {}, {}, {}, {}
    for u in units:
        hh, blk, d = u
        rs = slice(blk * n, (blk + 1) * n)
        g_lanes = gct_scr[d * DN_HEADS + h0 + hh, :, rs]
        decay = jnp.where(incl[d], jnp.exp(gc[hh][d][rs] - g_lanes), 0.0)
        m[u] = jnp.where(strict[d], beta[hh][d][rs] * kk[hh, blk] * decay, 0.0)
        m_hi[u] = m[u].astype(BF16)
        a_in[u] = (qk[hh, blk] * decay).astype(BF16)
        t[u] = eye - jnp.where(link[1, d], m[u], 0.0)

    for s in levels[1:]:
        tb, x = {}, {}
        for u in units:
            tb[u] = t[u].astype(BF16)
            c_s = jnp.where(link[s, u[2]], m_hi[u], jnp.zeros_like(m_hi[u]))
            x[u] = jnp.dot(c_s, tb[u], preferred_element_type=F32).astype(BF16)
        for u in units:
            t[u] = t[u] - jnp.dot(tb[u], x[u], preferred_element_type=F32)

    uw, q_dec, k_dec, g_last = {}, {}, {}, {}
    for u in units:
        hh, blk, d = u
        rs = slice(blk * n, (blk + 1) * n)
        gcol = gc[hh][d][rs]
        bcol = beta[hh][d][rs]
        e_g = jnp.exp(gcol)
        kb = kn[hh][rs]
        uw[u] = _dot(t[u], jnp.concatenate([vv[hh][rs] * bcol, kb * (bcol * e_g)], axis=1))
        g_last[u] = gcol[n - 1:n] if d == 0 else gcol[0:1]
        q_dec[u] = qs[hh][rs] * e_g
        k_dec[u] = kb * jnp.exp(g_last[u] - gcol)

    state = {}
    for hh in range(heads):
        for d in (0, 1):
            state[hh, d] = jnp.zeros((DN_DK, DN_DK), F32) if zero_init else s0_ref[d, hh]
    outs = {}
    for step in range(n_blk):
        chains = [(hh, step if d == 0 else n_blk - 1 - step, d) for hh in range(heads) for d in (0, 1)]
        ws_qs = {}
        for u in chains:
            ws_qs[u] = _dot(jnp.concatenate([uw[u][:, LANES:], q_dec[u]], axis=0), state[u[0], u[2]])
        for u in chains:
            v_new = (uw[u][:, :LANES] - ws_qs[u][0:n]).astype(BF16)
            outs[u] = ws_qs[u][n:] + jnp.dot(a_in[u], v_new, preferred_element_type=F32)
            state[u[0], u[2]] = state[u[0], u[2]] * jnp.exp(g_last[u]) + _dot_tn(k_dec[u], v_new)

    for hh in range(heads):
        cs = slice(hh * LANES, (hh + 1) * LANES)
        o = (jnp.concatenate([outs[hh, blk, 0] for blk in range(n_blk)], axis=0)
             + jnp.concatenate([outs[hh, blk, 1] for blk in range(n_blk)], axis=0))
        o = o * lax.rsqrt(jnp.mean(o * o, axis=-1, keepdims=True) + EPS) * nw_ref[...]
        o_ref[:, cs] = (o * _silu(z_ref[:, cs])).astype(o_ref.dtype)
        if state_slot == 0:
            st_ref[0, 0, hh] = state[hh, 0]
            st_ref[0, 1, hh] = state[hh, 1]
            st_ref[1:, :, hh] = jnp.zeros((N_EVEN - 1, 2, DN_DK, DN_DK), F32)
        elif state_slot:
            st_ref[0, hh] = state[hh, 0]
            st_ref[1, hh] = state[hh, 1]


def _deltanet(qkvz, gates, gates_t, conv_w, e, alog, dtb, norm_w, s0, seq, heads, n_seq, row_block0,
              state_slot=None, states=None):
    zero_init = s0 is None
    width = heads * LANES
    per = DN_HEADS // heads
    alog_row = jnp.concatenate([jnp.zeros((2 * DN_HEADS,), F32), alog.reshape(-1)])[None]
    dtb_row = jnp.concatenate([jnp.zeros((2 * DN_HEADS,), F32), dtb.reshape(-1)])[None]

    def pcol(cb):
        return pl.BlockSpec((seq, width), lambda b, h: (row_block0 + b, cb * per + h))

    def ccol(cb):
        return pl.BlockSpec((None, 3, width), lambda b, h: (e, 0, cb * per + h))

    grow = pl.BlockSpec((1, GATE_W), lambda b, h: (0, 0))
    gcol = pl.BlockSpec((GATE_W, 1), lambda b, h: (0, 0))
    in_specs = [pcol(COL_Q), pcol(COL_K), pcol(COL_V), pcol(COL_Z),
                pl.BlockSpec((seq, GATE_W), lambda b, h: (row_block0 + b, 0)),
                pl.BlockSpec((GATE_W, seq), lambda b, h: (0, row_block0 + b)),
                ccol(0), ccol(1), ccol(2), grow, grow, gcol, gcol,
                pl.BlockSpec((None, 1, LANES), lambda b, h: (e, 0, 0))]
    args = [qkvz, qkvz, qkvz, qkvz, gates, gates_t, conv_w, conv_w, conv_w, alog_row, dtb_row,
            alog_row.T, dtb_row.T, norm_w.reshape(N_EVEN, 1, LANES)]
    if not zero_init:
        in_specs.append(pl.BlockSpec((None, None, 2, heads, DN_DK, DN_DK), lambda b, h: (b, e, 0, h, 0, 0)))
        args.append(s0)
    aliases = {}
    if state_slot:
        aliases = {len(args): 1}
        in_specs.append(pl.BlockSpec(memory_space=pl.ANY))
        args.append(states)
    out_specs = [pl.BlockSpec((seq, width), lambda b, h: (b, h))]
    out_shape = [jax.ShapeDtypeStruct((n_seq * seq, B_W), BF16)]
    if state_slot is not None:
        if state_slot == 0:
            out_specs.append(pl.BlockSpec((None, N_EVEN, 2, heads, DN_DK, DN_DK), lambda b, h: (b, 0, 0, h, 0, 0)))
        else:
            out_specs.append(pl.BlockSpec((None, None, 2, heads, DN_DK, DN_DK),
                                          lambda b, h: (b, state_slot, 0, h, 0, 0)))
        out_shape.append(jax.ShapeDtypeStruct((n_seq, N_EVEN, 2, DN_HEADS, DN_DK, DN_DK), F32))
    res = pl.pallas_call(
        functools.partial(_dn_kernel, seq=seq, heads=heads, zero_init=zero_init, state_slot=state_slot),
        grid=(n_seq, per),
        in_specs=in_specs,
        out_specs=out_specs,
        out_shape=out_shape,
        scratch_shapes=[pltpu.VMEM((2 * DN_HEADS, 1, seq), F32)],
        input_output_aliases=aliases,
        compiler_params=_params("arbitrary", "arbitrary"),
        name=f"deltanet_{seq}",
    )(*args)
    return res if state_slot is not None else (res[0], None)


TS = 1024


def _sgu_kernel(u_ref, v_ref, nw_ref, ws_ref, bt_ref, o_ref):
    u = jax.nn.gelu(u_ref[...])
    v = jax.nn.gelu(v_ref[...])
    mu = jnp.mean(v, axis=-1, keepdims=True)
    vc = v - mu
    var = jnp.mean(vc * vc, axis=-1, keepdims=True)
    vn = vc * lax.rsqrt(var + EPS) * nw_ref[...]
    bt = bt_ref[...]
    for c in range(TS // SGU_CHUNK):
        rs = slice(c * SGU_CHUNK, (c + 1) * SGU_CHUNK)
        for hd in range(SGU_HEADS):
            cs = slice(hd * LANES, (hd + 1) * LANES)
            mixed = _dot(ws_ref[hd], vn[rs, cs]) + bt[:, hd:hd + 1]
            o_ref[rs, cs] = (u[rs, cs] * mixed).astype(o_ref.dtype)


def _sgu(uv, norm_w, w_s, b_t):
    return pl.pallas_call(
        _sgu_kernel,
        grid=(N_TOK // TS,),
        in_specs=[pl.BlockSpec((TS, C_W), lambda i: (i, 0)),
                  pl.BlockSpec((TS, C_W), lambda i: (i, 1)),
                  pl.BlockSpec((1, C_W), lambda i: (0, 0)),
                  pl.BlockSpec((SGU_HEADS, SGU_CHUNK, SGU_CHUNK), lambda i: (0, 0, 0)),
                  pl.BlockSpec((SGU_CHUNK, SGU_HEADS), lambda i: (0, 0))],
        out_specs=pl.BlockSpec((TS, C_W), lambda i: (i, 0)),
        out_shape=jax.ShapeDtypeStruct((N_TOK, C_W), BF16),
        compiler_params=_params("arbitrary"),
        name="sgu",
    )(uv, uv, norm_w, w_s, b_t)


def _fnet_kernel(f_ref, ch_hi_ref, ch_lo_ref, ctx_hi_ref, ctx_lo_ref, lat_hi_ref, lat_lo_ref, w_ref, o_ref):
    fcs = _dot3(_split(f_ref[...]), (ch_hi_ref[...], ch_lo_ref[...]))
    wb = w_ref[...].astype(BF16)
    ctx = pl.program_id(0) < CTX_TILES

    def finish(rows, hi_ref, lo_ref):
        stacked = jnp.concatenate([fcs[rows, :F_W], fcs[rows, F_W:]], axis=0)
        spec = _dot3((hi_ref[...], lo_ref[...]), _split(stacked))
        o_ref[rows, :] = jnp.dot(spec.astype(BF16), wb, preferred_element_type=F32).astype(o_ref.dtype)

    @pl.when(ctx)
    def _():
        for s in range(TM // SEQ):
            finish(slice(s * SEQ, (s + 1) * SEQ), ctx_hi_ref, ctx_lo_ref)

    @pl.when(jnp.logical_not(ctx))
    def _():
        finish(slice(0, DEC_SEQ), lat_hi_ref, lat_lo_ref)


def _fnet(f_all, tables, w_bd):
    const = lambda a: pl.BlockSpec(a.shape, lambda i: (0, 0))
    return pl.pallas_call(
        _fnet_kernel,
        grid=(N_ROW_TILES,),
        in_specs=[pl.BlockSpec((TM, F_W), lambda i: (i, 0))] + [const(t) for t in tables] + [const(w_bd)],
        out_specs=pl.BlockSpec((TM, F_W), lambda i: (i, 0)),
        out_shape=jax.ShapeDtypeStruct((N_TOK, F_W), BF16),
        compiler_params=_params("arbitrary"),
        name="fnet",
    )(f_all, *tables, w_bd)


def _grid_pos_embed(n_tok):
    rows = n_tok // GRID_W
    r = np.repeat(np.arange(rows, dtype=np.float64), GRID_W)
    col = np.tile(np.arange(GRID_W, dtype=np.float64), rows)
    quarter = D_MODEL // 4
    freq = np.exp(-math.log(10000.0) * np.arange(quarter, dtype=np.float64) / quarter)

    def emb(p):
        ang = p[:, None] * freq[None, :]
        return np.concatenate([np.sin(ang), np.cos(ang)], axis=-1)

    return np.concatenate([emb(r), emb(col)], axis=-1).astype(np.float32)


def _dft_tables(n):
    k = np.arange(n, dtype=np.int64)
    ang = ((k[:, None] * k[None, :]) % n).astype(np.float64) * (2.0 * math.pi / n)
    scale = n ** -0.5
    return np.cos(ang) * scale, np.sin(ang) * scale


def _host_split(a):
    hi = a.astype(np.float32).astype(BF16)
    lo = (a - hi.astype(np.float64)).astype(np.float32).astype(BF16)
    return hi, lo


def _fnet_tables():
    cc, sc = _dft_tables(FN_CH)
    eye = np.eye(F_W // FN_CH)
    tables = list(_host_split(np.concatenate([np.kron(eye, cc), np.kron(eye, sc)], axis=1)))
    for n in (SEQ, DEC_SEQ):
        cn, sn = _dft_tables(n)
        tables += list(_host_split(np.concatenate([cn, -sn], axis=1)))
    return tables


def _block_diag(blocks):
    g, a, b = blocks.shape
    eye = jnp.eye(g, dtype=blocks.dtype)
    return (eye[:, None, :, None] * blocks[:, :, None, :]).reshape(g * a, g * b)


def kernel(x_prompt, x_sample, state_delta, c, c_ctx, ffn1_norm, ffn1_w_gate, ffn1_w_up, ffn1_w_down,
           mix_norm, ffn2_norm, ffn2_w_gate, ffn2_w_up, ffn2_w_down, ada_w, ada_b, ev_w_in, ev_w_out,
           pool_w, pool_scale, dn_conv_w, dn_a_log, dn_dt_bias, dn_norm_w, od_w_in, od_w_out, sgu_norm,
           sgu_w, sgu_b, fnet_w, final_norm):
    cond8 = jnp.zeros((SUBLANES, D_MODEL), F32).at[0].set(c_ctx).at[1:1 + DEC_BATCH].set(c)
    mods = _ada(cond8, ada_w, ada_b)
    mods = mods[:, :1 + DEC_BATCH].reshape(DEPTH, 1 + DEC_BATCH, N_MOD, 1, D_MODEL).transpose(0, 2, 1, 3, 4)

    fnet_tables = _fnet_tables()
    groups = ((SEQ, BATCH, 0, DN_HEADS_PER_STEP[SEQ]),
              (DEC_SEQ, DEC_BATCH, CTX_TOK // DEC_SEQ, DN_HEADS_PER_STEP[DEC_SEQ]))
    even_segments = ((A_W, A_W + 4 * QK_W), (0, A_W), (A_W + 4 * QK_W, P_EVEN))
    odd_segments = ((0, 2 * C_W), (2 * C_W, P_ODD))

    x = (x_prompt.reshape(CTX_TOK, D_MODEL), x_sample.reshape(LAT_TOK, D_MODEL), _grid_pos_embed(DEC_SEQ))
    ev_w_in_t = jnp.swapaxes(ev_w_in, 1, 2)
    states = None
    for layer in range(DEPTH):
        x = _ffn(x, mods, layer, 0, ffn1_norm, ffn1_w_gate, ffn1_w_up, ffn1_w_down)
        if layer % 2 == 0:
            e = layer // 2
            qkvz, p_pool, gates, gates_t = _mixin(x, mods, layer, mix_norm, ev_w_in_t, e, even_segments,
                                                  TM_EVEN_PROJ, transposed=True)
            w_bd = _block_diag(pool_w[e])
            ya = _pool(p_pool, w_bd, pool_scale[e][None])
            yb = []
            for seq, n_seq, rb0, dn_heads in groups:
                ctx = rb0 == 0
                o, st = _deltanet(qkvz, gates, gates_t, dn_conv_w, e, dn_a_log[e], dn_dt_bias[e], dn_norm_w,
                                  None if ctx else state_delta, seq, dn_heads, n_seq, rb0,
                                  state_slot=e if ctx else None, states=states)
                yb.append(o)
                if ctx:
                    states = st
            mix = (ya, tuple(yb), ev_w_out, e)
        else:
            j = layer // 2
            uv, f_all = _mixin(x, mods, layer, mix_norm, od_w_in, j, odd_segments, TM)
            yc = _sgu(uv, sgu_norm[j][None], sgu_w[j], sgu_b[j].T)
            w_bd = _block_diag(fnet_w[j])
            yd = _fnet(f_all, fnet_tables, w_bd)
            mix = (yc, yd, od_w_out, j)
        x = _ffn(x, mods, layer, 2, ffn2_norm, ffn2_w_gate, ffn2_w_up, ffn2_w_down, mix=mix,
                 final_norm=final_norm if layer == DEPTH - 1 else None)

    y_prompt, y_sample = x
    return (y_prompt.reshape(BATCH, SEQ, D_MODEL), y_sample.reshape(DEC_BATCH, DEC_SEQ, D_MODEL), states)
```

```python
import functools
import math

import numpy as np
import jax
import jax.numpy as jnp
from jax import lax
from jax.experimental import pallas as pl
from jax.experimental.pallas import tpu as pltpu

F32 = jnp.float32
BF16 = jnp.bfloat16

D_MODEL = 1024
BATCH = 16
SEQ = 256
DEPTH = 4
DEC_BATCH = 2
DEC_SEQ = 1024
GRID_W = 64
EPS = 1e-6
D_FF = 2816
N_MOD = 9
N_EVEN = (DEPTH + 1) // 2
POOL_SIZES = (2, 4, 8, 16)
POOL_CH = 64
A_W = 256
DN_HEADS = 6
DN_DK = 128
QK_W = DN_HEADS * DN_DK
B_W = QK_W
GATE_W = 4 * DN_HEADS
P_EVEN = A_W + 4 * QK_W + GATE_W
SGU_CHUNK = 128
SGU_HEADS = 6
C_W = 768
FN_CH = 64
F_W = 256
P_ODD = 2 * C_W + F_W

CTX_TOK = BATCH * SEQ
LAT_TOK = DEC_BATCH * DEC_SEQ
N_TOK = CTX_TOK + LAT_TOK

LANES = 128
SUBLANES = 8
VMEM_LIMIT = 56 * 1024 * 1024

TM = 1024
N_ROW_TILES = N_TOK // TM
CTX_TILES = CTX_TOK // TM
TF = 256
TM_EVEN_PROJ = 512
ADA_TN = 2304
DN_CHUNK = 128
DN_HEADS_PER_STEP = {SEQ: DN_HEADS, DEC_SEQ: 3}
COL_Q, COL_K, COL_V, COL_Z = 0, 1, 2, 3


def _cond_of_tile(i, tm):
    return jnp.where(i < CTX_TOK // tm, 0, 1 + (i - CTX_TOK // tm) * tm // DEC_SEQ)


def _silu(x):
    half = 0.5 * x
    return half + half * jnp.tanh(half)


def _dot(a, b):
    return jnp.dot(a.astype(BF16), b.astype(BF16), preferred_element_type=F32)


def _dot_nt(a, b):
    return lax.dot_general(a.astype(BF16), b.astype(BF16), (((1,), (1,)), ((), ())),
                           preferred_element_type=F32)


def _dot_tn(a, b):
    return lax.dot_general(a.astype(BF16), b.astype(BF16), (((0,), (0,)), ((), ())),
                           preferred_element_type=F32)


def _split(a):
    hi = a.astype(BF16)
    lo = (a - hi.astype(F32)).astype(BF16)
    return hi, lo


def _dot3(a, b):
    ah, al = a
    bh, bl = b
    return (jnp.dot(ah, bh, preferred_element_type=F32)
            + (jnp.dot(al, bh, preferred_element_type=F32)
               + jnp.dot(ah, bl, preferred_element_type=F32)))


def _rms(x, w):
    return x * lax.rsqrt(jnp.mean(x * x, axis=-1, keepdims=True) + EPS) * w


def _norm_mod(x, nw, scale, shift):
    r = lax.rsqrt(jnp.mean(x * x, axis=-1, keepdims=True) + EPS)
    return ((x * r) * (nw * (1.0 + scale)) + shift).astype(BF16)


def _params(*sem):
    return pltpu.CompilerParams(dimension_semantics=sem, vmem_limit_bytes=VMEM_LIMIT)


def _ctx_rows(tm, width):
    return pl.BlockSpec((tm, width), lambda i, *_: (jnp.minimum(i, CTX_TOK // tm - 1), 0))


def _lat_rows(tm, width):
    return pl.BlockSpec((tm, width), lambda i, *_: (jnp.maximum(i - CTX_TOK // tm, 0), 0))


def _ada_kernel(c_ref, w_ref, b_ref, o_ref):
    o_ref[0] = _dot(_silu(c_ref[...]), w_ref[0]) + b_ref[0]


def _ada(cond8, ada_w, ada_b):
    tn = ADA_TN
    n = N_MOD * D_MODEL
    return pl.pallas_call(
        _ada_kernel,
        grid=(DEPTH, n // tn),
        in_specs=[pl.BlockSpec((SUBLANES, D_MODEL), lambda l, j: (0, 0)),
                  pl.BlockSpec((1, D_MODEL, tn), lambda l, j: (l, 0, j)),
                  pl.BlockSpec((1, 1, tn), lambda l, j: (l, 0, j))],
        out_specs=pl.BlockSpec((1, SUBLANES, tn), lambda l, j: (l, 0, j)),
        out_shape=jax.ShapeDtypeStruct((DEPTH, SUBLANES, n), F32),
        compiler_params=_params("arbitrary", "arbitrary"),
        name="ada",
    )(cond8, ada_w, ada_b.reshape(DEPTH, 1, n))


def _mod_spec(layer, which, tm=TM, n_seg=1, seg=0):
    return pl.BlockSpec((None, None, None, 1, D_MODEL),
                        lambda i, *_: (layer, which, _cond_of_tile(i * n_seg + seg, tm), 0, 0))


def _layer_vec(layer):
    return pl.BlockSpec((None, 1, D_MODEL), lambda i, *_: (layer, 0, 0))


def _ffn_kernel(*refs, first, mix_counts, mix_split, final, n_seg, per_step):
    refs = list(refs)
    take = lambda k: [refs.pop(0) for _ in range(k)]
    if first:
        xp_ref, xs_ref, pos_ref = take(3)
    else:
        (x_in_ref,) = take(1)
    if mix_counts:
        mix_groups = [take(c) for c in mix_counts]
        g2_ref, wo_ref = take(2)
    (nw_ref,) = take(1)
    seg_mods = [take(3) for _ in range(n_seg)]
    w_blocks = [take(3) for _ in range(per_step)]
    if final:
        fn_ref, oc_ref, ol_ref = take(3)
    else:
        (o_ref,) = take(1)
    (h_scr,) = take(1)
    acc_scr = o_ref if n_seg > 1 else refs.pop(0)
    x_ref = refs.pop(0) if (first or mix_counts) else x_in_ref
    i = pl.program_id(0)
    j = pl.program_id(1)
    ctx = i < CTX_TILES

    @pl.when(j == 0)
    def _():
        if first:
            @pl.when(ctx)
            def _():
                x_ref[...] = xp_ref[...]

            @pl.when(jnp.logical_not(ctx))
            def _():
                x_ref[...] = xs_ref[...] + pos_ref[...]
        elif mix_counts:
            def pick(rs):
                return rs[0][...] if len(rs) == 1 else jnp.where(ctx, rs[0][...], rs[1][...])

            if len(mix_groups) == 1:
                mix = _dot(pick(mix_groups[0]), wo_ref[...])
            else:
                mix = (_dot(pick(mix_groups[0]), wo_ref[0:mix_split, :])
                       + _dot(pick(mix_groups[1]), wo_ref[mix_split:, :]))
            x_ref[...] = x_in_ref[...] + g2_ref[...] * mix

        for s, (sh_ref, sc_ref, _) in enumerate(seg_mods):
            rows = slice(s * TM, (s + 1) * TM)
            h_scr[rows, :] = _norm_mod(x_ref[rows, :], nw_ref[...], sc_ref[...], sh_ref[...])
        acc_scr[...] = jnp.zeros_like(acc_scr)

    def accumulate(blocks):
        h = h_scr[...]
        total = None
        for wg_ref, wu_ref, wd_ref in blocks:
            g = jnp.dot(h, wg_ref[...].astype(BF16), preferred_element_type=F32)
            u = jnp.dot(h, wu_ref[...].astype(BF16), preferred_element_type=F32)
            part = _dot(_silu(g) * u, wd_ref[...])
            total = part if total is None else total + part
        acc_scr[...] += total

    last = pl.num_programs(1) - 1
    if per_step == 1:
        accumulate(w_blocks)
    else:
        @pl.when(j < last)
        def _():
            accumulate(w_blocks)

        @pl.when(j == last)
        def _():
            accumulate(w_blocks[:1])

    @pl.when(j == last)
    def _():
        if final:
            y = _rms(x_ref[...] + 0.5 * seg_mods[0][2][...] * acc_scr[...], fn_ref[...])

            @pl.when(ctx)
            def _():
                oc_ref[...] = y

            @pl.when(jnp.logical_not(ctx))
            def _():
                ol_ref[...] = y
        else:
            for s, (_, _, gt_ref) in enumerate(seg_mods):
                rows = slice(s * TM, (s + 1) * TM)
                o_ref[rows, :] = x_ref[rows, :] + 0.5 * gt_ref[...] * acc_scr[rows, :]


def _ffn(x, mods, layer, sub, norm_w, w_gate, w_up, w_down, mix=None, final_norm=None):
    first = isinstance(x, tuple)
    final = final_norm is not None
    n_seg = 2 if (not first and mix is None and not final) else 1
    tm = n_seg * TM
    row = lambda width: pl.BlockSpec((tm, width), lambda i, j: (i, 0))
    if first:
        in_specs = [_ctx_rows(TM, D_MODEL), _lat_rows(TM, D_MODEL),
                    pl.BlockSpec((TM, D_MODEL), lambda i, j: (0, 0))]
        args = list(x)
    else:
        in_specs, args = [row(D_MODEL)], [x]
    mix_counts, mix_split = (), 0
    if mix is not None:
        ya, yb, w_out, w_index = mix
        counts = []
        for y in (ya, yb):
            if y is None:
                continue
            parts = y if isinstance(y, tuple) else (y,)
            width = parts[0].shape[1]
            in_specs += [row(width)] if len(parts) == 1 else [_ctx_rows(TM, width), _lat_rows(TM, width)]
            args += list(parts)
            counts.append(len(parts))
        mix_counts = tuple(counts)
        mix_split = (ya[0] if isinstance(ya, tuple) else ya).shape[1]
        in_specs += [_mod_spec(layer, 5),
                     pl.BlockSpec((None, D_MODEL, D_MODEL), lambda i, j: (w_index, 0, 0),
                                  pipeline_mode=pl.Buffered(1))]
        args += [mods, w_out]
    in_specs.append(_layer_vec(layer))
    args.append(norm_w.reshape(DEPTH, 1, D_MODEL))
    for seg in range(n_seg):
        in_specs += [_mod_spec(layer, 3 * sub + k, TM, n_seg, seg) for k in range(3)]
        args += [mods, mods, mods]
    per_step = 1 if final else 2
    n_blocks = D_FF // TF
    for k in range(per_step):
        blk = lambda j, k=k: jnp.minimum(per_step * j + k, n_blocks - 1)
        in_specs += [pl.BlockSpec((None, D_MODEL, TF), lambda i, j, blk=blk: (layer, 0, blk(j))),
                     pl.BlockSpec((None, D_MODEL, TF), lambda i, j, blk=blk: (layer, 0, blk(j))),
                     pl.BlockSpec((None, TF, D_MODEL), lambda i, j, blk=blk: (layer, blk(j), 0))]
        args += [w_gate, w_up, w_down]
    scratch = [pltpu.VMEM((tm, D_MODEL), BF16)]
    if n_seg == 1:
        scratch.append(pltpu.VMEM((tm, D_MODEL), F32))
    if first or mix_counts:
        scratch.append(pltpu.VMEM((TM, D_MODEL), F32))
    if final:
        in_specs.append(pl.BlockSpec((1, D_MODEL), lambda i, j: (0, 0)))
        args.append(final_norm[None])
        out_specs = [_ctx_rows(TM, D_MODEL), _lat_rows(TM, D_MODEL)]
        out_shape = [jax.ShapeDtypeStruct((CTX_TOK, D_MODEL), F32), jax.ShapeDtypeStruct((LAT_TOK, D_MODEL), F32)]
    else:
        out_specs = row(D_MODEL)
        out_shape = jax.ShapeDtypeStruct((N_TOK, D_MODEL), F32)
    return pl.pallas_call(
        functools.partial(_ffn_kernel, first=first, mix_counts=mix_counts, mix_split=mix_split, final=final,
                          n_seg=n_seg, per_step=per_step),
        grid=(N_TOK // tm, pl.cdiv(n_blocks, per_step)),
        in_specs=in_specs,
        out_specs=out_specs,
        out_shape=out_shape,
        scratch_shapes=scratch,
        compiler_params=_params("arbitrary", "arbitrary"),
        name=f"ffn{sub}_{layer}",
    )(*args)


def _mixin_kernel(x_ref, nw_ref, sh_ref, sc_ref, w_ref, *rest, segments, transposed):
    o_refs, wb_scr = rest[:-1], rest[-1]

    @pl.when(pl.program_id(0) == 0)
    def _():
        wb_scr[...] = w_ref[...].astype(BF16)

    h = _norm_mod(x_ref[...], nw_ref[...], sc_ref[...], sh_ref[...])
    for (lo, hi), o_ref in zip(segments, o_refs):
        if transposed:
            o_ref[...] = _dot_nt(h, wb_scr[lo:hi, :])
        else:
            o_ref[...] = _dot(h, wb_scr[:, lo:hi])
    if transposed:
        lo, hi = segments[-1]
        o_refs[-1][...] = _dot_nt(wb_scr[lo:hi, :], h)


def _mixin(x, mods, layer, norm_w, w_in, w_index, segments, tm, transposed=False):
    shape = w_in.shape[1:]
    out_specs = [pl.BlockSpec((tm, hi - lo), lambda i: (i, 0)) for lo, hi in segments]
    out_shape = [jax.ShapeDtypeStruct((N_TOK, hi - lo), F32) for lo, hi in segments]
    if transposed:
        lo, hi = segments[-1]
        out_specs.append(pl.BlockSpec((hi - lo, tm), lambda i: (0, i)))
        out_shape.append(jax.ShapeDtypeStruct((hi - lo, N_TOK), F32))
    return pl.pallas_call(
        functools.partial(_mixin_kernel, segments=segments, transposed=transposed),
        grid=(N_TOK // tm,),
        in_specs=[pl.BlockSpec((tm, D_MODEL), lambda i: (i, 0)),
                  _layer_vec(layer), _mod_spec(layer, 3, tm), _mod_spec(layer, 4, tm),
                  pl.BlockSpec((None,) + shape, lambda i: (w_index, 0, 0), pipeline_mode=pl.Buffered(1))],
        out_specs=out_specs,
        out_shape=out_shape,
        scratch_shapes=[pltpu.VMEM(shape, BF16)],
        compiler_params=_params("arbitrary"),
        name=f"mixin_{layer}",
    )(x, norm_w.reshape(DEPTH, 1, D_MODEL), mods, mods, w_in)


def _pool_kernel(p_ref, w_ref, scale_ref, o_ref):
    seq = jnp.where(pl.program_id(0) < CTX_TILES, SEQ, DEC_SEQ)
    p = p_ref[...]
    pos = lax.broadcasted_iota(jnp.int32, (TM, A_W), 0) & (seq - 1)

    def before(a, s):
        return jnp.where(pos >= s, pltpu.roll(a, s, axis=0), 0.0)

    def after(a, s):
        return jnp.where(pos < seq - s, pltpu.roll(a, TM - s, axis=0), 0.0)

    sums = []
    f, b = p, before(p, 1)
    for h in (size // 2 for size in POOL_SIZES):
        if h > 1:
            f = f + after(f, h // 2)
            b = b + before(b, h // 2)
        sums.append(f + b)
    group = lax.broadcasted_iota(jnp.int32, (TM, A_W), 1) >> (POOL_CH.bit_length() - 1)
    wsum = jnp.where(group == 0, sums[0], jnp.where(group == 1, sums[1], jnp.where(group == 2, sums[2], sums[3])))
    half = jnp.left_shift(1, group)
    cnt = (jnp.minimum(pos + half, seq) - jnp.maximum(pos - half, 0)).astype(F32)
    d = wsum / cnt - p
    o_ref[...] = (_dot(d, w_ref[...]) * scale_ref[...]).astype(o_ref.dtype)


def _pool(p_pool, w_bd, scale):
    return pl.pallas_call(
        _pool_kernel,
        grid=(N_ROW_TILES,),
        in_specs=[pl.BlockSpec((TM, A_W), lambda i: (i, 0)),
                  pl.BlockSpec((A_W, A_W), lambda i: (0, 0)),
                  pl.BlockSpec((1, A_W), lambda i: (0, 0))],
        out_specs=pl.BlockSpec((TM, A_W), lambda i: (i, 0)),
        out_shape=jax.ShapeDtypeStruct((N_TOK, A_W), BF16),
        compiler_params=_params("arbitrary"),
        name="pool",
    )(p_pool, w_bd, scale)


def _link_mask(r, c, s, upper):
    lg = s.bit_length() - 1
    same = (r >> (lg + 1)) == (c >> (lg + 1))
    r_half = (r >> lg) & 1
    c_half = (c >> lg) & 1
    return same & ((r_half == 0) & (c_half == 1) if upper else (r_half == 1) & (c_half == 0))


def _dn_kernel(q_ref, k_ref, v_ref, z_ref, g_ref, gt_ref, cq_ref, ck_ref, cv_ref, alog_ref, dtb_ref,
               alog_c_ref, dtb_c_ref, nw_ref, *rest, seq, heads, zero_init, state_slot):
    rest = list(rest)
    s0_ref = None if zero_init else rest.pop(0)
    if state_slot:
        rest.pop(0)
    o_ref = rest.pop(0)
    st_ref = rest.pop(0) if state_slot is not None else None
    gct_scr = rest.pop(0)
    h0 = pl.program_id(1) * heads
    n = DN_CHUNK
    n_blk = seq // n
    row = lax.broadcasted_iota(jnp.int32, (seq, LANES), 0)
    g_lane = lax.broadcasted_iota(jnp.int32, (seq, GATE_W), 1)
    r2 = lax.broadcasted_iota(jnp.int32, (n, n), 0)
    c2 = lax.broadcasted_iota(jnp.int32, (n, n), 1)
    eye = (r2 == c2).astype(F32)
    incl = (r2 >= c2, r2 <= c2)
    strict = (r2 > c2, r2 < c2)
    levels = [1 << b for b in range(n.bit_length() - 1)]
    link = {(s, d): _link_mask(r2, c2, s, d == 1) for s in levels for d in (0, 1)}
    units = [(hh, blk, d) for hh in range(heads) for blk in range(n_blk) for d in (0, 1)]

    def conv_silu(x, cw):
        prev = jnp.where(row >= 1, pltpu.roll(x, 1, axis=0), 0.0)
        nxt = jnp.where(row <= seq - 2, pltpu.roll(x, seq - 1, axis=0), 0.0)
        return _silu(prev * cw[0:1] + x * cw[1:2] + nxt * cw[2:3])

    def l2n(x):
        return x * lax.rsqrt(jnp.sum(x * x, axis=-1, keepdims=True) + EPS)

    def col(a, idx):
        return jnp.sum(jnp.where(g_lane == idx, a, 0.0), axis=1, keepdims=True)

    def log_decay(a, alog, dtb):
        xg = a + dtb
        return -jnp.exp(alog) * (jnp.maximum(xg, 0.0) + jnp.log1p(jnp.exp(-jnp.abs(xg))))

    gates = g_ref[...]
    beta_all = jax.nn.sigmoid(gates)
    g_all = log_decay(gates, alog_ref[...], dtb_ref[...])
    g_all_t = log_decay(gt_ref[...], alog_c_ref[...], dtb_c_ref[...])
    lower = incl[0].astype(BF16)
    upper = incl[1].astype(BF16)
    backward_rows = lax.broadcasted_iota(jnp.int32, (GATE_W, n), 0) >= 3 * DN_HEADS

    def tri_sums(a, left):
        rem, pre_sum, suf_sum = a, 0.0, 0.0
        for _ in range(3):
            piece = rem.astype(BF16)
            rem = rem - piece.astype(F32)
            if left:
                pre_sum = pre_sum + jnp.dot(lower, piece, preferred_element_type=F32)
                suf_sum = suf_sum + jnp.dot(upper, piece, preferred_element_type=F32)
            else:
                pre_sum = pre_sum + jnp.dot(piece, upper, preferred_element_type=F32)
                suf_sum = suf_sum + jnp.dot(piece, lower, preferred_element_type=F32)
        return pre_sum, suf_sum

    pre_blocks, suf_blocks = [], []
    for blk in range(n_blk):
        rs = slice(blk * n, (blk + 1) * n)
        p, s_ = tri_sums(g_all[rs, :], left=True)
        pre_blocks.append(p)
        suf_blocks.append(s_)
        p_t, s_t = tri_sums(g_all_t[:, rs], left=False)
        sums = jnp.where(backward_rows, s_t, p_t)
        for r in range(2 * DN_HEADS):
            gct_scr[r, :, rs] = sums[2 * DN_HEADS + r:2 * DN_HEADS + r + 1, :]
    pre = jnp.concatenate(pre_blocks, axis=0)
    suf = jnp.concatenate(suf_blocks, axis=0)

    qs, kn, vv, beta, gc = [], [], [], [], []
    for hh in range(heads):
        cs = slice(hh * LANES, (hh + 1) * LANES)
        qs.append(l2n(conv_silu(q_ref[:, cs], cq_ref[:, cs])) * (DN_DK ** -0.5))
        kn.append(l2n(conv_silu(k_ref[:, cs], ck_ref[:, cs])))
        vv.append(conv_silu(v_ref[:, cs], cv_ref[:, cs]))
        beta.append((col(beta_all, h0 + hh), col(beta_all, DN_HEADS + h0 + hh)))
        gc.append((col(pre, 2 * DN_HEADS + h0 + hh), col(suf, 3 * DN_HEADS + h0 + hh)))

    kk, qk = {}, {}
    for hh in range(heads):
        for blk in range(n_blk):
            rs = slice(blk * n, (blk + 1) * n)
            kk[hh, blk] = _dot_nt(kn[hh][rs], kn[hh][rs])
            qk[hh, blk] = _dot_nt(qs[hh][rs], kn[hh][rs])

    m, m_hi, t, a_in = {}, {}, {}, {}
    for u in units:
        hh, blk, d = u
        rs = slice(blk * n, (blk + 1) * n)
        g_lanes = gct_scr[d * DN_HEADS + h0 + hh, :, rs]
        decay = jnp.where(incl[d], jnp.exp(gc[hh][d][rs] - g_lanes), 0.0)
        m[u] = jnp.where(strict[d], beta[hh][d][rs] * kk[hh, blk] * decay, 0.0)
        m_hi[u] = m[u].astype(BF16)
        a_in[u] = (qk[hh, blk] * decay).astype(BF16)
        t[u] = eye - jnp.where(link[1, d], m[u], 0.0)

    for s in levels[1:]:
        tb, x = {}, {}
        for u in units:
            tb[u] = t[u].astype(BF16)
            c_s = jnp.where(link[s, u[2]], m_hi[u], jnp.zeros_like(m_hi[u]))
            x[u] = jnp.dot(c_s, tb[u], preferred_element_type=F32).astype(BF16)
        for u in units:
            t[u] = t[u] - jnp.dot(tb[u], x[u], preferred_element_type=F32)

    uw, q_dec, k_dec, g_last = {}, {}, {}, {}
    for u in units:
        hh, blk, d = u
        rs = slice(blk * n, (blk + 1) * n)
        gcol = gc[hh][d][rs]
        bcol = beta[hh][d][rs]
        e_g = jnp.exp(gcol)
        kb = kn[hh][rs]
        uw[u] = _dot(t[u], jnp.concatenate([vv[hh][rs] * bcol, kb * (bcol * e_g)], axis=1))
        g_last[u] = gcol[n - 1:n] if d == 0 else gcol[0:1]
        q_dec[u] = qs[hh][rs] * e_g
        k_dec[u] = kb * jnp.exp(g_last[u] - gcol)

    state = {}
    for hh in range(heads):
        for d in (0, 1):
            state[hh, d] = jnp.zeros((DN_DK, DN_DK), F32) if zero_init else s0_ref[d, hh]
    outs = {}
    for step in range(n_blk):
        chains = [(hh, step if d == 0 else n_blk - 1 - step, d) for hh in range(heads) for d in (0, 1)]
        ws_qs = {}
        for u in chains:
            ws_qs[u] = _dot(jnp.concatenate([uw[u][:, LANES:], q_dec[u]], axis=0), state[u[0], u[2]])
        for u in chains:
            v_new = (uw[u][:, :LANES] - ws_qs[u][0:n]).astype(BF16)
            outs[u] = ws_qs[u][n:] + jnp.dot(a_in[u], v_new, preferred_element_type=F32)
            state[u[0], u[2]] = state[u[0], u[2]] * jnp.exp(g_last[u]) + _dot_tn(k_dec[u], v_new)

    for hh in range(heads):
        cs = slice(hh * LANES, (hh + 1) * LANES)
        o = (jnp.concatenate([outs[hh, blk, 0] for blk in range(n_blk)], axis=0)
             + jnp.concatenate([outs[hh, blk, 1] for blk in range(n_blk)], axis=0))
        o = o * lax.rsqrt(jnp.mean(o * o, axis=-1, keepdims=True) + EPS) * nw_ref[...]
        o_ref[:, cs] = (o * _silu(z_ref[:, cs])).astype(o_ref.dtype)
        if state_slot == 0:
            st_ref[0, 0, hh] = state[hh, 0]
            st_ref[0, 1, hh] = state[hh, 1]
            st_ref[1:, :, hh] = jnp.zeros((N_EVEN - 1, 2, DN_DK, DN_DK), F32)
        elif state_slot:
            st_ref[0, hh] = state[hh, 0]
            st_ref[1, hh] = state[hh, 1]


def _deltanet(qkvz, gates, gates_t, conv_w, e, alog, dtb, norm_w, s0, seq, heads, n_seq, row_block0,
              state_slot=None, states=None):
    zero_init = s0 is None
    width = heads * LANES
    per = DN_HEADS // heads
    alog_row = jnp.concatenate([jnp.zeros((2 * DN_HEADS,), F32), alog.reshape(-1)])[None]
    dtb_row = jnp.concatenate([jnp.zeros((2 * DN_HEADS,), F32), dtb.reshape(-1)])[None]

    def pcol(cb):
        return pl.BlockSpec((seq, width), lambda b, h: (row_block0 + b, cb * per + h))

    def ccol(cb):
        return pl.BlockSpec((None, 3, width), lambda b, h: (e, 0, cb * per + h))

    grow = pl.BlockSpec((1, GATE_W), lambda b, h: (0, 0))
    gcol = pl.BlockSpec((GATE_W, 1), lambda b, h: (0, 0))
    in_specs = [pcol(COL_Q), pcol(COL_K), pcol(COL_V), pcol(COL_Z),
                pl.BlockSpec((seq, GATE_W), lambda b, h: (row_block0 + b, 0)),
                pl.BlockSpec((GATE_W, seq), lambda b, h: (0, row_block0 + b)),
                ccol(0), ccol(1), ccol(2), grow, grow, gcol, gcol,
                pl.BlockSpec((None, 1, LANES), lambda b, h: (e, 0, 0))]
    args = [qkvz, qkvz, qkvz, qkvz, gates, gates_t, conv_w, conv_w, conv_w, alog_row, dtb_row,
            alog_row.T, dtb_row.T, norm_w.reshape(N_EVEN, 1, LANES)]
    if not zero_init:
        in_specs.append(pl.BlockSpec((None, None, 2, heads, DN_DK, DN_DK), lambda b, h: (b, e, 0, h, 0, 0)))
        args.append(s0)
    aliases = {}
    if state_slot:
        aliases = {len(args): 1}
        in_specs.append(pl.BlockSpec(memory_space=pl.ANY))
        args.append(states)
    out_specs = [pl.BlockSpec((seq, width), lambda b, h: (b, h))]
    out_shape = [jax.ShapeDtypeStruct((n_seq * seq, B_W), BF16)]
    if state_slot is not None:
        if state_slot == 0:
            out_specs.append(pl.BlockSpec((None, N_EVEN, 2, heads, DN_DK, DN_DK), lambda b, h: (b, 0, 0, h, 0, 0)))
        else:
            out_specs.append(pl.BlockSpec((None, None, 2, heads, DN_DK, DN_DK),
                                          lambda b, h: (b, state_slot, 0, h, 0, 0)))
        out_shape.append(jax.ShapeDtypeStruct((n_seq, N_EVEN, 2, DN_HEADS, DN_DK, DN_DK), F32))
    res = pl.pallas_call(
        functools.partial(_dn_kernel, seq=seq, heads=heads, zero_init=zero_init, state_slot=state_slot),
        grid=(n_seq, per),
        in_specs=in_specs,
        out_specs=out_specs,
        out_shape=out_shape,
        scratch_shapes=[pltpu.VMEM((2 * DN_HEADS, 1, seq), F32)],
        input_output_aliases=aliases,
        compiler_params=_params("arbitrary", "arbitrary"),
        name=f"deltanet_{seq}",
    )(*args)
    return res if state_slot is not None else (res[0], None)


def _oddmix_kernel(u_ref, v_ref, f_ref, nw_ref, ws_ref, bt_ref, ch_hi_ref, ch_lo_ref, ctx_hi_ref, ctx_lo_ref,
                   lat_hi_ref, lat_lo_ref, w_ref, o_ref):
    ctx = pl.program_id(0) < CTX_TILES

    def tile(windows):
        fcs = _dot3(_split(f_ref[...]), (ch_hi_ref[...], ch_lo_ref[...]))
        wb = w_ref[...].astype(BF16)
        for rows, hi_ref, lo_ref in windows:
            stacked = jnp.concatenate([fcs[rows, :F_W], fcs[rows, F_W:]], axis=0)
            spec = _dot3((hi_ref[...], lo_ref[...]), _split(stacked))
            o_ref[rows, C_W:] = jnp.dot(spec.astype(BF16), wb, preferred_element_type=F32).astype(o_ref.dtype)

        u = jax.nn.gelu(u_ref[...])
        v = jax.nn.gelu(v_ref[...])
        mu = jnp.mean(v, axis=-1, keepdims=True)
        vc = v - mu
        var = jnp.mean(vc * vc, axis=-1, keepdims=True)
        vn = vc * lax.rsqrt(var + EPS) * nw_ref[...]
        bt = bt_ref[...]
        for c in range(TM // SGU_CHUNK):
            rs = slice(c * SGU_CHUNK, (c + 1) * SGU_CHUNK)
            for hd in range(SGU_HEADS):
                cs = slice(hd * LANES, (hd + 1) * LANES)
                mixed = _dot(ws_ref[hd], vn[rs, cs]) + bt[:, hd:hd + 1]
                o_ref[rs, cs] = (u[rs, cs] * mixed).astype(o_ref.dtype)

    @pl.when(ctx)
    def _():
        tile([(slice(s * SEQ, (s + 1) * SEQ), ctx_hi_ref, ctx_lo_ref) for s in range(TM // SEQ)])

    @pl.when(jnp.logical_not(ctx))
    def _():
        tile([(slice(0, DEC_SEQ), lat_hi_ref, lat_lo_ref)])


def _oddmix(uv, f_all, norm_w, w_s, b_t, tables, w_bd):
    const = lambda a: pl.BlockSpec(a.shape, lambda i: (0,) * a.ndim)
    return pl.pallas_call(
        _oddmix_kernel,
        grid=(N_ROW_TILES,),
        in_specs=([pl.BlockSpec((TM, C_W), lambda i: (i, 0)),
                   pl.BlockSpec((TM, C_W), lambda i: (i, 1)),
                   pl.BlockSpec((TM, F_W), lambda i: (i, 0)),
                   const(norm_w), const(w_s), const(b_t)] + [const(t) for t in tables] + [const(w_bd)]),
        out_specs=pl.BlockSpec((TM, C_W + F_W), lambda i: (i, 0)),
        out_shape=jax.ShapeDtypeStruct((N_TOK, C_W + F_W), BF16),
        compiler_params=_params("arbitrary"),
        name="oddmix",
    )(uv, uv, f_all, norm_w, w_s, b_t, *tables, w_bd)


def _grid_pos_embed(n_tok):
    rows = n_tok // GRID_W
    r = np.repeat(np.arange(rows, dtype=np.float64), GRID_W)
    col = np.tile(np.arange(GRID_W, dtype=np.float64), rows)
    quarter = D_MODEL // 4
    freq = np.exp(-math.log(10000.0) * np.arange(quarter, dtype=np.float64) / quarter)

    def emb(p):
        ang = p[:, None] * freq[None, :]
        return np.concatenate([np.sin(ang), np.cos(ang)], axis=-1)

    return np.concatenate([emb(r), emb(col)], axis=-1).astype(np.float32)


def _dft_tables(n):
    k = np.arange(n, dtype=np.int64)
    ang = ((k[:, None] * k[None, :]) % n).astype(np.float64) * (2.0 * math.pi / n)
    scale = n ** -0.5
    return np.cos(ang) * scale, np.sin(ang) * scale


def _host_split(a):
    hi = a.astype(np.float32).astype(BF16)
    lo = (a - hi.astype(np.float64)).astype(np.float32).astype(BF16)
    return hi, lo


def _fnet_tables():
    cc, sc = _dft_tables(FN_CH)
    eye = np.eye(F_W // FN_CH)
    tables = list(_host_split(np.concatenate([np.kron(eye, cc), np.kron(eye, sc)], axis=1)))
    for n in (SEQ, DEC_SEQ):
        cn, sn = _dft_tables(n)
        tables += list(_host_split(np.concatenate([cn, -sn], axis=1)))
    return tables


def _block_diag(blocks):
    g, a, b = blocks.shape
    eye = jnp.eye(g, dtype=blocks.dtype)
    return (eye[:, None, :, None] * blocks[:, :, None, :]).reshape(g * a, g * b)


def kernel(x_prompt, x_sample, state_delta, c, c_ctx, ffn1_norm, ffn1_w_gate, ffn1_w_up, ffn1_w_down,
           mix_norm, ffn2_norm, ffn2_w_gate, ffn2_w_up, ffn2_w_down, ada_w, ada_b, ev_w_in, ev_w_out,
           pool_w, pool_scale, dn_conv_w, dn_a_log, dn_dt_bias, dn_norm_w, od_w_in, od_w_out, sgu_norm,
           sgu_w, sgu_b, fnet_w, final_norm):
    cond8 = jnp.zeros((SUBLANES, D_MODEL), F32).at[0].set(c_ctx).at[1:1 + DEC_BATCH].set(c)
    mods = _ada(cond8, ada_w, ada_b)
    mods = mods[:, :1 + DEC_BATCH].reshape(DEPTH, 1 + DEC_BATCH, N_MOD, 1, D_MODEL).transpose(0, 2, 1, 3, 4)

    fnet_tables = _fnet_tables()
    groups = ((SEQ, BATCH, 0, DN_HEADS_PER_STEP[SEQ]),
              (DEC_SEQ, DEC_BATCH, CTX_TOK // DEC_SEQ, DN_HEADS_PER_STEP[DEC_SEQ]))
    even_segments = ((A_W, A_W + 4 * QK_W), (0, A_W), (A_W + 4 * QK_W, P_EVEN))
    odd_segments = ((0, 2 * C_W), (2 * C_W, P_ODD))

    x = (x_prompt.reshape(CTX_TOK, D_MODEL), x_sample.reshape(LAT_TOK, D_MODEL), _grid_pos_embed(DEC_SEQ))
    ev_w_in_t = jnp.swapaxes(ev_w_in, 1, 2)
    states = None
    for layer in range(DEPTH):
        x = _ffn(x, mods, layer, 0, ffn1_norm, ffn1_w_gate, ffn1_w_up, ffn1_w_down)
        if layer % 2 == 0:
            e = layer // 2
            qkvz, p_pool, gates, gates_t = _mixin(x, mods, layer, mix_norm, ev_w_in_t, e, even_segments,
                                                  TM_EVEN_PROJ, transposed=True)
            w_bd = _block_diag(pool_w[e])
            ya = _pool(p_pool, w_bd, pool_scale[e][None])
            yb = []
            for seq, n_seq, rb0, dn_heads in groups:
                ctx = rb0 == 0
                o, st = _deltanet(qkvz, gates, gates_t, dn_conv_w, e, dn_a_log[e], dn_dt_bias[e], dn_norm_w,
                                  None if ctx else state_delta, seq, dn_heads, n_seq, rb0,
                                  state_slot=e if ctx else None, states=states)
                yb.append(o)
                if ctx:
                    states = st
            mix = (ya, tuple(yb), ev_w_out, e)
        else:
            j = layer // 2
            uv, f_all = _mixin(x, mods, layer, mix_norm, od_w_in, j, odd_segments, TM)
            y_cd = _oddmix(uv, f_all, sgu_norm[j][None], sgu_w[j], sgu_b[j].T, fnet_tables, _block_diag(fnet_w[j]))
            mix = (y_cd, None, od_w_out, j)
        x = _ffn(x, mods, layer, 2, ffn2_norm, ffn2_w_gate, ffn2_w_up, ffn2_w_down, mix=mix,
                 final_norm=final_norm if layer == DEPTH - 1 else None)

    y_prompt, y_sample = x
    return (y_prompt.reshape(BATCH, SEQ, D_MODEL), y_sample.reshape(DEC_BATCH, DEC_SEQ, D_MODEL), states)
```

```python
import functools
import math

import numpy as np
import jax
import jax.numpy as jnp
from jax import lax
from jax.experimental import pallas as pl
from jax.experimental.pallas import tpu as pltpu

F32 = jnp.float32
BF16 = jnp.bfloat16

D_MODEL = 1024
BATCH = 16
SEQ = 256
DEPTH = 4
DEC_BATCH = 2
DEC_SEQ = 1024
GRID_W = 64
EPS = 1e-6
D_FF = 2816
N_MOD = 9
N_EVEN = (DEPTH + 1) // 2
POOL_SIZES = (2, 4, 8, 16)
POOL_CH = 64
A_W = 256
DN_HEADS = 6
DN_DK = 128
QK_W = DN_HEADS * DN_DK
B_W = QK_W
GATE_W = 4 * DN_HEADS
P_EVEN = A_W + 4 * QK_W + GATE_W
SGU_CHUNK = 128
SGU_HEADS = 6
C_W = 768
FN_CH = 64
F_W = 256
P_ODD = 2 * C_W + F_W

CTX_TOK = BATCH * SEQ
LAT_TOK = DEC_BATCH * DEC_SEQ
N_TOK = CTX_TOK + LAT_TOK

LANES = 128
SUBLANES = 8
VMEM_LIMIT = 56 * 1024 * 1024

TM = 1024
N_ROW_TILES = N_TOK // TM
CTX_TILES = CTX_TOK // TM
TF = 256
TM_EVEN_PROJ = 512
ADA_TN = 2304
DN_CHUNK = 128
DN_HEADS_PER_STEP = {SEQ: DN_HEADS, DEC_SEQ: 3}
COL_Q, COL_K, COL_V, COL_Z = 0, 1, 2, 3


def _cond_of_tile(i, tm):
    return jnp.where(i < CTX_TOK // tm, 0, 1 + (i - CTX_TOK // tm) * tm // DEC_SEQ)


def _silu(x):
    half = 0.5 * x
    return half + half * jnp.tanh(half)


def _dot(a, b):
    return jnp.dot(a.astype(BF16), b.astype(BF16), preferred_element_type=F32)


def _dot_nt(a, b):
    return lax.dot_general(a.astype(BF16), b.astype(BF16), (((1,), (1,)), ((), ())),
                           preferred_element_type=F32)


def _dot_tn(a, b):
    return lax.dot_general(a.astype(BF16), b.astype(BF16), (((0,), (0,)), ((), ())),
                           preferred_element_type=F32)


def _split(a):
    hi = a.astype(BF16)
    lo = (a - hi.astype(F32)).astype(BF16)
    return hi, lo


def _dot3(a, b):
    ah, al = a
    bh, bl = b
    return (jnp.dot(ah, bh, preferred_element_type=F32)
            + (jnp.dot(al, bh, preferred_element_type=F32)
               + jnp.dot(ah, bl, preferred_element_type=F32)))


def _rms(x, w):
    return x * lax.rsqrt(jnp.mean(x * x, axis=-1, keepdims=True) + EPS) * w


def _norm_mod(x, nw, scale, shift):
    r = lax.rsqrt(jnp.mean(x * x, axis=-1, keepdims=True) + EPS)
    return ((x * r) * (nw * (1.0 + scale)) + shift).astype(BF16)


def _params(*sem):
    return pltpu.CompilerParams(dimension_semantics=sem, vmem_limit_bytes=VMEM_LIMIT)


def _ctx_rows(tm, width):
    return pl.BlockSpec((tm, width), lambda i, *_: (jnp.minimum(i, CTX_TOK // tm - 1), 0))


def _lat_rows(tm, width):
    return pl.BlockSpec((tm, width), lambda i, *_: (jnp.maximum(i - CTX_TOK // tm, 0), 0))


def _ada_kernel(c_ref, w_ref, b_ref, o_ref):
    o_ref[0] = _dot(_silu(c_ref[...]), w_ref[0]) + b_ref[0]


def _ada(cond8, ada_w, ada_b):
    tn = ADA_TN
    n = N_MOD * D_MODEL
    return pl.pallas_call(
        _ada_kernel,
        grid=(DEPTH, n // tn),
        in_specs=[pl.BlockSpec((SUBLANES, D_MODEL), lambda l, j: (0, 0)),
                  pl.BlockSpec((1, D_MODEL, tn), lambda l, j: (l, 0, j)),
                  pl.BlockSpec((1, 1, tn), lambda l, j: (l, 0, j))],
        out_specs=pl.BlockSpec((1, SUBLANES, tn), lambda l, j: (l, 0, j)),
        out_shape=jax.ShapeDtypeStruct((DEPTH, SUBLANES, n), F32),
        compiler_params=_params("arbitrary", "arbitrary"),
        name="ada",
    )(cond8, ada_w, ada_b.reshape(DEPTH, 1, n))


def _mod_spec(layer, which, tm=TM, n_seg=1, seg=0):
    return pl.BlockSpec((None, None, None, 1, D_MODEL),
                        lambda i, *_: (layer, which, _cond_of_tile(i * n_seg + seg, tm), 0, 0))


def _layer_vec(layer):
    return pl.BlockSpec((None, 1, D_MODEL), lambda i, *_: (layer, 0, 0))


def _ffn_kernel(*refs, first, mix_counts, mix_split, final, n_seg, per_step):
    refs = list(refs)
    take = lambda k: [refs.pop(0) for _ in range(k)]
    if first:
        xp_ref, xs_ref, pos_ref = take(3)
    else:
        (x_in_ref,) = take(1)
    if mix_counts:
        mix_groups = [take(c) for c in mix_counts]
        g2_ref, wo_ref = take(2)
    (nw_ref,) = take(1)
    seg_mods = [take(3) for _ in range(n_seg)]
    w_blocks = [take(3) for _ in range(per_step)]
    if final:
        fn_ref, oc_ref, ol_ref = take(3)
    else:
        (o_ref,) = take(1)
    (h_scr,) = take(1)
    acc_scr = o_ref if n_seg > 1 else refs.pop(0)
    x_ref = refs.pop(0) if (first or mix_counts) else x_in_ref
    i = pl.program_id(0)
    j = pl.program_id(1)
    ctx = i < CTX_TILES

    @pl.when(j == 0)
    def _():
        if first:
            @pl.when(ctx)
            def _():
                x_ref[...] = xp_ref[...]

            @pl.when(jnp.logical_not(ctx))
            def _():
                x_ref[...] = xs_ref[...] + pos_ref[...]
        elif mix_counts:
            def pick(rs):
                return rs[0][...] if len(rs) == 1 else jnp.where(ctx, rs[0][...], rs[1][...])

            if len(mix_groups) == 1:
                mix = _dot(pick(mix_groups[0]), wo_ref[...])
            else:
                mix = (_dot(pick(mix_groups[0]), wo_ref[0:mix_split, :])
                       + _dot(pick(mix_groups[1]), wo_ref[mix_split:, :]))
            x_ref[...] = x_in_ref[...] + g2_ref[...] * mix

        for s, (sh_ref, sc_ref, _) in enumerate(seg_mods):
            rows = slice(s * TM, (s + 1) * TM)
            h_scr[rows, :] = _norm_mod(x_ref[rows, :], nw_ref[...], sc_ref[...], sh_ref[...])
        acc_scr[...] = jnp.zeros_like(acc_scr)

    def accumulate(blocks):
        h = h_scr[...]
        total = None
        for wg_ref, wu_ref, wd_ref in blocks:
            g = jnp.dot(h, wg_ref[...].astype(BF16), preferred_element_type=F32)
            u = jnp.dot(h, wu_ref[...].astype(BF16), preferred_element_type=F32)
            part = _dot(_silu(g) * u, wd_ref[...])
            total = part if total is None else total + part
        acc_scr[...] += total

    last = pl.num_programs(1) - 1
    if per_step == 1:
        accumulate(w_blocks)
    else:
        @pl.when(j < last)
        def _():
            accumulate(w_blocks)

        @pl.when(j == last)
        def _():
            accumulate(w_blocks[:1])

    @pl.when(j == last)
    def _():
        if final:
            y = _rms(x_ref[...] + 0.5 * seg_mods[0][2][...] * acc_scr[...], fn_ref[...])

            @pl.when(ctx)
            def _():
                oc_ref[...] = y

            @pl.when(jnp.logical_not(ctx))
            def _():
                ol_ref[...] = y
        else:
            for s, (_, _, gt_ref) in enumerate(seg_mods):
                rows = slice(s * TM, (s + 1) * TM)
                o_ref[rows, :] = x_ref[rows, :] + 0.5 * gt_ref[...] * acc_scr[rows, :]


def _ffn(x, mods, layer, sub, norm_w, w_gate, w_up, w_down, mix=None, final_norm=None):
    first = isinstance(x, tuple)
    final = final_norm is not None
    n_seg = 2 if (not first and mix is None and not final) else 1
    tm = n_seg * TM
    row = lambda width: pl.BlockSpec((tm, width), lambda i, j: (i, 0))
    if first:
        in_specs = [_ctx_rows(TM, D_MODEL), _lat_rows(TM, D_MODEL),
                    pl.BlockSpec((TM, D_MODEL), lambda i, j: (0, 0))]
        args = list(x)
    else:
        in_specs, args = [row(D_MODEL)], [x]
    mix_counts, mix_split = (), 0
    if mix is not None:
        ya, yb, w_out, w_index = mix
        counts = []
        for y in (ya, yb):
            if y is None:
                continue
            parts = y if isinstance(y, tuple) else (y,)
            width = parts[0].shape[1]
            in_specs += [row(width)] if len(parts) == 1 else [_ctx_rows(TM, width), _lat_rows(TM, width)]
            args += list(parts)
            counts.append(len(parts))
        mix_counts = tuple(counts)
        mix_split = (ya[0] if isinstance(ya, tuple) else ya).shape[1]
        in_specs += [_mod_spec(layer, 5),
                     pl.BlockSpec((None, D_MODEL, D_MODEL), lambda i, j: (w_index, 0, 0),
                                  pipeline_mode=pl.Buffered(1))]
        args += [mods, w_out]
    in_specs.append(_layer_vec(layer))
    args.append(norm_w.reshape(DEPTH, 1, D_MODEL))
    for seg in range(n_seg):
        in_specs += [_mod_spec(layer, 3 * sub + k, TM, n_seg, seg) for k in range(3)]
        args += [mods, mods, mods]
    per_step = 1 if final else 2
    n_blocks = D_FF // TF
    for k in range(per_step):
        blk = lambda j, k=k: jnp.minimum(per_step * j + k, n_blocks - 1)
        in_specs += [pl.BlockSpec((None, D_MODEL, TF), lambda i, j, blk=blk: (layer, 0, blk(j))),
                     pl.BlockSpec((None, D_MODEL, TF), lambda i, j, blk=blk: (layer, 0, blk(j))),
                     pl.BlockSpec((None, TF, D_MODEL), lambda i, j, blk=blk: (layer, blk(j), 0))]
        args += [w_gate, w_up, w_down]
    scratch = [pltpu.VMEM((tm, D_MODEL), BF16)]
    if n_seg == 1:
        scratch.append(pltpu.VMEM((tm, D_MODEL), F32))
    if first or mix_counts:
        scratch.append(pltpu.VMEM((TM, D_MODEL), F32))
    if final:
        in_specs.append(pl.BlockSpec((1, D_MODEL), lambda i, j: (0, 0)))
        args.append(final_norm[None])
        out_specs = [_ctx_rows(TM, D_MODEL), _lat_rows(TM, D_MODEL)]
        out_shape = [jax.ShapeDtypeStruct((CTX_TOK, D_MODEL), F32), jax.ShapeDtypeStruct((LAT_TOK, D_MODEL), F32)]
    else:
        out_specs = row(D_MODEL)
        out_shape = jax.ShapeDtypeStruct((N_TOK, D_MODEL), F32)
    return pl.pallas_call(
        functools.partial(_ffn_kernel, first=first, mix_counts=mix_counts, mix_split=mix_split, final=final,
                          n_seg=n_seg, per_step=per_step),
        grid=(N_TOK // tm, pl.cdiv(n_blocks, per_step)),
        in_specs=in_specs,
        out_specs=out_specs,
        out_shape=out_shape,
        scratch_shapes=scratch,
        compiler_params=_params("arbitrary", "arbitrary"),
        name=f"ffn{sub}_{layer}",
    )(*args)


def _mixin_kernel(x_ref, nw_ref, sh_ref, sc_ref, w_ref, *rest, segments, transposed):
    o_refs, wb_scr = rest[:-1], rest[-1]

    @pl.when(pl.program_id(0) == 0)
    def _():
        wb_scr[...] = w_ref[...].astype(BF16)

    h = _norm_mod(x_ref[...], nw_ref[...], sc_ref[...], sh_ref[...])
    for (lo, hi), o_ref in zip(segments, o_refs):
        if transposed:
            o_ref[...] = _dot_nt(h, wb_scr[lo:hi, :])
        else:
            o_ref[...] = _dot(h, wb_scr[:, lo:hi])
    if transposed:
        lo, hi = segments[-1]
        o_refs[-1][...] = _dot_nt(wb_scr[lo:hi, :], h)


def _mixin(x, mods, layer, norm_w, w_in, w_index, segments, tm, transposed=False):
    shape = w_in.shape[1:]
    out_specs = [pl.BlockSpec((tm, hi - lo), lambda i: (i, 0)) for lo, hi in segments]
    out_shape = [jax.ShapeDtypeStruct((N_TOK, hi - lo), F32) for lo, hi in segments]
    if transposed:
        lo, hi = segments[-1]
        out_specs.append(pl.BlockSpec((hi - lo, tm), lambda i: (0, i)))
        out_shape.append(jax.ShapeDtypeStruct((hi - lo, N_TOK), F32))
    return pl.pallas_call(
        functools.partial(_mixin_kernel, segments=segments, transposed=transposed),
        grid=(N_TOK // tm,),
        in_specs=[pl.BlockSpec((tm, D_MODEL), lambda i: (i, 0)),
                  _layer_vec(layer), _mod_spec(layer, 3, tm), _mod_spec(layer, 4, tm),
                  pl.BlockSpec((None,) + shape, lambda i: (w_index, 0, 0), pipeline_mode=pl.Buffered(1))],
        out_specs=out_specs,
        out_shape=out_shape,
        scratch_shapes=[pltpu.VMEM(shape, BF16)],
        compiler_params=_params("arbitrary"),
        name=f"mixin_{layer}",
    )(x, norm_w.reshape(DEPTH, 1, D_MODEL), mods, mods, w_in)


def _pool_kernel(p_ref, w_ref, scale_ref, o_ref):
    seq = jnp.where(pl.program_id(0) < CTX_TILES, SEQ, DEC_SEQ)
    pos = lax.broadcasted_iota(jnp.int32, (TM, LANES), 0) & (seq - 1)
    second = lax.broadcasted_iota(jnp.int32, (TM, LANES), 1) >= POOL_CH

    def before(a, s):
        return jnp.where(pos >= s, pltpu.roll(a, s, axis=0), 0.0)

    def after(a, s):
        return jnp.where(pos < seq - s, pltpu.roll(a, TM - s, axis=0), 0.0)

    groups_per_tile = LANES // POOL_CH
    diffs = []
    for k in range(A_W // LANES):
        p = p_ref[:, k * LANES:(k + 1) * LANES]
        h_a, h_b = (size // 2 for size in POOL_SIZES[k * groups_per_tile:(k + 1) * groups_per_tile])
        sums = {}
        f, b, h = p, before(p, 1), 1
        sums[1] = f + b
        while h < max(h_a, h_b):
            f = f + after(f, h)
            b = b + before(b, h)
            h *= 2
            sums[h] = f + b
        wsum = jnp.where(second, sums[h_b], sums[h_a])
        half = jnp.where(second, h_b, h_a)
        cnt = (jnp.minimum(pos + half, seq) - jnp.maximum(pos - half, 0)).astype(F32)
        diffs.append(wsum / cnt - p)
    d = jnp.concatenate(diffs, axis=1)
    o_ref[...] = (_dot(d, w_ref[...]) * scale_ref[...]).astype(o_ref.dtype)


def _pool(p_pool, w_bd, scale):
    return pl.pallas_call(
        _pool_kernel,
        grid=(N_ROW_TILES,),
        in_specs=[pl.BlockSpec((TM, A_W), lambda i: (i, 0)),
                  pl.BlockSpec((A_W, A_W), lambda i: (0, 0)),
                  pl.BlockSpec((1, A_W), lambda i: (0, 0))],
        out_specs=pl.BlockSpec((TM, A_W), lambda i: (i, 0)),
        out_shape=jax.ShapeDtypeStruct((N_TOK, A_W), BF16),
        compiler_params=_params("arbitrary"),
        name="pool",
    )(p_pool, w_bd, scale)


def _link_mask(r, c, s, upper):
    lg = s.bit_length() - 1
    same = (r >> (lg + 1)) == (c >> (lg + 1))
    r_half = (r >> lg) & 1
    c_half = (c >> lg) & 1
    return same & ((r_half == 0) & (c_half == 1) if upper else (r_half == 1) & (c_half == 0))


def _dn_kernel(q_ref, k_ref, v_ref, z_ref, g_ref, gt_ref, cq_ref, ck_ref, cv_ref, alog_ref, dtb_ref,
               alog_c_ref, dtb_c_ref, nw_ref, *rest, seq, heads, zero_init, state_slot):
    rest = list(rest)
    s0_ref = None if zero_init else rest.pop(0)
    if state_slot:
        rest.pop(0)
    o_ref = rest.pop(0)
    st_ref = rest.pop(0) if state_slot is not None else None
    gct_scr = rest.pop(0)
    h0 = pl.program_id(1) * heads
    n = DN_CHUNK
    n_blk = seq // n
    row = lax.broadcasted_iota(jnp.int32, (seq, LANES), 0)
    g_lane = lax.broadcasted_iota(jnp.int32, (seq, GATE_W), 1)
    r2 = lax.broadcasted_iota(jnp.int32, (n, n), 0)
    c2 = lax.broadcasted_iota(jnp.int32, (n, n), 1)
    eye = (r2 == c2).astype(F32)
    incl = (r2 >= c2, r2 <= c2)
    strict = (r2 > c2, r2 < c2)
    levels = [1 << b for b in range(n.bit_length() - 1)]
    link = {(s, d): _link_mask(r2, c2, s, d == 1) for s in levels for d in (0, 1)}
    units = [(hh, blk, d) for hh in range(heads) for blk in range(n_blk) for d in (0, 1)]

    def conv_silu(x, cw):
        prev = jnp.where(row >= 1, pltpu.roll(x, 1, axis=0), 0.0)
        nxt = jnp.where(row <= seq - 2, pltpu.roll(x, seq - 1, axis=0), 0.0)
        return _silu(prev * cw[0:1] + x * cw[1:2] + nxt * cw[2:3])

    def l2n(x):
        return x * lax.rsqrt(jnp.sum(x * x, axis=-1, keepdims=True) + EPS)

    def col(a, idx):
        return jnp.sum(jnp.where(g_lane == idx, a, 0.0), axis=1, keepdims=True)

    def log_decay(a, alog, dtb):
        xg = a + dtb
        return -jnp.exp(alog) * (jnp.maximum(xg, 0.0) + jnp.log1p(jnp.exp(-jnp.abs(xg))))

    gates = g_ref[...]
    beta_all = jax.nn.sigmoid(gates)
    g_all = log_decay(gates, alog_ref[...], dtb_ref[...])
    g_all_t = log_decay(gt_ref[...], alog_c_ref[...], dtb_c_ref[...])
    lower = incl[0].astype(BF16)
    upper = incl[1].astype(BF16)
    backward_rows = lax.broadcasted_iota(jnp.int32, (GATE_W, n), 0) >= 3 * DN_HEADS

    def tri_sums(a, left):
        rem, pre_sum, suf_sum = a, 0.0, 0.0
        for _ in range(3):
            piece = rem.astype(BF16)
            rem = rem - piece.astype(F32)
            if left:
                pre_sum = pre_sum + jnp.dot(lower, piece, preferred_element_type=F32)
                suf_sum = suf_sum + jnp.dot(upper, piece, preferred_element_type=F32)
            else:
                pre_sum = pre_sum + jnp.dot(piece, upper, preferred_element_type=F32)
                suf_sum = suf_sum + jnp.dot(piece, lower, preferred_element_type=F32)
        return pre_sum, suf_sum

    pre_blocks, suf_blocks = [], []
    for blk in range(n_blk):
        rs = slice(blk * n, (blk + 1) * n)
        p, s_ = tri_sums(g_all[rs, :], left=True)
        pre_blocks.append(p)
        suf_blocks.append(s_)
        p_t, s_t = tri_sums(g_all_t[:, rs], left=False)
        sums = jnp.where(backward_rows, s_t, p_t)
        for r in range(2 * DN_HEADS):
            gct_scr[r, :, rs] = sums[2 * DN_HEADS + r:2 * DN_HEADS + r + 1, :]
    pre = jnp.concatenate(pre_blocks, axis=0)
    suf = jnp.concatenate(suf_blocks, axis=0)

    qs, kn, vv, beta, gc = [], [], [], [], []
    for hh in range(heads):
        cs = slice(hh * LANES, (hh + 1) * LANES)
        qs.append(l2n(conv_silu(q_ref[:, cs], cq_ref[:, cs])) * (DN_DK ** -0.5))
        kn.append(l2n(conv_silu(k_ref[:, cs], ck_ref[:, cs])))
        vv.append(conv_silu(v_ref[:, cs], cv_ref[:, cs]))
        beta.append((col(beta_all, h0 + hh), col(beta_all, DN_HEADS + h0 + hh)))
        gc.append((col(pre, 2 * DN_HEADS + h0 + hh), col(suf, 3 * DN_HEADS + h0 + hh)))

    kk, qk = {}, {}
    for hh in range(heads):
        for blk in range(n_blk):
            rs = slice(blk * n, (blk + 1) * n)
            kk[hh, blk] = _dot_nt(kn[hh][rs], kn[hh][rs])
            qk[hh, blk] = _dot_nt(qs[hh][rs], kn[hh][rs])

    m, m_hi, t, a_in = {}, {}, {}, {}
    for u in units:
        hh, blk, d = u
        rs = slice(blk * n, (blk + 1) * n)
        g_lanes = gct_scr[d * DN_HEADS + h0 + hh, :, rs]
        decay = jnp.where(incl[d], jnp.exp(gc[hh][d][rs] - g_lanes), 0.0)
        m[u] = jnp.where(strict[d], beta[hh][d][rs] * kk[hh, blk] * decay, 0.0)
        m_hi[u] = m[u].astype(BF16)
        a_in[u] = (qk[hh, blk] * decay).astype(BF16)
        t[u] = eye - jnp.where(link[1, d], m[u], 0.0)

    for s in levels[1:]:
        tb, x = {}, {}
        for u in units:
            tb[u] = t[u].astype(BF16)
            c_s = jnp.where(link[s, u[2]], m_hi[u], jnp.zeros_like(m_hi[u]))
            x[u] = jnp.dot(c_s, tb[u], preferred_element_type=F32).astype(BF16)
        for u in units:
            t[u] = t[u] - jnp.dot(tb[u], x[u], preferred_element_type=F32)

    uw, q_dec, k_dec, g_last = {}, {}, {}, {}
    for u in units:
        hh, blk, d = u
        rs = slice(blk * n, (blk + 1) * n)
        gcol = gc[hh][d][rs]
        bcol = beta[hh][d][rs]
        e_g = jnp.exp(gcol)
        kb = kn[hh][rs]
        uw[u] = _dot(t[u], jnp.concatenate([vv[hh][rs] * bcol, kb * (bcol * e_g)], axis=1))
        g_last[u] = gcol[n - 1:n] if d == 0 else gcol[0:1]
        q_dec[u] = qs[hh][rs] * e_g
        k_dec[u] = kb * jnp.exp(g_last[u] - gcol)

    state = {}
    for hh in range(heads):
        for d in (0, 1):
            state[hh, d] = jnp.zeros((DN_DK, DN_DK), F32) if zero_init else s0_ref[d, hh]
    outs = {}
    for step in range(n_blk):
        chains = [(hh, step if d == 0 else n_blk - 1 - step, d) for hh in range(heads) for d in (0, 1)]
        ws_qs = {}
        for u in chains:
            ws_qs[u] = _dot(jnp.concatenate([uw[u][:, LANES:], q_dec[u]], axis=0), state[u[0], u[2]])
        for u in chains:
            v_new = (uw[u][:, :LANES] - ws_qs[u][0:n]).astype(BF16)
            outs[u] = ws_qs[u][n:] + jnp.dot(a_in[u], v_new, preferred_element_type=F32)
            state[u[0], u[2]] = state[u[0], u[2]] * jnp.exp(g_last[u]) + _dot_tn(k_dec[u], v_new)

    for hh in range(heads):
        cs = slice(hh * LANES, (hh + 1) * LANES)
        o = (jnp.concatenate([outs[hh, blk, 0] for blk in range(n_blk)], axis=0)
             + jnp.concatenate([outs[hh, blk, 1] for blk in range(n_blk)], axis=0))
        o = o * lax.rsqrt(jnp.mean(o * o, axis=-1, keepdims=True) + EPS) * nw_ref[...]
        o_ref[:, cs] = (o * _silu(z_ref[:, cs])).astype(o_ref.dtype)
        if state_slot == 0:
            st_ref[0, 0, hh] = state[hh, 0]
            st_ref[0, 1, hh] = state[hh, 1]
            st_ref[1:, :, hh] = jnp.zeros((N_EVEN - 1, 2, DN_DK, DN_DK), F32)
        elif state_slot:
            st_ref[0, hh] = state[hh, 0]
            st_ref[1, hh] = state[hh, 1]


def _deltanet(qkvz, gates, gates_t, conv_w, e, alog, dtb, norm_w, s0, seq, heads, n_seq, row_block0,
              state_slot=None, states=None):
    zero_init = s0 is None
    width = heads * LANES
    per = DN_HEADS // heads
    alog_row = jnp.concatenate([jnp.zeros((2 * DN_HEADS,), F32), alog.reshape(-1)])[None]
    dtb_row = jnp.concatenate([jnp.zeros((2 * DN_HEADS,), F32), dtb.reshape(-1)])[None]

    def pcol(cb):
        return pl.BlockSpec((seq, width), lambda b, h: (row_block0 + b, cb * per + h))

    def ccol(cb):
        return pl.BlockSpec((None, 3, width), lambda b, h: (e, 0, cb * per + h))

    grow = pl.BlockSpec((1, GATE_W), lambda b, h: (0, 0))
    gcol = pl.BlockSpec((GATE_W, 1), lambda b, h: (0, 0))
    in_specs = [pcol(COL_Q), pcol(COL_K), pcol(COL_V), pcol(COL_Z),
                pl.BlockSpec((seq, GATE_W), lambda b, h: (row_block0 + b, 0)),
                pl.BlockSpec((GATE_W, seq), lambda b, h: (0, row_block0 + b)),
                ccol(0), ccol(1), ccol(2), grow, grow, gcol, gcol,
                pl.BlockSpec((None, 1, LANES), lambda b, h: (e, 0, 0))]
    args = [qkvz, qkvz, qkvz, qkvz, gates, gates_t, conv_w, conv_w, conv_w, alog_row, dtb_row,
            alog_row.T, dtb_row.T, norm_w.reshape(N_EVEN, 1, LANES)]
    if not zero_init:
        in_specs.append(pl.BlockSpec((None, None, 2, heads, DN_DK, DN_DK), lambda b, h: (b, e, 0, h, 0, 0)))
        args.append(s0)
    aliases = {}
    if state_slot:
        aliases = {len(args): 1}
        in_specs.append(pl.BlockSpec(memory_space=pl.ANY))
        args.append(states)
    out_specs = [pl.BlockSpec((seq, width), lambda b, h: (b, h))]
    out_shape = [jax.ShapeDtypeStruct((n_seq * seq, B_W), BF16)]
    if state_slot is not None:
        if state_slot == 0:
            out_specs.append(pl.BlockSpec((None, N_EVEN, 2, heads, DN_DK, DN_DK), lambda b, h: (b, 0, 0, h, 0, 0)))
        else:
            out_specs.append(pl.BlockSpec((None, None, 2, heads, DN_DK, DN_DK),
                                          lambda b, h: (b, state_slot, 0, h, 0, 0)))
        out_shape.append(jax.ShapeDtypeStruct((n_seq, N_EVEN, 2, DN_HEADS, DN_DK, DN_DK), F32))
    res = pl.pallas_call(
        functools.partial(_dn_kernel, seq=seq, heads=heads, zero_init=zero_init, state_slot=state_slot),
        grid=(n_seq, per),
        in_specs=in_specs,
        out_specs=out_specs,
        out_shape=out_shape,
        scratch_shapes=[pltpu.VMEM((2 * DN_HEADS, 1, seq), F32)],
        input_output_aliases=aliases,
        compiler_params=_params("arbitrary", "arbitrary"),
        name=f"deltanet_{seq}",
    )(*args)
    return res if state_slot is not None else (res[0], None)


def _oddmix_kernel(u_ref, v_ref, f_ref, nw_ref, ws_ref, bt_ref, ch_hi_ref, ch_lo_ref, ctx_hi_ref, ctx_lo_ref,
                   lat_hi_ref, lat_lo_ref, w_ref, o_ref):
    ctx = pl.program_id(0) < CTX_TILES

    def tile(windows):
        fcs = _dot3(_split(f_ref[...]), (ch_hi_ref[...], ch_lo_ref[...]))
        wb = w_ref[...].astype(BF16)
        for rows, hi_ref, lo_ref in windows:
            stacked = jnp.concatenate([fcs[rows, :F_W], fcs[rows, F_W:]], axis=0)
            spec = _dot3((hi_ref[...], lo_ref[...]), _split(stacked))
            o_ref[rows, C_W:] = jnp.dot(spec.astype(BF16), wb, preferred_element_type=F32).astype(o_ref.dtype)

        u = jax.nn.gelu(u_ref[...])
        v = jax.nn.gelu(v_ref[...])
        mu = jnp.mean(v, axis=-1, keepdims=True)
        vc = v - mu
        var = jnp.mean(vc * vc, axis=-1, keepdims=True)
        vn = vc * lax.rsqrt(var + EPS) * nw_ref[...]
        bt = bt_ref[...]
        for c in range(TM // SGU_CHUNK):
            rs = slice(c * SGU_CHUNK, (c + 1) * SGU_CHUNK)
            for hd in range(SGU_HEADS):
                cs = slice(hd * LANES, (hd + 1) * LANES)
                mixed = _dot(ws_ref[hd], vn[rs, cs]) + bt[:, hd:hd + 1]
                o_ref[rs, cs] = (u[rs, cs] * mixed).astype(o_ref.dtype)

    @pl.when(ctx)
    def _():
        tile([(slice(s * SEQ, (s + 1) * SEQ), ctx_hi_ref, ctx_lo_ref) for s in range(TM // SEQ)])

    @pl.when(jnp.logical_not(ctx))
    def _():
        tile([(slice(0, DEC_SEQ), lat_hi_ref, lat_lo_ref)])


def _oddmix(uv, f_all, norm_w, w_s, b_t, tables, w_bd):
    const = lambda a: pl.BlockSpec(a.shape, lambda i: (0,) * a.ndim)
    return pl.pallas_call(
        _oddmix_kernel,
        grid=(N_ROW_TILES,),
        in_specs=([pl.BlockSpec((TM, C_W), lambda i: (i, 0)),
                   pl.BlockSpec((TM, C_W), lambda i: (i, 1)),
                   pl.BlockSpec((TM, F_W), lambda i: (i, 0)),
                   const(norm_w), const(w_s), const(b_t)] + [const(t) for t in tables] + [const(w_bd)]),
        out_specs=pl.BlockSpec((TM, C_W + F_W), lambda i: (i, 0)),
        out_shape=jax.ShapeDtypeStruct((N_TOK, C_W + F_W), BF16),
        compiler_params=_params("arbitrary"),
        name="oddmix",
    )(uv, uv, f_all, norm_w, w_s, b_t, *tables, w_bd)


def _grid_pos_embed(n_tok):
    rows = n_tok // GRID_W
    r = np.repeat(np.arange(rows, dtype=np.float64), GRID_W)
    col = np.tile(np.arange(GRID_W, dtype=np.float64), rows)
    quarter = D_MODEL // 4
    freq = np.exp(-math.log(10000.0) * np.arange(quarter, dtype=np.float64) / quarter)

    def emb(p):
        ang = p[:, None] * freq[None, :]
        return np.concatenate([np.sin(ang), np.cos(ang)], axis=-1)

    return np.concatenate([emb(r), emb(col)], axis=-1).astype(np.float32)


def _dft_tables(n):
    k = np.arange(n, dtype=np.int64)
    ang = ((k[:, None] * k[None, :]) % n).astype(np.float64) * (2.0 * math.pi / n)
    scale = n ** -0.5
    return np.cos(ang) * scale, np.sin(ang) * scale


def _host_split(a):
    hi = a.astype(np.float32).astype(BF16)
    lo = (a - hi.astype(np.float64)).astype(np.float32).astype(BF16)
    return hi, lo


def _fnet_tables():
    cc, sc = _dft_tables(FN_CH)
    eye = np.eye(F_W // FN_CH)
    tables = list(_host_split(np.concatenate([np.kron(eye, cc), np.kron(eye, sc)], axis=1)))
    for n in (SEQ, DEC_SEQ):
        cn, sn = _dft_tables(n)
        tables += list(_host_split(np.concatenate([cn, -sn], axis=1)))
    return tables


def _block_diag(blocks):
    g, a, b = blocks.shape
    eye = jnp.eye(g, dtype=blocks.dtype)
    return (eye[:, None, :, None] * blocks[:, :, None, :]).reshape(g * a, g * b)


def kernel(x_prompt, x_sample, state_delta, c, c_ctx, ffn1_norm, ffn1_w_gate, ffn1_w_up, ffn1_w_down,
           mix_norm, ffn2_norm, ffn2_w_gate, ffn2_w_up, ffn2_w_down, ada_w, ada_b, ev_w_in, ev_w_out,
           pool_w, pool_scale, dn_conv_w, dn_a_log, dn_dt_bias, dn_norm_w, od_w_in, od_w_out, sgu_norm,
           sgu_w, sgu_b, fnet_w, final_norm):
    cond8 = jnp.zeros((SUBLANES, D_MODEL), F32).at[0].set(c_ctx).at[1:1 + DEC_BATCH].set(c)
    mods = _ada(cond8, ada_w, ada_b)
    mods = mods[:, :1 + DEC_BATCH].reshape(DEPTH, 1 + DEC_BATCH, N_MOD, 1, D_MODEL).transpose(0, 2, 1, 3, 4)

    fnet_tables = _fnet_tables()
    groups = ((SEQ, BATCH, 0, DN_HEADS_PER_STEP[SEQ]),
              (DEC_SEQ, DEC_BATCH, CTX_TOK // DEC_SEQ, DN_HEADS_PER_STEP[DEC_SEQ]))
    even_segments = ((A_W, A_W + 4 * QK_W), (0, A_W), (A_W + 4 * QK_W, P_EVEN))
    odd_segments = ((0, 2 * C_W), (2 * C_W, P_ODD))

    x = (x_prompt.reshape(CTX_TOK, D_MODEL), x_sample.reshape(LAT_TOK, D_MODEL), _grid_pos_embed(DEC_SEQ))
    ev_w_in_t = jnp.swapaxes(ev_w_in, 1, 2)
    states = None
    for layer in range(DEPTH):
        x = _ffn(x, mods, layer, 0, ffn1_norm, ffn1_w_gate, ffn1_w_up, ffn1_w_down)
        if layer % 2 == 0:
            e = layer // 2
            qkvz, p_pool, gates, gates_t = _mixin(x, mods, layer, mix_norm, ev_w_in_t, e, even_segments,
                                                  TM_EVEN_PROJ, transposed=True)
            w_bd = _block_diag(pool_w[e])
            ya = _pool(p_pool, w_bd, pool_scale[e][None])
            yb = []
            for seq, n_seq, rb0, dn_heads in groups:
                ctx = rb0 == 0
                o, st = _deltanet(qkvz, gates, gates_t, dn_conv_w, e, dn_a_log[e], dn_dt_bias[e], dn_norm_w,
                                  None if ctx else state_delta, seq, dn_heads, n_seq, rb0,
                                  state_slot=e if ctx else None, states=states)
                yb.append(o)
                if ctx:
                    states = st
            mix = (ya, tuple(yb), ev_w_out, e)
        else:
            j = layer // 2
            uv, f_all = _mixin(x, mods, layer, mix_norm, od_w_in, j, odd_segments, TM)
            y_cd = _oddmix(uv, f_all, sgu_norm[j][None], sgu_w[j], sgu_b[j].T, fnet_tables, _block_diag(fnet_w[j]))
            mix = (y_cd, None, od_w_out, j)
        x = _ffn(x, mods, layer, 2, ffn2_norm, ffn2_w_gate, ffn2_w_up, ffn2_w_down, mix=mix,
                 final_norm=final_norm if layer == DEPTH - 1 else None)

    y_prompt, y_sample = x
    return (y_prompt.reshape(BATCH, SEQ, D_MODEL), y_sample.reshape(DEC_BATCH, DEC_SEQ, D_MODEL), states)
```

```python
import functools
import math

import numpy as np
import jax
import jax.numpy as jnp
from jax import lax
from jax.experimental import pallas as pl
from jax.experimental.pallas import tpu as pltpu

F32 = jnp.float32
BF16 = jnp.bfloat16

D_MODEL = 1024
BATCH = 16
SEQ = 256
DEPTH = 4
DEC_BATCH = 2
DEC_SEQ = 1024
GRID_W = 64
EPS = 1e-6
D_FF = 2816
N_MOD = 9
N_EVEN = (DEPTH + 1) // 2
POOL_SIZES = (2, 4, 8, 16)
POOL_CH = 64
A_W = 256
DN_HEADS = 6
DN_DK = 128
QK_W = DN_HEADS * DN_DK
B_W = QK_W
GATE_W = 4 * DN_HEADS
P_EVEN = A_W + 4 * QK_W + GATE_W
SGU_CHUNK = 128
SGU_HEADS = 6
C_W = 768
FN_CH = 64
F_W = 256
P_ODD = 2 * C_W + F_W

CTX_TOK = BATCH * SEQ
LAT_TOK = DEC_BATCH * DEC_SEQ
N_TOK = CTX_TOK + LAT_TOK

LANES = 128
SUBLANES = 8
VMEM_LIMIT = 56 * 1024 * 1024

TM = 1024
N_ROW_TILES = N_TOK // TM
CTX_TILES = CTX_TOK // TM
TF = 256
TM_EVEN_PROJ = 512
ADA_TN = 2304
DN_CHUNK = 128
DN_HEADS_PER_STEP = {SEQ: DN_HEADS, DEC_SEQ: 3}
COL_Q, COL_K, COL_V, COL_Z = 0, 1, 2, 3


def _cond_of_tile(i, tm):
    return jnp.where(i < CTX_TOK // tm, 0, 1 + (i - CTX_TOK // tm) * tm // DEC_SEQ)


def _silu(x):
    half = 0.5 * x
    return half + half * jnp.tanh(half)


def _dot(a, b):
    return jnp.dot(a.astype(BF16), b.astype(BF16), preferred_element_type=F32)


def _dot_nt(a, b):
    return lax.dot_general(a.astype(BF16), b.astype(BF16), (((1,), (1,)), ((), ())),
                           preferred_element_type=F32)


def _dot_tn(a, b):
    return lax.dot_general(a.astype(BF16), b.astype(BF16), (((0,), (0,)), ((), ())),
                           preferred_element_type=F32)


def _split(a):
    hi = a.astype(BF16)
    lo = (a - hi.astype(F32)).astype(BF16)
    return hi, lo


def _dot3(a, b):
    ah, al = a
    bh, bl = b
    return (jnp.dot(ah, bh, preferred_element_type=F32)
            + (jnp.dot(al, bh, preferred_element_type=F32)
               + jnp.dot(ah, bl, preferred_element_type=F32)))


def _rms(x, w):
    return x * lax.rsqrt(jnp.mean(x * x, axis=-1, keepdims=True) + EPS) * w


def _norm_mod(x, nw, scale, shift):
    r = lax.rsqrt(jnp.mean(x * x, axis=-1, keepdims=True) + EPS)
    return ((x * r) * (nw * (1.0 + scale)) + shift).astype(BF16)


def _params(*sem):
    return pltpu.CompilerParams(dimension_semantics=sem, vmem_limit_bytes=VMEM_LIMIT)


def _ctx_rows(tm, width):
    return pl.BlockSpec((tm, width), lambda i, *_: (jnp.minimum(i, CTX_TOK // tm - 1), 0))


def _lat_rows(tm, width):
    return pl.BlockSpec((tm, width), lambda i, *_: (jnp.maximum(i - CTX_TOK // tm, 0), 0))


def _ada_kernel(c_ref, w_ref, b_ref, o_ref):
    o_ref[0] = _dot(_silu(c_ref[...]), w_ref[0]) + b_ref[0]


def _ada(cond8, ada_w, ada_b):
    tn = ADA_TN
    n = N_MOD * D_MODEL
    return pl.pallas_call(
        _ada_kernel,
        grid=(DEPTH, n // tn),
        in_specs=[pl.BlockSpec((SUBLANES, D_MODEL), lambda l, j: (0, 0)),
                  pl.BlockSpec((1, D_MODEL, tn), lambda l, j: (l, 0, j)),
                  pl.BlockSpec((1, 1, tn), lambda l, j: (l, 0, j))],
        out_specs=pl.BlockSpec((1, SUBLANES, tn), lambda l, j: (l, 0, j)),
        out_shape=jax.ShapeDtypeStruct((DEPTH, SUBLANES, n), F32),
        compiler_params=_params("arbitrary", "arbitrary"),
        name="ada",
    )(cond8, ada_w, ada_b.reshape(DEPTH, 1, n))


def _mod_spec(layer, which, tm=TM, n_seg=1, seg=0):
    return pl.BlockSpec((None, None, None, 1, D_MODEL),
                        lambda i, *_: (layer, which, _cond_of_tile(i * n_seg + seg, tm), 0, 0))


def _layer_vec(layer):
    return pl.BlockSpec((None, 1, D_MODEL), lambda i, *_: (layer, 0, 0))


def _ffn_kernel(*refs, first, mix_counts, mix_split, final, n_seg, per_step):
    refs = list(refs)
    take = lambda k: [refs.pop(0) for _ in range(k)]
    if first:
        xp_ref, xs_ref, pos_ref = take(3)
    else:
        (x_in_ref,) = take(1)
    if mix_counts:
        mix_groups = [take(c) for c in mix_counts]
        g2_ref, wo_ref = take(2)
    (nw_ref,) = take(1)
    seg_mods = [take(3) for _ in range(n_seg)]
    w_blocks = [take(3) for _ in range(per_step)]
    if final:
        fn_ref, oc_ref, ol_ref = take(3)
    else:
        (o_ref,) = take(1)
    (h_scr,) = take(1)
    acc_scr = o_ref if n_seg > 1 else refs.pop(0)
    x_ref = refs.pop(0) if (first or mix_counts) else x_in_ref
    i = pl.program_id(0)
    j = pl.program_id(1)
    ctx = i < CTX_TILES

    @pl.when(j == 0)
    def _():
        if first:
            @pl.when(ctx)
            def _():
                x_ref[...] = xp_ref[...]

            @pl.when(jnp.logical_not(ctx))
            def _():
                x_ref[...] = xs_ref[...] + pos_ref[...]
        elif mix_counts:
            def pick(rs):
                return rs[0][...] if len(rs) == 1 else jnp.where(ctx, rs[0][...], rs[1][...])

            if len(mix_groups) == 1:
                mix = _dot(pick(mix_groups[0]), wo_ref[...])
            else:
                mix = (_dot(pick(mix_groups[0]), wo_ref[0:mix_split, :])
                       + _dot(pick(mix_groups[1]), wo_ref[mix_split:, :]))
            x_ref[...] = x_in_ref[...] + g2_ref[...] * mix

        for s, (sh_ref, sc_ref, _) in enumerate(seg_mods):
            rows = slice(s * TM, (s + 1) * TM)
            h_scr[rows, :] = _norm_mod(x_ref[rows, :], nw_ref[...], sc_ref[...], sh_ref[...])
        acc_scr[...] = jnp.zeros_like(acc_scr)

    def accumulate(blocks):
        h = h_scr[...]
        total = None
        for wg_ref, wu_ref, wd_ref in blocks:
            g = jnp.dot(h, wg_ref[...].astype(BF16), preferred_element_type=F32)
            u = jnp.dot(h, wu_ref[...].astype(BF16), preferred_element_type=F32)
            part = _dot(_silu(g) * u, wd_ref[...])
            total = part if total is None else total + part
        acc_scr[...] += total

    last = pl.num_programs(1) - 1
    if per_step == 1:
        accumulate(w_blocks)
    else:
        @pl.when(j < last)
        def _():
            accumulate(w_blocks)

        @pl.when(j == last)
        def _():
            accumulate(w_blocks[:1])

    @pl.when(j == last)
    def _():
        if final:
            y = _rms(x_ref[...] + 0.5 * seg_mods[0][2][...] * acc_scr[...], fn_ref[...])

            @pl.when(ctx)
            def _():
                oc_ref[...] = y

            @pl.when(jnp.logical_not(ctx))
            def _():
                ol_ref[...] = y
        else:
            for s, (_, _, gt_ref) in enumerate(seg_mods):
                rows = slice(s * TM, (s + 1) * TM)
                o_ref[rows, :] = x_ref[rows, :] + 0.5 * gt_ref[...] * acc_scr[rows, :]


def _ffn(x, mods, layer, sub, norm_w, w_gate, w_up, w_down, mix=None, final_norm=None):
    first = isinstance(x, tuple)
    final = final_norm is not None
    n_seg = 2 if (not first and mix is None and not final) else 1
    tm = n_seg * TM
    row = lambda width: pl.BlockSpec((tm, width), lambda i, j: (i, 0))
    if first:
        in_specs = [_ctx_rows(TM, D_MODEL), _lat_rows(TM, D_MODEL),
                    pl.BlockSpec((TM, D_MODEL), lambda i, j: (0, 0))]
        args = list(x)
    else:
        in_specs, args = [row(D_MODEL)], [x]
    mix_counts, mix_split = (), 0
    if mix is not None:
        ya, yb, w_out, w_index = mix
        counts = []
        for y in (ya, yb):
            if y is None:
                continue
            parts = y if isinstance(y, tuple) else (y,)
            width = parts[0].shape[1]
            in_specs += [row(width)] if len(parts) == 1 else [_ctx_rows(TM, width), _lat_rows(TM, width)]
            args += list(parts)
            counts.append(len(parts))
        mix_counts = tuple(counts)
        mix_split = (ya[0] if isinstance(ya, tuple) else ya).shape[1]
        in_specs += [_mod_spec(layer, 5),
                     pl.BlockSpec((None, D_MODEL, D_MODEL), lambda i, j: (w_index, 0, 0),
                                  pipeline_mode=pl.Buffered(1))]
        args += [mods, w_out]
    in_specs.append(_layer_vec(layer))
    args.append(norm_w.reshape(DEPTH, 1, D_MODEL))
    for seg in range(n_seg):
        in_specs += [_mod_spec(layer, 3 * sub + k, TM, n_seg, seg) for k in range(3)]
        args += [mods, mods, mods]
    per_step = 1 if final else 2
    n_blocks = D_FF // TF
    for k in range(per_step):
        blk = lambda j, k=k: jnp.minimum(per_step * j + k, n_blocks - 1)
        in_specs += [pl.BlockSpec((None, D_MODEL, TF), lambda i, j, blk=blk: (layer, 0, blk(j))),
                     pl.BlockSpec((None, D_MODEL, TF), lambda i, j, blk=blk: (layer, 0, blk(j))),
                     pl.BlockSpec((None, TF, D_MODEL), lambda i, j, blk=blk: (layer, blk(j), 0))]
        args += [w_gate, w_up, w_down]
    scratch = [pltpu.VMEM((tm, D_MODEL), BF16)]
    if n_seg == 1:
        scratch.append(pltpu.VMEM((tm, D_MODEL), F32))
    if first or mix_counts:
        scratch.append(pltpu.VMEM((TM, D_MODEL), F32))
    if final:
        in_specs.append(pl.BlockSpec((1, D_MODEL), lambda i, j: (0, 0)))
        args.append(final_norm[None])
        out_specs = [_ctx_rows(TM, D_MODEL), _lat_rows(TM, D_MODEL)]
        out_shape = [jax.ShapeDtypeStruct((CTX_TOK, D_MODEL), F32), jax.ShapeDtypeStruct((LAT_TOK, D_MODEL), F32)]
    else:
        out_specs = row(D_MODEL)
        out_shape = jax.ShapeDtypeStruct((N_TOK, D_MODEL), F32)
    return pl.pallas_call(
        functools.partial(_ffn_kernel, first=first, mix_counts=mix_counts, mix_split=mix_split, final=final,
                          n_seg=n_seg, per_step=per_step),
        grid=(N_TOK // tm, pl.cdiv(n_blocks, per_step)),
        in_specs=in_specs,
        out_specs=out_specs,
        out_shape=out_shape,
        scratch_shapes=scratch,
        compiler_params=_params("arbitrary", "arbitrary"),
        name=f"ffn{sub}_{layer}",
    )(*args)


def _mixin_kernel(x_ref, nw_ref, sh_ref, sc_ref, w_ref, *rest, segments, transposed):
    o_refs, wb_scr = rest[:-1], rest[-1]

    @pl.when(pl.program_id(0) == 0)
    def _():
        wb_scr[...] = w_ref[...].astype(BF16)

    h = _norm_mod(x_ref[...], nw_ref[...], sc_ref[...], sh_ref[...])
    for (lo, hi), o_ref in zip(segments, o_refs):
        if transposed:
            o_ref[...] = _dot_nt(h, wb_scr[lo:hi, :])
        else:
            o_ref[...] = _dot(h, wb_scr[:, lo:hi])
    if transposed:
        lo, hi = segments[-1]
        o_refs[-1][...] = _dot_nt(wb_scr[lo:hi, :], h)


def _mixin(x, mods, layer, norm_w, w_in, w_index, segments, tm, transposed=False):
    shape = w_in.shape[1:]
    out_specs = [pl.BlockSpec((tm, hi - lo), lambda i: (i, 0)) for lo, hi in segments]
    out_shape = [jax.ShapeDtypeStruct((N_TOK, hi - lo), F32) for lo, hi in segments]
    if transposed:
        lo, hi = segments[-1]
        out_specs.append(pl.BlockSpec((hi - lo, tm), lambda i: (0, i)))
        out_shape.append(jax.ShapeDtypeStruct((hi - lo, N_TOK), F32))
    return pl.pallas_call(
        functools.partial(_mixin_kernel, segments=segments, transposed=transposed),
        grid=(N_TOK // tm,),
        in_specs=[pl.BlockSpec((tm, D_MODEL), lambda i: (i, 0)),
                  _layer_vec(layer), _mod_spec(layer, 3, tm), _mod_spec(layer, 4, tm),
                  pl.BlockSpec((None,) + shape, lambda i: (w_index, 0, 0), pipeline_mode=pl.Buffered(1))],
        out_specs=out_specs,
        out_shape=out_shape,
        scratch_shapes=[pltpu.VMEM(shape, BF16)],
        compiler_params=_params("arbitrary"),
        name=f"mixin_{layer}",
    )(x, norm_w.reshape(DEPTH, 1, D_MODEL), mods, mods, w_in)


def _pool_kernel(p_ref, w_ref, scale_ref, o_ref):
    seq = jnp.where(pl.program_id(0) < CTX_TILES, SEQ, DEC_SEQ)
    pos = lax.broadcasted_iota(jnp.int32, (TM, LANES), 0) & (seq - 1)
    second = lax.broadcasted_iota(jnp.int32, (TM, LANES), 1) >= POOL_CH

    def before(a, s):
        return jnp.where(pos >= s, pltpu.roll(a, s, axis=0), 0.0)

    def after(a, s):
        return jnp.where(pos < seq - s, pltpu.roll(a, TM - s, axis=0), 0.0)

    groups_per_tile = LANES // POOL_CH
    diffs = []
    for k in range(A_W // LANES):
        p = p_ref[:, k * LANES:(k + 1) * LANES]
        h_a, h_b = (size // 2 for size in POOL_SIZES[k * groups_per_tile:(k + 1) * groups_per_tile])
        sums = {}
        f, b, h = p, before(p, 1), 1
        sums[1] = f + b
        while h < max(h_a, h_b):
            f = f + after(f, h)
            b = b + before(b, h)
            h *= 2
            sums[h] = f + b
        wsum = jnp.where(second, sums[h_b], sums[h_a])
        half = jnp.where(second, h_b, h_a)
        cnt = (jnp.minimum(pos + half, seq) - jnp.maximum(pos - half, 0)).astype(F32)
        diffs.append(wsum / cnt - p)
    d = jnp.concatenate(diffs, axis=1)
    o_ref[...] = (_dot(d, w_ref[...]) * scale_ref[...]).astype(o_ref.dtype)


def _pool(p_pool, w_bd, scale):
    return pl.pallas_call(
        _pool_kernel,
        grid=(N_ROW_TILES,),
        in_specs=[pl.BlockSpec((TM, A_W), lambda i: (i, 0)),
                  pl.BlockSpec((A_W, A_W), lambda i: (0, 0)),
                  pl.BlockSpec((1, A_W), lambda i: (0, 0))],
        out_specs=pl.BlockSpec((TM, A_W), lambda i: (i, 0)),
        out_shape=jax.ShapeDtypeStruct((N_TOK, A_W), BF16),
        compiler_params=_params("arbitrary"),
        name="pool",
    )(p_pool, w_bd, scale)


def _link_mask(r, c, s, upper):
    lg = s.bit_length() - 1
    same = (r >> (lg + 1)) == (c >> (lg + 1))
    r_half = (r >> lg) & 1
    c_half = (c >> lg) & 1
    return same & ((r_half == 0) & (c_half == 1) if upper else (r_half == 1) & (c_half == 0))


def _dn_kernel(q_ref, k_ref, v_ref, z_ref, g_ref, gt_ref, cq_ref, ck_ref, cv_ref, alog_ref, dtb_ref,
               alog_c_ref, dtb_c_ref, nw_ref, *rest, seq, heads, zero_init, state_slot):
    rest = list(rest)
    s0_ref = None if zero_init else rest.pop(0)
    if state_slot:
        rest.pop(0)
    o_ref = rest.pop(0)
    st_ref = rest.pop(0) if state_slot is not None else None
    gct_scr = rest.pop(0)
    h0 = pl.program_id(1) * heads
    n = DN_CHUNK
    n_blk = seq // n
    row = lax.broadcasted_iota(jnp.int32, (seq, LANES), 0)
    g_lane = lax.broadcasted_iota(jnp.int32, (seq, GATE_W), 1)
    r2 = lax.broadcasted_iota(jnp.int32, (n, n), 0)
    c2 = lax.broadcasted_iota(jnp.int32, (n, n), 1)
    eye = (r2 == c2).astype(F32)
    incl = (r2 >= c2, r2 <= c2)
    strict = (r2 > c2, r2 < c2)
    levels = [1 << b for b in range(n.bit_length() - 1)]
    link = {(s, d): _link_mask(r2, c2, s, d == 1) for s in levels for d in (0, 1)}
    units = [(hh, blk, d) for hh in range(heads) for blk in range(n_blk) for d in (0, 1)]

    def conv_silu(x, cw):
        prev = jnp.where(row >= 1, pltpu.roll(x, 1, axis=0), 0.0)
        nxt = jnp.where(row <= seq - 2, pltpu.roll(x, seq - 1, axis=0), 0.0)
        return _silu(prev * cw[0:1] + x * cw[1:2] + nxt * cw[2:3])

    def l2n(x):
        return x * lax.rsqrt(jnp.sum(x * x, axis=-1, keepdims=True) + EPS)

    def col(a, idx):
        return jnp.sum(jnp.where(g_lane == idx, a, 0.0), axis=1, keepdims=True)

    def log_decay(a, alog, dtb):
        xg = a + dtb
        return -jnp.exp(alog) * (jnp.maximum(xg, 0.0) + jnp.log1p(jnp.exp(-jnp.abs(xg))))

    gates = g_ref[...]
    beta_all = jax.nn.sigmoid(gates)
    g_all = log_decay(gates, alog_ref[...], dtb_ref[...])
    g_all_t = log_decay(gt_ref[...], alog_c_ref[...], dtb_c_ref[...])
    lower = incl[0].astype(BF16)
    upper = incl[1].astype(BF16)
    backward_rows = lax.broadcasted_iota(jnp.int32, (GATE_W, n), 0) >= 3 * DN_HEADS

    def tri_sums(a, left):
        rem, pre_sum, suf_sum = a, 0.0, 0.0
        for _ in range(3):
            piece = rem.astype(BF16)
            rem = rem - piece.astype(F32)
            if left:
                pre_sum = pre_sum + jnp.dot(lower, piece, preferred_element_type=F32)
                suf_sum = suf_sum + jnp.dot(upper, piece, preferred_element_type=F32)
            else:
                pre_sum = pre_sum + jnp.dot(piece, upper, preferred_element_type=F32)
                suf_sum = suf_sum + jnp.dot(piece, lower, preferred_element_type=F32)
        return pre_sum, suf_sum

    pre_blocks, suf_blocks = [], []
    for blk in range(n_blk):
        rs = slice(blk * n, (blk + 1) * n)
        p, s_ = tri_sums(g_all[rs, :], left=True)
        pre_blocks.append(p)
        suf_blocks.append(s_)
        p_t, s_t = tri_sums(g_all_t[:, rs], left=False)
        sums = jnp.where(backward_rows, s_t, p_t)
        for r in range(2 * DN_HEADS):
            gct_scr[r, :, rs] = sums[2 * DN_HEADS + r:2 * DN_HEADS + r + 1, :]
    pre = jnp.concatenate(pre_blocks, axis=0)
    suf = jnp.concatenate(suf_blocks, axis=0)

    qs, kn, vv, beta, gc = [], [], [], [], []
    for hh in range(heads):
        cs = slice(hh * LANES, (hh + 1) * LANES)
        qs.append(l2n(conv_silu(q_ref[:, cs], cq_ref[:, cs])) * (DN_DK ** -0.5))
        kn.append(l2n(conv_silu(k_ref[:, cs], ck_ref[:, cs])))
        vv.append(conv_silu(v_ref[:, cs], cv_ref[:, cs]))
        beta.append((col(beta_all, h0 + hh), col(beta_all, DN_HEADS + h0 + hh)))
        gc.append((col(pre, 2 * DN_HEADS + h0 + hh), col(suf, 3 * DN_HEADS + h0 + hh)))

    kk, qk = {}, {}
    for hh in range(heads):
        for blk in range(n_blk):
            rs = slice(blk * n, (blk + 1) * n)
            kk[hh, blk] = _dot_nt(kn[hh][rs], kn[hh][rs])
            qk[hh, blk] = _dot_nt(qs[hh][rs], kn[hh][rs])

    m, m_hi, t, a_in = {}, {}, {}, {}
    for u in units:
        hh, blk, d = u
        rs = slice(blk * n, (blk + 1) * n)
        g_lanes = gct_scr[d * DN_HEADS + h0 + hh, :, rs]
        decay = jnp.where(incl[d], jnp.exp(gc[hh][d][rs] - g_lanes), 0.0)
        m[u] = jnp.where(strict[d], beta[hh][d][rs] * kk[hh, blk] * decay, 0.0)
        m_hi[u] = m[u].astype(BF16)
        a_in[u] = (qk[hh, blk] * decay).astype(BF16)
        t[u] = eye - jnp.where(link[1, d], m[u], 0.0)

    for s in levels[1:]:
        tb, x = {}, {}
        for u in units:
            tb[u] = t[u].astype(BF16)
            c_s = jnp.where(link[s, u[2]], m_hi[u], jnp.zeros_like(m_hi[u]))
            x[u] = jnp.dot(c_s, tb[u], preferred_element_type=F32).astype(BF16)
        for u in units:
            t[u] = t[u] - jnp.dot(tb[u], x[u], preferred_element_type=F32)

    uw, q_dec, k_dec, g_last = {}, {}, {}, {}
    for u in units:
        hh, blk, d = u
        rs = slice(blk * n, (blk + 1) * n)
        gcol = gc[hh][d][rs]
        bcol = beta[hh][d][rs]
        e_g = jnp.exp(gcol)
        kb = kn[hh][rs]
        uw[u] = _dot(t[u], jnp.concatenate([vv[hh][rs] * bcol, kb * (bcol * e_g)], axis=1))
        g_last[u] = gcol[n - 1:n] if d == 0 else gcol[0:1]
        q_dec[u] = qs[hh][rs] * e_g
        k_dec[u] = kb * jnp.exp(g_last[u] - gcol)

    state = {}
    for hh in range(heads):
        for d in (0, 1):
            state[hh, d] = jnp.zeros((DN_DK, DN_DK), F32) if zero_init else s0_ref[d, hh]
    outs = {}
    for step in range(n_blk):
        chains = [(hh, step if d == 0 else n_blk - 1 - step, d) for hh in range(heads) for d in (0, 1)]
        ws_qs = {}
        for u in chains:
            ws_qs[u] = _dot(jnp.concatenate([uw[u][:, LANES:], q_dec[u]], axis=0), state[u[0], u[2]])
        for u in chains:
            v_new = (uw[u][:, :LANES] - ws_qs[u][0:n]).astype(BF16)
            outs[u] = ws_qs[u][n:] + jnp.dot(a_in[u], v_new, preferred_element_type=F32)
            state[u[0], u[2]] = state[u[0], u[2]] * jnp.exp(g_last[u]) + _dot_tn(k_dec[u], v_new)

    for hh in range(heads):
        cs = slice(hh * LANES, (hh + 1) * LANES)
        o = (jnp.concatenate([outs[hh, blk, 0] for blk in range(n_blk)], axis=0)
             + jnp.concatenate([outs[hh, blk, 1] for blk in range(n_blk)], axis=0))
        o = o * lax.rsqrt(jnp.mean(o * o, axis=-1, keepdims=True) + EPS) * nw_ref[...]
        o_ref[:, cs] = (o * _silu(z_ref[:, cs])).astype(o_ref.dtype)
        if state_slot == 0:
            st_ref[0, 0, hh] = state[hh, 0]
            st_ref[0, 1, hh] = state[hh, 1]
            st_ref[1:, :, hh] = jnp.zeros((N_EVEN - 1, 2, DN_DK, DN_DK), F32)
        elif state_slot:
            st_ref[0, hh] = state[hh, 0]
            st_ref[1, hh] = state[hh, 1]


def _deltanet(qkvz, gates, gates_t, conv_w, e, alog, dtb, norm_w, s0, seq, heads, n_seq, row_block0,
              state_slot=None, states=None):
    zero_init = s0 is None
    width = heads * LANES
    per = DN_HEADS // heads
    alog_row = jnp.concatenate([jnp.zeros((2 * DN_HEADS,), F32), alog.reshape(-1)])[None]
    dtb_row = jnp.concatenate([jnp.zeros((2 * DN_HEADS,), F32), dtb.reshape(-1)])[None]

    def pcol(cb):
        return pl.BlockSpec((seq, width), lambda b, h: (row_block0 + b, cb * per + h))

    def ccol(cb):
        return pl.BlockSpec((None, 3, width), lambda b, h: (e, 0, cb * per + h))

    grow = pl.BlockSpec((1, GATE_W), lambda b, h: (0, 0))
    gcol = pl.BlockSpec((GATE_W, 1), lambda b, h: (0, 0))
    in_specs = [pcol(COL_Q), pcol(COL_K), pcol(COL_V), pcol(COL_Z),
                pl.BlockSpec((seq, GATE_W), lambda b, h: (row_block0 + b, 0)),
                pl.BlockSpec((GATE_W, seq), lambda b, h: (0, row_block0 + b)),
                ccol(0), ccol(1), ccol(2), grow, grow, gcol, gcol,
                pl.BlockSpec((None, 1, LANES), lambda b, h: (e, 0, 0))]
    args = [qkvz, qkvz, qkvz, qkvz, gates, gates_t, conv_w, conv_w, conv_w, alog_row, dtb_row,
            alog_row.T, dtb_row.T, norm_w.reshape(N_EVEN, 1, LANES)]
    if not zero_init:
        in_specs.append(pl.BlockSpec((None, None, 2, heads, DN_DK, DN_DK), lambda b, h: (b, e, 0, h, 0, 0)))
        args.append(s0)
    aliases = {}
    if state_slot:
        aliases = {len(args): 1}
        in_specs.append(pl.BlockSpec(memory_space=pl.ANY))
        args.append(states)
    out_specs = [pl.BlockSpec((seq, width), lambda b, h: (b, h))]
    out_shape = [jax.ShapeDtypeStruct((n_seq * seq, B_W), BF16)]
    if state_slot is not None:
        if state_slot == 0:
            out_specs.append(pl.BlockSpec((None, N_EVEN, 2, heads, DN_DK, DN_DK), lambda b, h: (b, 0, 0, h, 0, 0)))
        else:
            out_specs.append(pl.BlockSpec((None, None, 2, heads, DN_DK, DN_DK),
                                          lambda b, h: (b, state_slot, 0, h, 0, 0)))
        out_shape.append(jax.ShapeDtypeStruct((n_seq, N_EVEN, 2, DN_HEADS, DN_DK, DN_DK), F32))
    res = pl.pallas_call(
        functools.partial(_dn_kernel, seq=seq, heads=heads, zero_init=zero_init, state_slot=state_slot),
        grid=(n_seq, per),
        in_specs=in_specs,
        out_specs=out_specs,
        out_shape=out_shape,
        scratch_shapes=[pltpu.VMEM((2 * DN_HEADS, 1, seq), F32)],
        input_output_aliases=aliases,
        compiler_params=_params("arbitrary", "arbitrary"),
        name=f"deltanet_{seq}",
    )(*args)
    return res if state_slot is not None else (res[0], None)


def _oddmix_kernel(u_ref, v_ref, f_ref, nw_ref, ws_ref, bt_ref, ch_hi_ref, ch_lo_ref, ctx_hi_ref, ctx_lo_ref,
                   lat_hi_ref, lat_lo_ref, w_ref, o_ref):
    ctx = pl.program_id(0) < CTX_TILES

    def fourier(windows):
        fcs = _dot3(_split(f_ref[...]), (ch_hi_ref[...], ch_lo_ref[...]))
        wb = w_ref[...].astype(BF16)
        for rows, hi_ref, lo_ref in windows:
            stacked = jnp.concatenate([fcs[rows, :F_W], fcs[rows, F_W:]], axis=0)
            spec = _dot3((hi_ref[...], lo_ref[...]), _split(stacked))
            o_ref[rows, C_W:] = jnp.dot(spec.astype(BF16), wb, preferred_element_type=F32).astype(o_ref.dtype)

    def gating():
        u = jax.nn.gelu(u_ref[...])
        v = jax.nn.gelu(v_ref[...])
        mu = jnp.mean(v, axis=-1, keepdims=True)
        vc = v - mu
        var = jnp.mean(vc * vc, axis=-1, keepdims=True)
        vn = vc * lax.rsqrt(var + EPS) * nw_ref[...]
        bt = bt_ref[...]
        for c in range(TM // SGU_CHUNK):
            rs = slice(c * SGU_CHUNK, (c + 1) * SGU_CHUNK)
            for hd in range(SGU_HEADS):
                cs = slice(hd * LANES, (hd + 1) * LANES)
                mixed = _dot(ws_ref[hd], vn[rs, cs]) + bt[:, hd:hd + 1]
                o_ref[rs, cs] = (u[rs, cs] * mixed).astype(o_ref.dtype)

    @pl.when(ctx)
    def _():
        fourier([(slice(s * SEQ, (s + 1) * SEQ), ctx_hi_ref, ctx_lo_ref) for s in range(TM // SEQ)])
        gating()

    @pl.when(jnp.logical_not(ctx))
    def _():
        gating()
        fourier([(slice(0, DEC_SEQ), lat_hi_ref, lat_lo_ref)])


def _oddmix(uv, f_all, norm_w, w_s, b_t, tables, w_bd):
    const = lambda a: pl.BlockSpec(a.shape, lambda i: (0,) * a.ndim)
    return pl.pallas_call(
        _oddmix_kernel,
        grid=(N_ROW_TILES,),
        in_specs=([pl.BlockSpec((TM, C_W), lambda i: (i, 0)),
                   pl.BlockSpec((TM, C_W), lambda i: (i, 1)),
                   pl.BlockSpec((TM, F_W), lambda i: (i, 0)),
                   const(norm_w), const(w_s), const(b_t)] + [const(t) for t in tables] + [const(w_bd)]),
        out_specs=pl.BlockSpec((TM, C_W + F_W), lambda i: (i, 0)),
        out_shape=jax.ShapeDtypeStruct((N_TOK, C_W + F_W), BF16),
        compiler_params=_params("arbitrary"),
        name="oddmix",
    )(uv, uv, f_all, norm_w, w_s, b_t, *tables, w_bd)


def _grid_pos_embed(n_tok):
    rows = n_tok // GRID_W
    r = np.repeat(np.arange(rows, dtype=np.float64), GRID_W)
    col = np.tile(np.arange(GRID_W, dtype=np.float64), rows)
    quarter = D_MODEL // 4
    freq = np.exp(-math.log(10000.0) * np.arange(quarter, dtype=np.float64) / quarter)

    def emb(p):
        ang = p[:, None] * freq[None, :]
        return np.concatenate([np.sin(ang), np.cos(ang)], axis=-1)

    return np.concatenate([emb(r), emb(col)], axis=-1).astype(np.float32)


def _dft_tables(n):
    k = np.arange(n, dtype=np.int64)
    ang = ((k[:, None] * k[None, :]) % n).astype(np.float64) * (2.0 * math.pi / n)
    scale = n ** -0.5
    return np.cos(ang) * scale, np.sin(ang) * scale


def _host_split(a):
    hi = a.astype(np.float32).astype(BF16)
    lo = (a - hi.astype(np.float64)).astype(np.float32).astype(BF16)
    return hi, lo


def _fnet_tables():
    cc, sc = _dft_tables(FN_CH)
    eye = np.eye(F_W // FN_CH)
    tables = list(_host_split(np.concatenate([np.kron(eye, cc), np.kron(eye, sc)], axis=1)))
    for n in (SEQ, DEC_SEQ):
        cn, sn = _dft_tables(n)
        tables += list(_host_split(np.concatenate([cn, -sn], axis=1)))
    return tables


def _block_diag(blocks):
    g, a, b = blocks.shape
    eye = jnp.eye(g, dtype=blocks.dtype)
    return (eye[:, None, :, None] * blocks[:, :, None, :]).reshape(g * a, g * b)


def kernel(x_prompt, x_sample, state_delta, c, c_ctx, ffn1_norm, ffn1_w_gate, ffn1_w_up, ffn1_w_down,
           mix_norm, ffn2_norm, ffn2_w_gate, ffn2_w_up, ffn2_w_down, ada_w, ada_b, ev_w_in, ev_w_out,
           pool_w, pool_scale, dn_conv_w, dn_a_log, dn_dt_bias, dn_norm_w, od_w_in, od_w_out, sgu_norm,
           sgu_w, sgu_b, fnet_w, final_norm):
    cond8 = jnp.zeros((SUBLANES, D_MODEL), F32).at[0].set(c_ctx).at[1:1 + DEC_BATCH].set(c)
    mods = _ada(cond8, ada_w, ada_b)
    mods = mods[:, :1 + DEC_BATCH].reshape(DEPTH, 1 + DEC_BATCH, N_MOD, 1, D_MODEL).transpose(0, 2, 1, 3, 4)

    fnet_tables = _fnet_tables()
    groups = ((SEQ, BATCH, 0, DN_HEADS_PER_STEP[SEQ]),
              (DEC_SEQ, DEC_BATCH, CTX_TOK // DEC_SEQ, DN_HEADS_PER_STEP[DEC_SEQ]))
    even_segments = ((A_W, A_W + 4 * QK_W), (0, A_W), (A_W + 4 * QK_W, P_EVEN))
    odd_segments = ((0, 2 * C_W), (2 * C_W, P_ODD))

    x = (x_prompt.reshape(CTX_TOK, D_MODEL), x_sample.reshape(LAT_TOK, D_MODEL), _grid_pos_embed(DEC_SEQ))
    ev_w_in_t = jnp.swapaxes(ev_w_in, 1, 2)
    states = None
    for layer in range(DEPTH):
        x = _ffn(x, mods, layer, 0, ffn1_norm, ffn1_w_gate, ffn1_w_up, ffn1_w_down)
        if layer % 2 == 0:
            e = layer // 2
            qkvz, p_pool, gates, gates_t = _mixin(x, mods, layer, mix_norm, ev_w_in_t, e, even_segments,
                                                  TM_EVEN_PROJ, transposed=True)
            w_bd = _block_diag(pool_w[e])
            ya = _pool(p_pool, w_bd, pool_scale[e][None])
            yb = []
            for seq, n_seq, rb0, dn_heads in groups:
                ctx = rb0 == 0
                o, st = _deltanet(qkvz, gates, gates_t, dn_conv_w, e, dn_a_log[e], dn_dt_bias[e], dn_norm_w,
                                  None if ctx else state_delta, seq, dn_heads, n_seq, rb0,
                                  state_slot=e if ctx else None, states=states)
                yb.append(o)
                if ctx:
                    states = st
            mix = (ya, tuple(yb), ev_w_out, e)
        else:
            j = layer // 2
            uv, f_all = _mixin(x, mods, layer, mix_norm, od_w_in, j, odd_segments, TM)
            y_cd = _oddmix(uv, f_all, sgu_norm[j][None], sgu_w[j], sgu_b[j].T, fnet_tables, _block_diag(fnet_w[j]))
            mix = (y_cd, None, od_w_out, j)
        x = _ffn(x, mods, layer, 2, ffn2_norm, ffn2_w_gate, ffn2_w_up, ffn2_w_down, mix=mix,
                 final_norm=final_norm if layer == DEPTH - 1 else None)

    y_prompt, y_sample = x
    return (y_prompt.reshape(BATCH, SEQ, D_MODEL), y_sample.reshape(DEC_BATCH, DEC_SEQ, D_MODEL), states)
```
